```python
import jax
import jax.numpy as jnp
from jax import lax
import numpy as np

D_MODEL = 1024
BATCH = 4
SEQ = 8192
DEPTH = 2

CHUNK = 64
Q_BLOCK = 128
N_A_LAYERS = DEPTH // 2
N_B_LAYERS = DEPTH - N_A_LAYERS
CONV_WIDTH = 31
N_HEADS = 8
QK_NOPE_DIM = 128
QK_ROPE_DIM = 64
QK_DIM = QK_NOPE_DIM + QK_ROPE_DIM
V_HEAD_DIM = 128
KV_LORA_RANK = D_MODEL // 4
Q_LORA_RANK = 3 * KV_LORA_RANK // 2
ROPE_THETA = 10000.0
SOFTMAX_SCALE = QK_DIM ** -0.5
N_GROUPS = 4
EXPERTS_PER_GROUP = 8
N_EXPERTS = N_GROUPS * EXPERTS_PER_GROUP
TOP_K_IN_GROUP = 2
D_EXPERT = D_MODEL // 2
EPS = 1e-6
MASK_VALUE = -1e30

kernel_name = "conv_mla_yoco_hier_moe_block"


def rms_norm(x, g):
    xf = x.astype(jnp.float32)
    y = xf * lax.rsqrt(jnp.mean(xf * xf, axis=-1, keepdims=True) + EPS)
    return (y * g.astype(jnp.float32)).astype(x.dtype)


def layer_norm(x, g, b):
    xf = x.astype(jnp.float32)
    mu = jnp.mean(xf, axis=-1, keepdims=True)
    var = jnp.mean(jnp.square(xf - mu), axis=-1, keepdims=True)
    y = (xf - mu) * lax.rsqrt(var + EPS)
    return (y * g.astype(jnp.float32) + b.astype(jnp.float32)).astype(x.dtype)


def rope_tables(positions):
    inv_freq = ROPE_THETA ** (-jnp.arange(0, QK_ROPE_DIM, 2, dtype=jnp.float32) / QK_ROPE_DIM)
    ang = positions.astype(jnp.float32)[..., None] * inv_freq
    return jnp.cos(ang)[:, :, None, :], jnp.sin(ang)[:, :, None, :]


def rope_tail(x, cos, sin):
    x_nope = x[..., :QK_NOPE_DIM]
    xr = x[..., QK_NOPE_DIM:].astype(jnp.float32)
    x1, x2 = jnp.split(xr, 2, axis=-1)
    rot = jnp.concatenate([x1 * cos - x2 * sin, x1 * sin + x2 * cos], axis=-1)
    return jnp.concatenate([x_nope, rot.astype(x.dtype)], axis=-1)


def conformer_conv_module(h, norm_g, pw1_w, pw1_b, dw_w, dw_b, ln_g, ln_b, pw2_w, pw2_b):
    hn = rms_norm(h, norm_g)
    a, gate = jnp.split(hn @ pw1_w + pw1_b, 2, axis=-1)
    u = a * jax.nn.sigmoid(gate)
    u = lax.conv_general_dilated(
        u, dw_w[:, None, :].astype(u.dtype), window_strides=(1,),
        padding=[(CONV_WIDTH - 1, 0)],
        dimension_numbers=('NWC', 'WIO', 'NWC'),
        feature_group_count=D_MODEL) + dw_b
    u = jax.nn.silu(layer_norm(u, ln_g, ln_b))
    return u @ pw2_w + pw2_b


def shared_mla_kv(h, kv_norm_g, w_dkv, kv_latent_g, w_kr, w_ukv, k_norm_g, cos, sin):
    B, S, _ = h.shape
    hn = rms_norm(h, kv_norm_g)
    c_kv = rms_norm(hn @ w_dkv, kv_latent_g)
    kv = (c_kv @ w_ukv).reshape(B, S, N_HEADS, QK_NOPE_DIM + V_HEAD_DIM)
    k_nope, v = kv[..., :QK_NOPE_DIM], kv[..., QK_NOPE_DIM:]
    k_rope = hn @ w_kr
    k = jnp.concatenate(
        [k_nope, jnp.broadcast_to(k_rope[:, :, None, :], (B, S, N_HEADS, QK_ROPE_DIM))], axis=-1)
    k = rope_tail(rms_norm(k, k_norm_g), cos, sin)
    return k, v


def chunk_causal_attention(q, k, v):
    B, S, H, Dqk = q.shape
    n_blk = S // Q_BLOCK
    qb = q.reshape(B, n_blk, Q_BLOCK, H, Dqk).transpose(1, 0, 3, 2, 4)
    key_chunk = jnp.arange(S) // CHUNK

    def one_block(args):
        q_blk, blk = args
        q_chunk = (blk * Q_BLOCK + jnp.arange(Q_BLOCK)) // CHUNK
        allowed = key_chunk[None, :] <= q_chunk[:, None]
        s = jnp.einsum('bhqd,bkhd->bhqk', q_blk, k).astype(jnp.float32) * SOFTMAX_SCALE
        s = jnp.where(allowed, s, MASK_VALUE)
        p = jax.nn.softmax(s, axis=-1).astype(v.dtype)
        return jnp.einsum('bhqk,bkhd->bqhd', p, v)

    out = lax.map(one_block, (qb, jnp.arange(n_blk)))
    return out.transpose(1, 0, 2, 3, 4).reshape(B, S, H, v.shape[-1])


def mla_attention_layer(h, k, v, norm_g, w_dq, q_latent_g, w_uq, q_norm_g, w_o, cos, sin):
    B, S, _ = h.shape
    hn = rms_norm(h, norm_g)
    c_q = rms_norm(hn @ w_dq, q_latent_g)
    q = (c_q @ w_uq).reshape(B, S, N_HEADS, QK_DIM)
    q = rope_tail(rms_norm(q, q_norm_g), cos, sin)
    o = chunk_causal_attention(q, k, v)
    return o.reshape(B, S, N_HEADS * V_HEAD_DIM) @ w_o


def hierarchical_moe(h, norm_g, w_group, b_group, w_router, b_router, w_gate_up, w_down):
    B, S, D = h.shape
    xt = rms_norm(h, norm_g).reshape(B * S, D)
    g_prob = jax.nn.softmax((xt @ w_group + b_group).astype(jnp.float32), axis=-1)
    g_idx = jnp.argmax(g_prob, axis=-1)
    g_w = jnp.take_along_axis(g_prob, g_idx[:, None], axis=-1)
    e_logits = (xt @ w_router + b_router).astype(jnp.float32).reshape(-1, N_GROUPS, EXPERTS_PER_GROUP)
    e_logits = jnp.take_along_axis(e_logits, g_idx[:, None, None], axis=1)[:, 0]
    e_prob = jax.nn.softmax(e_logits, axis=-1)
    top_p, top_i = lax.top_k(e_prob, TOP_K_IN_GROUP)
    top_p = top_p / jnp.sum(top_p, axis=-1, keepdims=True)
    expert_id = g_idx[:, None] * EXPERTS_PER_GROUP + top_i
    gates = jnp.sum(jax.nn.one_hot(expert_id, N_EXPERTS, dtype=jnp.float32)
                    * (g_w * top_p)[..., None], axis=1).astype(xt.dtype)

    def expert_step(acc, ex):
        w_gu, w_d, g = ex
        a, b = jnp.split(xt @ w_gu, 2, axis=-1)
        return acc + ((jax.nn.silu(a) * b) @ w_d) * g[:, None], None

    y, _ = lax.scan(expert_step, jnp.zeros_like(xt), (w_gate_up, w_down, gates.T))
    return y.reshape(B, S, D)


def setup_inputs(seed: int = 0) -> dict:
    key = jax.random.key(seed)
    ks = iter(jax.random.split(key, 40))

    def nrm(shape, scale):
        return jax.random.normal(next(ks), shape, jnp.float32) * scale

    def gain(shape):
        return 1.0 + nrm(shape, 0.05)

    D, NA, NB, L = D_MODEL, N_A_LAYERS, N_B_LAYERS, DEPTH
    x = nrm((BATCH, SEQ, D), 1.0)
    offsets = jax.random.randint(next(ks), (BATCH, 1), 0, 64) * CHUNK
    positions = (jnp.arange(SEQ, dtype=jnp.int32)[None, :] + offsets).astype(jnp.int32)
    return {
        "x": x,
        "positions": positions,
        "a_norm_g": gain((NA, D)),
        "a_pw1_w": nrm((NA, D, 2 * D), D ** -0.5),
        "a_pw1_b": nrm((NA, 2 * D), 0.01),
        "a_dw_w": nrm((NA, CONV_WIDTH, D), CONV_WIDTH ** -0.5),
        "a_dw_b": nrm((NA, D), 0.01),
        "a_ln_g": gain((NA, D)),
        "a_ln_b": nrm((NA, D), 0.01),
        "a_pw2_w": nrm((NA, D, D), D ** -0.5),
        "a_pw2_b": nrm((NA, D), 0.01),
        "kv_norm_g": gain((D,)),
        "w_dkv": nrm((D, KV_LORA_RANK), D ** -0.5),
        "kv_latent_g": gain((KV_LORA_RANK,)),
        "w_kr": nrm((D, QK_ROPE_DIM), D ** -0.5),
        "w_ukv": nrm((KV_LORA_RANK, N_HEADS * (QK_NOPE_DIM + V_HEAD_DIM)), KV_LORA_RANK ** -0.5),
        "k_norm_g": gain((QK_DIM,)),
        "b_norm_g": gain((NB, D)),
        "w_dq": nrm((NB, D, Q_LORA_RANK), D ** -0.5),
        "q_latent_g": gain((NB, Q_LORA_RANK)),
        "w_uq": nrm((NB, Q_LORA_RANK, N_HEADS * QK_DIM), Q_LORA_RANK ** -0.5),
        "q_norm_g": gain((NB, QK_DIM)),
        "w_o": nrm((NB, N_HEADS * V_HEAD_DIM, D), (N_HEADS * V_HEAD_DIM) ** -0.5),
        "ffn_norm_g": gain((L, D)),
        "w_group": nrm((L, D, N_GROUPS), D ** -0.5),
        "b_group": nrm((L, N_GROUPS), 0.01),
        "w_router": nrm((L, D, N_EXPERTS), D ** -0.5),
        "b_router": nrm((L, N_EXPERTS), 0.01),
        "w_gate_up": nrm((L, N_EXPERTS, D, 2 * D_EXPERT), D ** -0.5),
        "w_down": nrm((L, N_EXPERTS, D_EXPERT, D), D_EXPERT ** -0.5),
    }


def reference(x, positions, a_norm_g, a_pw1_w, a_pw1_b, a_dw_w, a_dw_b, a_ln_g, a_ln_b, a_pw2_w, a_pw2_b,
              kv_norm_g, w_dkv, kv_latent_g, w_kr, w_ukv, k_norm_g,
              b_norm_g, w_dq, q_latent_g, w_uq, q_norm_g, w_o,
              ffn_norm_g, w_group, b_group, w_router, b_router, w_gate_up, w_down):
    cos, sin = rope_tables(positions)
    h = x
    k_shared = None
    v_shared = None
    for layer in range(DEPTH):
        if layer < N_A_LAYERS:
            i = layer
            h = h + conformer_conv_module(h, a_norm_g[i], a_pw1_w[i], a_pw1_b[i], a_dw_w[i], a_dw_b[i],
                                          a_ln_g[i], a_ln_b[i], a_pw2_w[i], a_pw2_b[i])
        else:
            i = layer - N_A_LAYERS
            if i == 0:
                k_shared, v_shared = shared_mla_kv(h, kv_norm_g, w_dkv, kv_latent_g, w_kr, w_ukv,
                                                   k_norm_g, cos, sin)
            h = h + mla_attention_layer(h, k_shared, v_shared, b_norm_g[i], w_dq[i], q_latent_g[i],
                                        w_uq[i], q_norm_g[i], w_o[i], cos, sin)
        h = h + hierarchical_moe(h, ffn_norm_g[layer], w_group[layer], b_group[layer], w_router[layer],
                                 b_router[layer], w_gate_up[layer], w_down[layer])
    return h
```

```python
import functools

import jax
import jax.numpy as jnp
from jax import lax
from jax.experimental import pallas as pl
from jax.experimental.pallas import tpu as pltpu

F32 = jnp.float32
BF16 = jnp.bfloat16
I32 = jnp.int32

EPS = 1e-6
CHUNK = 64
N_HEADS = 8
QK_NOPE_DIM = 128
QK_ROPE_DIM = 64
QK_DIM = QK_NOPE_DIM + QK_ROPE_DIM
V_HEAD_DIM = 128
ROPE_THETA = 10000.0
SOFTMAX_SCALE = QK_DIM ** -0.5
MASK_VALUE = -1e30
N_GROUPS = 4
EXPERTS_PER_GROUP = 8
N_EXPERTS = N_GROUPS * EXPERTS_PER_GROUP

LANES = 128
HEAD_PAD = 256
CONV_HALO = 32
CONV_ROWS = 64
TOKEN_TILE = 512
MOE_TILE = 256
ATTN_TILE = 512
VMEM_LIMIT = 56 * 1024 * 1024


def _cparams(sem):
    return pltpu.CompilerParams(dimension_semantics=sem, vmem_limit_bytes=VMEM_LIMIT)


def _rms(x, g):
    return x * lax.rsqrt(jnp.mean(x * x, axis=-1, keepdims=True) + EPS) * g


def _dot(a, b):
    return jnp.dot(a, b, preferred_element_type=F32)


def _split_bf16(x):
    hi = x.astype(BF16)
    lo = (x - hi.astype(F32)).astype(BF16)
    return hi, lo


def _full(shape):
    return pl.BlockSpec(shape, lambda *_: (0,) * len(shape))


def _rope_kernel(pos_ref, freq_ref, cos_ref, sin_ref):
    ang = pos_ref[...] * freq_ref[...]
    cos_ref[...] = jnp.cos(ang)
    sin_ref[...] = jnp.sin(ang)


def _rope_tables(positions):
    t = positions.size
    half = QK_ROPE_DIM // 2
    per_row = LANES // half
    inv_freq = ROPE_THETA ** (-jnp.arange(0, QK_ROPE_DIM, 2, dtype=F32) / QK_ROPE_DIM)
    pos = jnp.repeat(positions.reshape(t).astype(F32), half).reshape(t // per_row, LANES)
    freq = jnp.tile(inv_freq, per_row).reshape(1, LANES)
    rows = t // per_row
    blk = min(rows, 1024)
    cos, sin = pl.pallas_call(
        _rope_kernel,
        grid=(rows // blk,),
        in_specs=[pl.BlockSpec((blk, LANES), lambda i: (i, 0)), _full((1, LANES))],
        out_specs=[pl.BlockSpec((blk, LANES), lambda i: (i, 0))] * 2,
        out_shape=[jax.ShapeDtypeStruct((rows, LANES), F32)] * 2,
        compiler_params=_cparams(("arbitrary",)),
        name="rope_tables",
    )(pos, freq)
    cos = cos.reshape(t, half)
    sin = sin.reshape(t, half)
    zeros = jnp.zeros((t, LANES - QK_ROPE_DIM), F32)
    cos_tab = jnp.concatenate([cos, cos, zeros], axis=1)
    sin_tab = jnp.concatenate([-sin, sin, zeros], axis=1)
    return cos_tab, sin_tab


def _route(h, fng, wr_hi, wr_lo, br, tri, cnt_ref):
    tm = h.shape[0]
    xt = _rms(h, fng)
    x_hi, x_lo = _split_bf16(xt)
    logits = _dot(x_hi, wr_hi) + _dot(x_lo, wr_hi) + _dot(x_hi, wr_lo) + br
    lane = lax.broadcasted_iota(I32, (tm, LANES), 1).astype(F32)
    neg = -jnp.inf
    big = float(LANES)

    gl = jnp.where(lane >= N_EXPERTS, jnp.where(lane < N_EXPERTS + N_GROUPS, logits, neg), neg)
    gmax = jnp.max(gl, axis=1, keepdims=True)
    gidx = jnp.min(jnp.where(gl == gmax, lane, big), axis=1, keepdims=True) - N_EXPERTS
    g_w = 1.0 / jnp.sum(jnp.exp(gl - gmax), axis=1, keepdims=True)

    lo_lane = gidx * EXPERTS_PER_GROUP
    el = jnp.where(lane >= lo_lane, jnp.where(lane < lo_lane + EXPERTS_PER_GROUP, logits, neg), neg)
    m1 = jnp.max(el, axis=1, keepdims=True)
    i1 = jnp.min(jnp.where(el == m1, lane, big), axis=1, keepdims=True)
    el2 = jnp.where(lane == i1, neg, el)
    m2 = jnp.max(el2, axis=1, keepdims=True)
    i2 = jnp.min(jnp.where(el2 == m2, lane, big), axis=1, keepdims=True)
    p2 = jnp.exp(m2 - m1)
    den = 1.0 + p2
    w1 = g_w / den
    w2 = g_w * p2 / den

    sel1 = lane == i1
    sel2 = lane == i2
    onehot = jnp.where(sel1, 1.0, jnp.where(sel2, 1.0, 0.0))
    before = _dot(tri, onehot.astype(BF16)) + cnt_ref[...]
    r1 = jnp.sum(jnp.where(sel1, before, 0.0), axis=1, keepdims=True)
    r2 = jnp.sum(jnp.where(sel2, before, 0.0), axis=1, keepdims=True)
    cnt_ref[...] += jnp.sum(onehot, axis=0, keepdims=True)

    route = jnp.where(lane == 0, i1, jnp.where(lane == 1, i2, jnp.where(lane == 2, w1, jnp.where(
        lane == 3, w2, jnp.where(lane == 4, r1, jnp.where(lane == 5, r2, 0.0))))))
    return xt, route


def _router_weights(w_group, b_group, w_router, b_router):
    d = w_group.shape[0]
    pad = LANES - N_EXPERTS - N_GROUPS
    w = jnp.concatenate([w_router, w_group, jnp.zeros((d, pad), F32)], axis=1)
    b = jnp.concatenate([b_router, b_group, jnp.zeros((pad,), F32)]).reshape(1, LANES)
    hi, lo = _split_bf16(w)
    return hi, lo, b


def _strict_lower(n):
    r = lax.broadcasted_iota(I32, (n, n), 0)
    c = lax.broadcasted_iota(I32, (n, n), 1)
    return (c < r).astype(BF16)


def _conv_kernel(x_ref, ng_ref, w1_ref, b1_ref, dw_ref, dwb_ref, lng_ref, lnb_ref, w2_ref, b2_ref,
                 fng_ref, wrh_ref, wrl_ref, br_ref, tri_ref,
                 h_ref, xt_ref, route_ref, cnt_out_ref,
                 ubuf, cbuf, cnt_ref, *, tiles_per_seq, width):
    i = pl.program_id(0)
    tm, d = x_ref.shape
    n_strips = d // LANES

    @pl.when(i == 0)
    def _():
        cnt_ref[...] = jnp.zeros_like(cnt_ref)

    @pl.when(i % tiles_per_seq == 0)
    def _():
        ubuf[:, 0:CONV_HALO, :] = jnp.zeros((n_strips, CONV_HALO, LANES), F32)

    @pl.when(i % tiles_per_seq != 0)
    def _():
        ubuf[:, 0:CONV_HALO, :] = ubuf[:, tm:tm + CONV_HALO, :]

    x = x_ref[...]
    hn = _rms(x, ng_ref[...])
    ag = _dot(hn.astype(BF16), w1_ref[...]) + b1_ref[...]
    u = ag[:, :d] * jax.nn.sigmoid(ag[:, d:])
    for c in range(n_strips):
        ubuf[c, CONV_HALO:CONV_HALO + tm, :] = u[:, c * LANES:(c + 1) * LANES]

    first = CONV_HALO - (width - 1)
    for c in range(n_strips):
        cols = slice(c * LANES, (c + 1) * LANES)

        def chunk(r, carry, c=c, cols=cols):
            base = pl.multiple_of(r * CONV_ROWS, CONV_ROWS)
            acc = jnp.zeros((CONV_ROWS, LANES), F32)
            for t in range(width):
                acc = acc + dw_ref[t:t + 1, cols] * ubuf[c, pl.ds(base + first + t, CONV_ROWS), :]
            cbuf[pl.ds(base, CONV_ROWS), cols] = acc
            return carry

        lax.fori_loop(0, tm // CONV_ROWS, chunk, 0)

    v = cbuf[...] + dwb_ref[...]
    mu = jnp.mean(v, axis=-1, keepdims=True)
    vc = v - mu
    var = jnp.mean(vc * vc, axis=-1, keepdims=True)
    y = vc * lax.rsqrt(var + EPS) * lng_ref[...] + lnb_ref[...]
    y = y * jax.nn.sigmoid(y)
    h = x + _dot(y.astype(BF16), w2_ref[...]) + b2_ref[...]
    h_ref[...] = h

    xt, route = _route(h, fng_ref[...], wrh_ref[...], wrl_ref[...], br_ref[...], tri_ref[...], cnt_ref)
    xt_ref[...] = xt
    route_ref[...] = route
    cnt_out_ref[...] = jnp.broadcast_to(cnt_ref[...], cnt_out_ref.shape)


def _conv_layer(x, seq, ng, w1, b1, dw, dwb, lng, lnb, w2, b2, fng, wrh, wrl, br):
    t, d = x.shape
    tm = min(TOKEN_TILE, seq)
    width = dw.shape[0]
    dw_p = jnp.concatenate([dw, jnp.zeros((CONV_HALO - width, d), F32)], axis=0)
    row = lambda a: a.reshape(1, -1)
    tok = lambda w: pl.BlockSpec((tm, w), lambda i: (i, 0))
    kern = functools.partial(_conv_kernel, tiles_per_seq=seq // tm, width=width)
    return pl.pallas_call(
        kern,
        grid=(t // tm,),
        in_specs=[tok(d), _full((1, d)), _full((d, 2 * d)), _full((1, 2 * d)), _full((CONV_HALO, d)),
                  _full((1, d)), _full((1, d)), _full((1, d)), _full((d, d)), _full((1, d)),
                  _full((1, d)), _full((d, LANES)), _full((d, LANES)), _full((1, LANES)), _full((tm, tm))],
        out_specs=[tok(d), tok(d), tok(LANES), _full((8, LANES))],
        out_shape=[jax.ShapeDtypeStruct((t, d), F32), jax.ShapeDtypeStruct((t, d), F32),
                   jax.ShapeDtypeStruct((t, LANES), F32), jax.ShapeDtypeStruct((8, LANES), F32)],
        scratch_shapes=[pltpu.VMEM((d // LANES, CONV_HALO + tm, LANES), F32), pltpu.VMEM((tm, d), F32),
                        pltpu.VMEM((1, LANES), F32)],
        compiler_params=_cparams(("arbitrary",)),
        name="conv_router",
    )(x, row(ng), w1.astype(BF16), row(b1), dw_p, row(dwb), row(lng), row(lnb), w2.astype(BF16), row(b2),
      row(fng), wrh, wrl, br, _strict_lower(tm))


def _issue_rows(idx_ref, src_hbm, dst, sem, n):
    def body(r, carry):
        tok = idx_ref[0, 0, r]
        pltpu.make_async_copy(src_hbm.at[pl.ds(tok, 1)], dst.at[pl.ds(r, 1)], sem).start()
        return carry

    lax.fori_loop(0, n, body, 0)


def _wait_rows(src_hbm, dst, sem, n):
    pltpu.make_async_copy(src_hbm.at[pl.ds(0, n)], dst, sem).wait()


def _moe_kernel(te_ref, nu_ref, inv_ref, inv_next_ref, xt_hbm, wgu_ref, wd_ref, y_ref,
                xbuf, sems, wgu_bf, wd_bf):
    j = pl.program_id(0)
    n_used = nu_ref[0]
    tm = y_ref.shape[0]
    de = wd_ref.shape[1]
    slot = j % 2

    @pl.when(j == 0)
    def _():
        _issue_rows(inv_ref, xt_hbm, xbuf.at[0], sems.at[0], tm)

    @pl.when(j + 1 < n_used)
    def _():
        _issue_rows(inv_next_ref, xt_hbm, xbuf.at[1 - slot], sems.at[1 - slot], tm)

    @pl.when(j < n_used)
    def _():
        prev = te_ref[jnp.maximum(j - 1, 0)]

        @pl.when(jnp.logical_or(j == 0, te_ref[j] != prev))
        def _():
            wgu_bf[...] = wgu_ref[0].astype(BF16)
            wd_bf[...] = wd_ref[0].astype(BF16)

        _wait_rows(xt_hbm, xbuf.at[slot], sems.at[slot], tm)
        x = xbuf[slot].astype(BF16)
        gu = _dot(x, wgu_bf[...])
        a = gu[:, :de]
        b = gu[:, de:]
        mid = (a * jax.nn.sigmoid(a) * b).astype(BF16)
        y_ref[...] = _dot(mid, wd_bf[...])

    @pl.when(j >= n_used)
    def _():
        y_ref[...] = jnp.zeros_like(y_ref)


def _moe_plan(route, cnt, n_tok):
    tm = MOE_TILE
    n_tiles = (2 * n_tok) // tm + N_EXPERTS
    e1 = route[:, 0].astype(I32)
    e2 = route[:, 1].astype(I32)
    r1 = route[:, 4].astype(I32)
    r2 = route[:, 5].astype(I32)
    counts = cnt[0, :N_EXPERTS].astype(I32)
    padded = ((counts + tm - 1) // tm) * tm
    ends = jnp.cumsum(padded)
    base = ends - padded
    pos1 = base[e1] + r1
    pos2 = base[e2] + r2
    n_used = ends[-1] // tm
    tile_start = jnp.arange(n_tiles, dtype=I32) * tm
    tile_start = jnp.minimum(tile_start, ends[-1] - tm)
    tile_expert = jnp.sum((tile_start[:, None] >= ends[None, :]).astype(I32), axis=1)
    tok = jnp.arange(n_tok, dtype=I32)
    inv = jnp.zeros((n_tiles * tm,), I32).at[pos1].set(tok).at[pos2].set(tok)
    return dict(pos1=pos1, pos2=pos2, inv=inv.reshape(n_tiles, 1, tm), tile_expert=tile_expert,
                n_used=n_used.reshape(1).astype(I32), n_tiles=n_tiles)


def _moe_experts(xt, plan, w_gate_up, w_down):
    t, d = xt.shape
    tm = MOE_TILE
    n_tiles = plan["n_tiles"]
    de = w_down.shape[1]
    grid_spec = pltpu.PrefetchScalarGridSpec(
        num_scalar_prefetch=2,
        grid=(n_tiles,),
        in_specs=[
            pl.BlockSpec((1, 1, tm), lambda j, te, nu: (j, 0, 0), memory_space=pltpu.SMEM),
            pl.BlockSpec((1, 1, tm), lambda j, te, nu: (jnp.minimum(j + 1, n_tiles - 1), 0, 0),
                         memory_space=pltpu.SMEM),
            pl.BlockSpec(memory_space=pl.ANY),
            pl.BlockSpec((1, d, 2 * de), lambda j, te, nu: (te[j], 0, 0)),
            pl.BlockSpec((1, de, d), lambda j, te, nu: (te[j], 0, 0)),
        ],
        out_specs=pl.BlockSpec((tm, d), lambda j, te, nu: (j, 0)),
        scratch_shapes=[pltpu.VMEM((2, tm, d), F32), pltpu.SemaphoreType.DMA((2,)),
                        pltpu.VMEM((d, 2 * de), BF16), pltpu.VMEM((de, d), BF16)],
    )
    return pl.pallas_call(
        _moe_kernel,
        grid_spec=grid_spec,
        out_shape=jax.ShapeDtypeStruct((n_tiles * tm, d), F32),
        compiler_params=_cparams(("arbitrary",)),
        name="moe_experts",
    )(plan["tile_expert"], plan["n_used"], plan["inv"], plan["inv"], xt, w_gate_up, w_down)


def _combine_kernel(p1_ref, p2_ref, p1n_ref, p2n_ref, h_ref, route_ref, y_hbm, o_ref, buf1, buf2, sems):
    i = pl.program_id(0)
    n = pl.num_programs(0)
    tm = o_ref.shape[0]
    slot = i % 2

    @pl.when(i == 0)
    def _():
        _issue_rows(p1_ref, y_hbm, buf1.at[0], sems.at[0, 0], tm)
        _issue_rows(p2_ref, y_hbm, buf2.at[0], sems.at[1, 0], tm)

    @pl.when(i + 1 < n)
    def _():
        _issue_rows(p1n_ref, y_hbm, buf1.at[1 - slot], sems.at[0, 1 - slot], tm)
        _issue_rows(p2n_ref, y_hbm, buf2.at[1 - slot], sems.at[1, 1 - slot], tm)

    _wait_rows(y_hbm, buf1.at[slot], sems.at[0, slot], tm)
    _wait_rows(y_hbm, buf2.at[slot], sems.at[1, slot], tm)
    route = route_ref[...]
    o_ref[...] = h_ref[...] + buf1[slot] * route[:, 2:3] + buf2[slot] * route[:, 3:4]


def _moe_combine(h, route, y, plan):
    t, d = h.shape
    tm = min(MOE_TILE, t)
    n = t // tm
    p1 = plan["pos1"].reshape(n, 1, tm)
    p2 = plan["pos2"].reshape(n, 1, tm)
    cur = pl.BlockSpec((1, 1, tm), lambda i: (i, 0, 0), memory_space=pltpu.SMEM)
    nxt = pl.BlockSpec((1, 1, tm), lambda i: (jnp.minimum(i + 1, n - 1), 0, 0), memory_space=pltpu.SMEM)
    return pl.pallas_call(
        _combine_kernel,
        grid=(n,),
        in_specs=[cur, cur, nxt, nxt, pl.BlockSpec((tm, d), lambda i: (i, 0)),
                  pl.BlockSpec((tm, LANES), lambda i: (i, 0)), pl.BlockSpec(memory_space=pl.ANY)],
        out_specs=pl.BlockSpec((tm, d), lambda i: (i, 0)),
        out_shape=jax.ShapeDtypeStruct((t, d), F32),
        scratch_shapes=[pltpu.VMEM((2, tm, d), F32), pltpu.VMEM((2, tm, d), F32),
                        pltpu.SemaphoreType.DMA((2, 2))],
        compiler_params=_cparams(("arbitrary",)),
        name="moe_combine",
    )(p1, p2, p1, p2, h, route, y)


def _moe_layer(h, xt, route, cnt, w_gate_up, w_down):
    plan = _moe_plan(route, cnt, h.shape[0])
    y = _moe_experts(xt, plan, w_gate_up, w_down)
    return _moe_combine(h, route, y, plan)


def _rope_block(x, cos_tab, sin_tab):
    lane = lax.broadcasted_iota(I32, x.shape, 1)
    half = QK_ROPE_DIM // 2
    up = pltpu.roll(x, LANES - half, 1)
    down = pltpu.roll(x, half, 1)
    partner = jnp.where(lane < half, up, jnp.where(lane < QK_ROPE_DIM, down, 0.0))
    return x * cos_tab + partner * sin_tab


def _proj_kernel(h_ref, cos_ref, sin_ref, kvg_ref, wdkv_ref, kvlg_ref, wkr_ref, wuk_ref, wuv_ref, kng_ref,
                 qng_ref, wdq_ref, qlg_ref, wuq_ref, qg_ref, hsum_ref, hexp_ref,
                 q_ref, k_ref, v_ref):
    h = h_ref[...]
    cos_tab = cos_ref[...]
    sin_tab = sin_ref[...]
    hsum = hsum_ref[...]
    hexp = hexp_ref[...]

    def head_sums(sq):
        hi, lo = _split_bf16(sq)
        return _dot(hi, hsum) + _dot(lo, hsum)

    def head_spread(val):
        hi, lo = _split_bf16(val)
        return _dot(hi, hexp) + _dot(lo, hexp)

    hn = _rms(h, kvg_ref[...]).astype(BF16)
    c_kv = _rms(_dot(hn, wdkv_ref[...]), kvlg_ref[...]).astype(BF16)
    kn = _dot(c_kv, wuk_ref[...])
    vv = _dot(c_kv, wuv_ref[...])
    kr = _dot(hn, wkr_ref[...])
    kng = kng_ref[...]
    ss = head_sums(kn * kn) + jnp.sum(kr * kr, axis=-1, keepdims=True)
    inv = lax.rsqrt(ss * (1.0 / QK_DIM) + EPS)
    spread = head_spread(inv)
    kr_rot = _rope_block(kr * kng[:, LANES:], cos_tab, sin_tab)
    for hd in range(N_HEADS):
        cols = slice(hd * LANES, (hd + 1) * LANES)
        f = spread[:, cols]
        k_ref[hd, :, 0:LANES] = (kn[:, cols] * f * kng[:, :LANES]).astype(BF16)
        k_ref[hd, :, LANES:HEAD_PAD] = (kr_rot * f).astype(BF16)
        v_ref[hd] = vv[:, cols].astype(BF16)

    hq = _rms(h, qng_ref[...]).astype(BF16)
    c_q = _rms(_dot(hq, wdq_ref[...]), qlg_ref[...]).astype(BF16)
    qn = _dot(c_q, wuq_ref[:, 0:N_HEADS * LANES])
    qr = _dot(c_q, wuq_ref[:, N_HEADS * LANES:2 * N_HEADS * LANES])
    qg = qg_ref[...]
    ssq = head_sums(qn * qn) + head_sums(qr * qr)
    invq = lax.rsqrt(ssq * (1.0 / QK_DIM) + EPS) * SOFTMAX_SCALE
    spreadq = head_spread(invq)
    for hd in range(N_HEADS):
        cols = slice(hd * LANES, (hd + 1) * LANES)
        f = spreadq[:, cols]
        q_ref[hd, :, 0:LANES] = (qn[:, cols] * f * qg[:, :LANES]).astype(BF16)
        rot = _rope_block(qr[:, cols] * qg[:, LANES:], cos_tab, sin_tab)
        q_ref[hd, :, LANES:HEAD_PAD] = (rot * f).astype(BF16)


def _head_major(w, per_head, lo, hi):
    k = w.shape[0]
    w3 = w.reshape(k, N_HEADS, per_head)[:, :, lo:hi]
    w3 = jnp.pad(w3, ((0, 0), (0, 0), (0, LANES - (hi - lo))))
    return w3.reshape(k, N_HEADS * LANES)


def _mla_project(h, seq, cos_tab, sin_tab, kv_norm_g, w_dkv, kv_latent_g, w_kr, w_ukv, k_norm_g,
                 b_norm_g, w_dq, q_latent_g, w_uq, q_norm_g):
    t, d = h.shape
    tm = min(TOKEN_TILE, seq)
    kv_rank = w_dkv.shape[1]
    q_rank = w_dq.shape[1]
    hw = N_HEADS * LANES
    row = lambda a: a.reshape(1, -1)
    wkr = jnp.pad(w_kr, ((0, 0), (0, LANES - QK_ROPE_DIM))).astype(BF16)
    wuk = _head_major(w_ukv, QK_NOPE_DIM + V_HEAD_DIM, 0, QK_NOPE_DIM).astype(BF16)
    wuv = _head_major(w_ukv, QK_NOPE_DIM + V_HEAD_DIM, QK_NOPE_DIM, QK_NOPE_DIM + V_HEAD_DIM).astype(BF16)
    wuq = jnp.concatenate([_head_major(w_uq, QK_DIM, 0, QK_NOPE_DIM),
                           _head_major(w_uq, QK_DIM, QK_NOPE_DIM, QK_DIM)], axis=1).astype(BF16)
    pad_gain = lambda g: jnp.pad(g, (0, HEAD_PAD - QK_DIM)).reshape(1, HEAD_PAD)
    head_of = jnp.arange(hw, dtype=I32) // LANES
    hsum = (head_of[:, None] == jnp.arange(LANES, dtype=I32)[None, :]).astype(BF16)
    hexp = hsum.T
    tok = lambda w: pl.BlockSpec((tm, w), lambda i: (i, 0))
    heads = lambda w: pl.BlockSpec((N_HEADS, tm, w), lambda i: (0, i, 0))
    return pl.pallas_call(
        _proj_kernel,
        grid=(t // tm,),
        in_specs=[tok(d), tok(LANES), tok(LANES), _full((1, d)), _full((d, kv_rank)), _full((1, kv_rank)),
                  _full((d, LANES)), _full((kv_rank, hw)), _full((kv_rank, hw)), _full((1, HEAD_PAD)),
                  _full((1, d)), _full((d, q_rank)), _full((1, q_rank)), _full((q_rank, 2 * hw)),
                  _full((1, HEAD_PAD)), _full((hw, LANES)), _full((LANES, hw))],
        out_specs=[heads(HEAD_PAD), heads(HEAD_PAD), heads(LANES)],
        out_shape=[jax.ShapeDtypeStruct((N_HEADS, t, HEAD_PAD), BF16),
                   jax.ShapeDtypeStruct((N_HEADS, t, HEAD_PAD), BF16),
                   jax.ShapeDtypeStruct((N_HEADS, t, LANES), BF16)],
        compiler_params=_cparams(("arbitrary",)),
        name="mla_project",
    )(h, cos_tab, sin_tab, row(kv_norm_g), w_dkv.astype(BF16), row(kv_latent_g), wkr, wuk, wuv,
      pad_gain(k_norm_g), row(b_norm_g), w_dq.astype(BF16), row(q_latent_g), wuq, pad_gain(q_norm_g),
      hsum, hexp)


def _attn_kernel(qi_ref, ki_ref, q_ref, k_ref, v_ref, o_ref, m_ref, l_ref, acc_ref):
    s_idx = pl.program_id(1)
    qi = qi_ref[s_idx]
    ki = ki_ref[s_idx]
    tq = q_ref.shape[1]
    tk = k_ref.shape[1]

    @pl.when(ki == 0)
    def _():
        m_ref[...] = jnp.full_like(m_ref, -jnp.inf)
        l_ref[...] = jnp.zeros_like(l_ref)
        acc_ref[...] = jnp.zeros_like(acc_ref)

    def step(masked):
        if masked:
            qc = lax.broadcasted_iota(I32, (tq, tk), 0) // CHUNK
            kc = lax.broadcasted_iota(I32, (tq, tk), 1) // CHUNK
            allowed = kc <= qc
        for hd in range(N_HEADS):
            s = lax.dot_general(q_ref[hd], k_ref[hd], (((1,), (1,)), ((), ())), preferred_element_type=F32)
            if masked:
                s = jnp.where(allowed, s, MASK_VALUE)
            m_prev = m_ref[hd]
            m_new = jnp.maximum(m_prev, jnp.max(s, axis=-1, keepdims=True))
            alpha = jnp.exp(m_prev - m_new)
            p = jnp.exp(s - m_new)
            l_ref[hd] = alpha * l_ref[hd] + jnp.sum(p, axis=-1, keepdims=True)
            acc_ref[hd] = alpha * acc_ref[hd] + _dot(p.astype(BF16), v_ref[hd])
            m_ref[hd] = m_new

    @pl.when(ki < qi)
    def _():
        step(False)

    @pl.when(ki == qi)
    def _():
        step(True)
        for hd in range(N_HEADS):
            o_ref[:, hd * V_HEAD_DIM:(hd + 1) * V_HEAD_DIM] = (acc_ref[hd] / l_ref[hd]).astype(o_ref.dtype)


def _attention(q, k, v, batch, seq):
    t = q.shape[1]
    tq = min(ATTN_TILE, seq)
    nq = seq // tq
    pairs = [(a, b) for a in range(nq) for b in range(a + 1)]
    qi_tab = jnp.array([p[0] for p in pairs], I32)
    ki_tab = jnp.array([p[1] for p in pairs], I32)
    grid_spec = pltpu.PrefetchScalarGridSpec(
        num_scalar_prefetch=2,
        grid=(batch, len(pairs)),
        in_specs=[
            pl.BlockSpec((N_HEADS, tq, HEAD_PAD), lambda b, s, qi, ki: (0, b * nq + qi[s], 0)),
            pl.BlockSpec((N_HEADS, tq, HEAD_PAD), lambda b, s, qi, ki: (0, b * nq + ki[s], 0)),
            pl.BlockSpec((N_HEADS, tq, V_HEAD_DIM), lambda b, s, qi, ki: (0, b * nq + ki[s], 0)),
        ],
        out_specs=pl.BlockSpec((tq, N_HEADS * V_HEAD_DIM), lambda b, s, qi, ki: (b * nq + qi[s], 0)),
        scratch_shapes=[pltpu.VMEM((N_HEADS, tq, 1), F32), pltpu.VMEM((N_HEADS, tq, 1), F32),
                        pltpu.VMEM((N_HEADS, tq, V_HEAD_DIM), F32)],
    )
    return pl.pallas_call(
        _attn_kernel,
        grid_spec=grid_spec,
        out_shape=jax.ShapeDtypeStruct((t, N_HEADS * V_HEAD_DIM), BF16),
        compiler_params=_cparams(("arbitrary", "arbitrary")),
        name="attention",
    )(qi_tab, ki_tab, q, k, v)


def _oproj_kernel(o_ref, h_ref, wo_ref, fng_ref, wrh_ref, wrl_ref, br_ref, tri_ref,
                  h_out_ref, xt_ref, route_ref, cnt_out_ref, cnt_ref):
    @pl.when(pl.program_id(0) == 0)
    def _():
        cnt_ref[...] = jnp.zeros_like(cnt_ref)

    h = h_ref[...] + _dot(o_ref[...], wo_ref[...])
    h_out_ref[...] = h
    xt, route = _route(h, fng_ref[...], wrh_ref[...], wrl_ref[...], br_ref[...], tri_ref[...], cnt_ref)
    xt_ref[...] = xt
    route_ref[...] = route
    cnt_out_ref[...] = jnp.broadcast_to(cnt_ref[...], cnt_out_ref.shape)


def _oproj_layer(o, h, seq, w_o, fng, wrh, wrl, br):
    t, d = h.shape
    tm = min(TOKEN_TILE, seq)
    tok = lambda w: pl.BlockSpec((tm, w), lambda i: (i, 0))
    return pl.pallas_call(
        _oproj_kernel,
        grid=(t // tm,),
        in_specs=[tok(o.shape[1]), tok(d), _full(w_o.shape), _full((1, d)), _full((d, LANES)),
                  _full((d, LANES)), _full((1, LANES)), _full((tm, tm))],
        out_specs=[tok(d), tok(d), tok(LANES), _full((8, LANES))],
        out_shape=[jax.ShapeDtypeStruct((t, d), F32), jax.ShapeDtypeStruct((t, d), F32),
                   jax.ShapeDtypeStruct((t, LANES), F32), jax.ShapeDtypeStruct((8, LANES), F32)],
        scratch_shapes=[pltpu.VMEM((1, LANES), F32)],
        compiler_params=_cparams(("arbitrary",)),
        name="oproj_router",
    )(o, h, w_o.astype(BF16), fng.reshape(1, d), wrh, wrl, br, _strict_lower(tm))


def kernel(x, positions, a_norm_g, a_pw1_w, a_pw1_b, a_dw_w, a_dw_b, a_ln_g, a_ln_b, a_pw2_w, a_pw2_b, kv_norm_g, w_dkv, kv_latent_g, w_kr, w_ukv, k_norm_g, b_norm_g, w_dq, q_latent_g, w_uq, q_norm_g, w_o, ffn_norm_g, w_group, b_group, w_router, b_router, w_gate_up, w_down):
    batch, seq, d = x.shape
    t = batch * seq
    assert a_norm_g.shape[0] == 1 and b_norm_g.shape[0] == 1 and ffn_norm_g.shape[0] == 2
    cos_tab, sin_tab = _rope_tables(positions)

    wrh, wrl, br = _router_weights(w_group[0], b_group[0], w_router[0], b_router[0])
    h, xt, route, cnt = _conv_layer(x.reshape(t, d), seq, a_norm_g[0], a_pw1_w[0], a_pw1_b[0], a_dw_w[0],
                                    a_dw_b[0], a_ln_g[0], a_ln_b[0], a_pw2_w[0], a_pw2_b[0],
                                    ffn_norm_g[0], wrh, wrl, br)
    h = _moe_layer(h, xt, route, cnt, w_gate_up[0], w_down[0])

    q, k, v = _mla_project(h, seq, cos_tab, sin_tab, kv_norm_g, w_dkv, kv_latent_g, w_kr, w_ukv, k_norm_g,
                           b_norm_g[0], w_dq[0], q_latent_g[0], w_uq[0], q_norm_g[0])
    o = _attention(q, k, v, batch, seq)
    wrh, wrl, br = _router_weights(w_group[1], b_group[1], w_router[1], b_router[1])
    h, xt, route, cnt = _oproj_layer(o, h, seq, w_o[0], ffn_norm_g[1], wrh, wrl, br)
    h = _moe_layer(h, xt, route, cnt, w_gate_up[1], w_down[1])
    return h.reshape(batch, seq, d)
```

```python
import functools

import jax
import jax.numpy as jnp
from jax import lax
from jax.experimental import pallas as pl
from jax.experimental.pallas import tpu as pltpu

F32 = jnp.float32
BF16 = jnp.bfloat16
I32 = jnp.int32

EPS = 1e-6
CHUNK = 64
N_HEADS = 8
QK_NOPE_DIM = 128
QK_ROPE_DIM = 64
QK_DIM = QK_NOPE_DIM + QK_ROPE_DIM
V_HEAD_DIM = 128
ROPE_THETA = 10000.0
SOFTMAX_SCALE = QK_DIM ** -0.5
MASK_VALUE = -1e30
N_GROUPS = 4
EXPERTS_PER_GROUP = 8
N_EXPERTS = N_GROUPS * EXPERTS_PER_GROUP

LANES = 128
HEAD_PAD = 256
CONV_HALO = 32
CONV_ROWS = 64
TOKEN_TILE = 512
MOE_TILE = 256
ATTN_TILE = 512
VMEM_LIMIT = 56 * 1024 * 1024


def _cparams(sem):
    return pltpu.CompilerParams(dimension_semantics=sem, vmem_limit_bytes=VMEM_LIMIT)


def _rms(x, g):
    return x * lax.rsqrt(jnp.mean(x * x, axis=-1, keepdims=True) + EPS) * g


def _dot(a, b):
    return jnp.dot(a, b, preferred_element_type=F32)


def _split_bf16(x):
    hi = x.astype(BF16)
    lo = (x - hi.astype(F32)).astype(BF16)
    return hi, lo


def _full(shape):
    return pl.BlockSpec(shape, lambda *_: (0,) * len(shape))


def _rope_kernel(pos_ref, freq_ref, cos_ref, sin_ref):
    ang = pos_ref[...] * freq_ref[...]
    cos_ref[...] = jnp.cos(ang)
    sin_ref[...] = jnp.sin(ang)


def _rope_tables(positions):
    t = positions.size
    half = QK_ROPE_DIM // 2
    per_row = LANES // half
    inv_freq = ROPE_THETA ** (-jnp.arange(0, QK_ROPE_DIM, 2, dtype=F32) / QK_ROPE_DIM)
    pos = jnp.repeat(positions.reshape(t).astype(F32), half).reshape(t // per_row, LANES)
    freq = jnp.tile(inv_freq, per_row).reshape(1, LANES)
    rows = t // per_row
    blk = min(rows, 1024)
    cos, sin = pl.pallas_call(
        _rope_kernel,
        grid=(rows // blk,),
        in_specs=[pl.BlockSpec((blk, LANES), lambda i: (i, 0)), _full((1, LANES))],
        out_specs=[pl.BlockSpec((blk, LANES), lambda i: (i, 0))] * 2,
        out_shape=[jax.ShapeDtypeStruct((rows, LANES), F32)] * 2,
        compiler_params=_cparams(("arbitrary",)),
        name="rope_tables",
    )(pos, freq)
    cos = cos.reshape(t, half)
    sin = sin.reshape(t, half)
    zeros = jnp.zeros((t, LANES - QK_ROPE_DIM), F32)
    cos_tab = jnp.concatenate([cos, cos, zeros], axis=1)
    sin_tab = jnp.concatenate([-sin, sin, zeros], axis=1)
    return cos_tab, sin_tab


def _route(h, fng, wr_hi, wr_lo, br, tri, cnt_ref):
    tm = h.shape[0]
    xt = _rms(h, fng)
    x_hi, x_lo = _split_bf16(xt)
    logits = _dot(x_hi, wr_hi) + _dot(x_lo, wr_hi) + _dot(x_hi, wr_lo) + br
    lane = lax.broadcasted_iota(I32, (tm, LANES), 1).astype(F32)
    neg = -jnp.inf
    big = float(LANES)

    gl = jnp.where(lane >= N_EXPERTS, jnp.where(lane < N_EXPERTS + N_GROUPS, logits, neg), neg)
    gmax = jnp.max(gl, axis=1, keepdims=True)
    gidx = jnp.min(jnp.where(gl == gmax, lane, big), axis=1, keepdims=True) - N_EXPERTS
    g_w = 1.0 / jnp.sum(jnp.exp(gl - gmax), axis=1, keepdims=True)

    lo_lane = gidx * EXPERTS_PER_GROUP
    el = jnp.where(lane >= lo_lane, jnp.where(lane < lo_lane + EXPERTS_PER_GROUP, logits, neg), neg)
    m1 = jnp.max(el, axis=1, keepdims=True)
    i1 = jnp.min(jnp.where(el == m1, lane, big), axis=1, keepdims=True)
    el2 = jnp.where(lane == i1, neg, el)
    m2 = jnp.max(el2, axis=1, keepdims=True)
    i2 = jnp.min(jnp.where(el2 == m2, lane, big), axis=1, keepdims=True)
    p2 = jnp.exp(m2 - m1)
    den = 1.0 + p2
    w1 = g_w / den
    w2 = g_w * p2 / den

    sel1 = lane == i1
    sel2 = lane == i2
    onehot = jnp.where(sel1, 1.0, jnp.where(sel2, 1.0, 0.0))
    before = _dot(tri, onehot.astype(BF16)) + cnt_ref[...]
    r1 = jnp.sum(jnp.where(sel1, before, 0.0), axis=1, keepdims=True)
    r2 = jnp.sum(jnp.where(sel2, before, 0.0), axis=1, keepdims=True)
    cnt_ref[...] += jnp.sum(onehot, axis=0, keepdims=True)

    route = jnp.where(lane == 0, i1, jnp.where(lane == 1, i2, jnp.where(lane == 2, w1, jnp.where(
        lane == 3, w2, jnp.where(lane == 4, r1, jnp.where(lane == 5, r2, 0.0))))))
    return xt, route


def _router_weights(w_group, b_group, w_router, b_router):
    d = w_group.shape[0]
    pad = LANES - N_EXPERTS - N_GROUPS
    w = jnp.concatenate([w_router, w_group, jnp.zeros((d, pad), F32)], axis=1)
    b = jnp.concatenate([b_router, b_group, jnp.zeros((pad,), F32)]).reshape(1, LANES)
    hi, lo = _split_bf16(w)
    return hi, lo, b


def _strict_lower(n):
    r = lax.broadcasted_iota(I32, (n, n), 0)
    c = lax.broadcasted_iota(I32, (n, n), 1)
    return (c < r).astype(BF16)


def _conv_kernel(x_ref, ng_ref, w1_ref, b1_ref, dw_ref, dwb_ref, lng_ref, lnb_ref, w2_ref, b2_ref,
                 fng_ref, wrh_ref, wrl_ref, br_ref, tri_ref,
                 h_ref, xt_ref, route_ref, cnt_out_ref,
                 ubuf, cbuf, cnt_ref, *, tiles_per_seq, width):
    i = pl.program_id(0)
    tm, d = x_ref.shape
    n_strips = d // LANES

    @pl.when(i == 0)
    def _():
        cnt_ref[...] = jnp.zeros_like(cnt_ref)

    @pl.when(i % tiles_per_seq == 0)
    def _():
        ubuf[:, 0:CONV_HALO, :] = jnp.zeros((n_strips, CONV_HALO, LANES), F32)

    @pl.when(i % tiles_per_seq != 0)
    def _():
        ubuf[:, 0:CONV_HALO, :] = ubuf[:, tm:tm + CONV_HALO, :]

    x = x_ref[...]
    hn = _rms(x, ng_ref[...])
    ag = _dot(hn.astype(BF16), w1_ref[...]) + b1_ref[...]
    u = ag[:, :d] * jax.nn.sigmoid(ag[:, d:])
    for c in range(n_strips):
        ubuf[c, CONV_HALO:CONV_HALO + tm, :] = u[:, c * LANES:(c + 1) * LANES]

    first = CONV_HALO - (width - 1)
    for c in range(n_strips):
        cols = slice(c * LANES, (c + 1) * LANES)

        def chunk(r, carry, c=c, cols=cols):
            base = pl.multiple_of(r * CONV_ROWS, CONV_ROWS)
            acc = jnp.zeros((CONV_ROWS, LANES), F32)
            for t in range(width):
                acc = acc + dw_ref[t:t + 1, cols] * ubuf[c, pl.ds(base + first + t, CONV_ROWS), :]
            cbuf[pl.ds(base, CONV_ROWS), cols] = acc
            return carry

        lax.fori_loop(0, tm // CONV_ROWS, chunk, 0)

    v = cbuf[...] + dwb_ref[...]
    mu = jnp.mean(v, axis=-1, keepdims=True)
    vc = v - mu
    var = jnp.mean(vc * vc, axis=-1, keepdims=True)
    y = vc * lax.rsqrt(var + EPS) * lng_ref[...] + lnb_ref[...]
    y = y * jax.nn.sigmoid(y)
    h = x + _dot(y.astype(BF16), w2_ref[...]) + b2_ref[...]
    h_ref[...] = h

    xt, route = _route(h, fng_ref[...], wrh_ref[...], wrl_ref[...], br_ref[...], tri_ref[...], cnt_ref)
    xt_ref[...] = xt
    route_ref[...] = route
    cnt_out_ref[...] = jnp.broadcast_to(cnt_ref[...], cnt_out_ref.shape)


def _conv_layer(x, seq, ng, w1, b1, dw, dwb, lng, lnb, w2, b2, fng, wrh, wrl, br):
    t, d = x.shape
    tm = min(TOKEN_TILE, seq)
    width = dw.shape[0]
    dw_p = jnp.concatenate([dw, jnp.zeros((CONV_HALO - width, d), F32)], axis=0)
    row = lambda a: a.reshape(1, -1)
    tok = lambda w: pl.BlockSpec((tm, w), lambda i: (i, 0))
    kern = functools.partial(_conv_kernel, tiles_per_seq=seq // tm, width=width)
    return pl.pallas_call(
        kern,
        grid=(t // tm,),
        in_specs=[tok(d), _full((1, d)), _full((d, 2 * d)), _full((1, 2 * d)), _full((CONV_HALO, d)),
                  _full((1, d)), _full((1, d)), _full((1, d)), _full((d, d)), _full((1, d)),
                  _full((1, d)), _full((d, LANES)), _full((d, LANES)), _full((1, LANES)), _full((tm, tm))],
        out_specs=[tok(d), tok(d), tok(LANES), _full((8, LANES))],
        out_shape=[jax.ShapeDtypeStruct((t, d), F32), jax.ShapeDtypeStruct((t, d), F32),
                   jax.ShapeDtypeStruct((t, LANES), F32), jax.ShapeDtypeStruct((8, LANES), F32)],
        scratch_shapes=[pltpu.VMEM((d // LANES, CONV_HALO + tm, LANES), F32), pltpu.VMEM((tm, d), F32),
                        pltpu.VMEM((1, LANES), F32)],
        compiler_params=_cparams(("arbitrary",)),
        name="conv_router",
    )(x, row(ng), w1.astype(BF16), row(b1), dw_p, row(dwb), row(lng), row(lnb), w2.astype(BF16), row(b2),
      row(fng), wrh, wrl, br, _strict_lower(tm))


def _issue_rows(idx_ref, src_hbm, dst, sem, n):
    def body(r, carry):
        tok = idx_ref[0, 0, r]
        pltpu.make_async_copy(src_hbm.at[pl.ds(tok, 1)], dst.at[pl.ds(r, 1)], sem).start()
        return carry

    lax.fori_loop(0, n, body, 0)


def _wait_rows(src_hbm, dst, sem, n):
    pltpu.make_async_copy(src_hbm.at[pl.ds(0, n)], dst, sem).wait()


def _moe_kernel(te_ref, nu_ref, inv_ref, inv_next_ref, xt_hbm, wgu_ref, wd_ref, y_ref,
                xbuf, sems, wgu_bf, wd_bf):
    j = pl.program_id(0)
    n_used = nu_ref[0]
    tm = y_ref.shape[0]
    de = wd_ref.shape[1]
    slot = j % 2

    @pl.when(j == 0)
    def _():
        _issue_rows(inv_ref, xt_hbm, xbuf.at[0], sems.at[0], tm)

    @pl.when(j + 1 < n_used)
    def _():
        _issue_rows(inv_next_ref, xt_hbm, xbuf.at[1 - slot], sems.at[1 - slot], tm)

    @pl.when(j < n_used)
    def _():
        prev = te_ref[jnp.maximum(j - 1, 0)]

        @pl.when(jnp.logical_or(j == 0, te_ref[j] != prev))
        def _():
            wgu_bf[...] = wgu_ref[0].astype(BF16)
            wd_bf[...] = wd_ref[0].astype(BF16)

        _wait_rows(xt_hbm, xbuf.at[slot], sems.at[slot], tm)
        x = xbuf[slot].astype(BF16)
        gu = _dot(x, wgu_bf[...])
        a = gu[:, :de]
        b = gu[:, de:]
        mid = (a * jax.nn.sigmoid(a) * b).astype(BF16)
        y_ref[...] = _dot(mid, wd_bf[...])

    @pl.when(j >= n_used)
    def _():
        y_ref[...] = jnp.zeros_like(y_ref)


def _moe_plan(route, cnt, n_tok):
    tm = MOE_TILE
    n_tiles = (2 * n_tok) // tm + N_EXPERTS
    e1 = route[:, 0].astype(I32)
    e2 = route[:, 1].astype(I32)
    r1 = route[:, 4].astype(I32)
    r2 = route[:, 5].astype(I32)
    counts = cnt[0, :N_EXPERTS].astype(I32)
    padded = ((counts + tm - 1) // tm) * tm
    ends = jnp.cumsum(padded)
    base = ends - padded
    pos1 = base[e1] + r1
    pos2 = base[e2] + r2
    n_used = ends[-1] // tm
    tile_start = jnp.arange(n_tiles, dtype=I32) * tm
    tile_start = jnp.minimum(tile_start, ends[-1] - tm)
    tile_expert = jnp.sum((tile_start[:, None] >= ends[None, :]).astype(I32), axis=1)
    tok = jnp.arange(n_tok, dtype=I32)
    inv = jnp.zeros((n_tiles * tm,), I32).at[pos1].set(tok).at[pos2].set(tok)
    return dict(pos1=pos1, pos2=pos2, inv=inv.reshape(n_tiles, 1, tm), tile_expert=tile_expert,
                n_used=n_used.reshape(1).astype(I32), n_tiles=n_tiles)


def _moe_experts(xt, plan, w_gate_up, w_down):
    t, d = xt.shape
    tm = MOE_TILE
    n_tiles = plan["n_tiles"]
    de = w_down.shape[1]
    grid_spec = pltpu.PrefetchScalarGridSpec(
        num_scalar_prefetch=2,
        grid=(n_tiles,),
        in_specs=[
            pl.BlockSpec((1, 1, tm), lambda j, te, nu: (j, 0, 0), memory_space=pltpu.SMEM),
            pl.BlockSpec((1, 1, tm), lambda j, te, nu: (jnp.minimum(j + 1, n_tiles - 1), 0, 0),
                         memory_space=pltpu.SMEM),
            pl.BlockSpec(memory_space=pl.ANY),
            pl.BlockSpec((1, d, 2 * de), lambda j, te, nu: (te[j], 0, 0)),
            pl.BlockSpec((1, de, d), lambda j, te, nu: (te[j], 0, 0)),
        ],
        out_specs=pl.BlockSpec((tm, d), lambda j, te, nu: (j, 0)),
        scratch_shapes=[pltpu.VMEM((2, tm, d), F32), pltpu.SemaphoreType.DMA((2,)),
                        pltpu.VMEM((d, 2 * de), BF16), pltpu.VMEM((de, d), BF16)],
    )
    return pl.pallas_call(
        _moe_kernel,
        grid_spec=grid_spec,
        out_shape=jax.ShapeDtypeStruct((n_tiles * tm, d), F32),
        compiler_params=_cparams(("arbitrary",)),
        name="moe_experts",
    )(plan["tile_expert"], plan["n_used"], plan["inv"], plan["inv"], xt, w_gate_up, w_down)


def _combine_kernel(p1_ref, p2_ref, p1n_ref, p2n_ref, h_ref, route_ref, y_hbm, o_ref, buf1, buf2, sems):
    i = pl.program_id(0)
    n = pl.num_programs(0)
    tm = o_ref.shape[0]
    slot = i % 2

    @pl.when(i == 0)
    def _():
        _issue_rows(p1_ref, y_hbm, buf1.at[0], sems.at[0, 0], tm)
        _issue_rows(p2_ref, y_hbm, buf2.at[0], sems.at[1, 0], tm)

    @pl.when(i + 1 < n)
    def _():
        _issue_rows(p1n_ref, y_hbm, buf1.at[1 - slot], sems.at[0, 1 - slot], tm)
        _issue_rows(p2n_ref, y_hbm, buf2.at[1 - slot], sems.at[1, 1 - slot], tm)

    _wait_rows(y_hbm, buf1.at[slot], sems.at[0, slot], tm)
    _wait_rows(y_hbm, buf2.at[slot], sems.at[1, slot], tm)
    route = route_ref[...]
    o_ref[...] = h_ref[...] + buf1[slot] * route[:, 2:3] + buf2[slot] * route[:, 3:4]


def _moe_combine(h, route, y, plan):
    t, d = h.shape
    tm = min(MOE_TILE, t)
    n = t // tm
    p1 = plan["pos1"].reshape(n, 1, tm)
    p2 = plan["pos2"].reshape(n, 1, tm)
    cur = pl.BlockSpec((1, 1, tm), lambda i: (i, 0, 0), memory_space=pltpu.SMEM)
    nxt = pl.BlockSpec((1, 1, tm), lambda i: (jnp.minimum(i + 1, n - 1), 0, 0), memory_space=pltpu.SMEM)
    return pl.pallas_call(
        _combine_kernel,
        grid=(n,),
        in_specs=[cur, cur, nxt, nxt, pl.BlockSpec((tm, d), lambda i: (i, 0)),
                  pl.BlockSpec((tm, LANES), lambda i: (i, 0)), pl.BlockSpec(memory_space=pl.ANY)],
        out_specs=pl.BlockSpec((tm, d), lambda i: (i, 0)),
        out_shape=jax.ShapeDtypeStruct((t, d), F32),
        scratch_shapes=[pltpu.VMEM((2, tm, d), F32), pltpu.VMEM((2, tm, d), F32),
                        pltpu.SemaphoreType.DMA((2, 2))],
        compiler_params=_cparams(("arbitrary",)),
        name="moe_combine",
    )(p1, p2, p1, p2, h, route, y)


def _moe_layer(h, xt, route, cnt, w_gate_up, w_down):
    plan = _moe_plan(route, cnt, h.shape[0])
    y = _moe_experts(xt, plan, w_gate_up, w_down)
    return _moe_combine(h, route, y, plan)


def _rope_block(x, cos_tab, sin_tab):
    lane = lax.broadcasted_iota(I32, x.shape, 1)
    half = QK_ROPE_DIM // 2
    up = pltpu.roll(x, LANES - half, 1)
    down = pltpu.roll(x, half, 1)
    partner = jnp.where(lane < half, up, jnp.where(lane < QK_ROPE_DIM, down, 0.0))
    return x * cos_tab + partner * sin_tab


def _proj_kernel(h_ref, cos_ref, sin_ref, kvg_ref, wdkv_ref, kvlg_ref, wkr_ref, wuk_ref, wuv_ref, kng_ref,
                 qng_ref, wdq_ref, qlg_ref, wuq_ref, qg_ref, hsum_ref, hexp_ref,
                 q_ref, k_ref, v_ref):
    h = h_ref[...]
    cos_tab = cos_ref[...]
    sin_tab = sin_ref[...]
    hsum = hsum_ref[...]
    hexp = hexp_ref[...]

    def head_sums(sq):
        hi, lo = _split_bf16(sq)
        return _dot(hi, hsum) + _dot(lo, hsum)

    def head_spread(val):
        hi, lo = _split_bf16(val)
        return _dot(hi, hexp) + _dot(lo, hexp)

    hn = _rms(h, kvg_ref[...]).astype(BF16)
    c_kv = _rms(_dot(hn, wdkv_ref[...]), kvlg_ref[...]).astype(BF16)
    kn = _dot(c_kv, wuk_ref[...])
    vv = _dot(c_kv, wuv_ref[...])
    kr = _dot(hn, wkr_ref[...])
    kng = kng_ref[...]
    ss = head_sums(kn * kn) + jnp.sum(kr * kr, axis=-1, keepdims=True)
    inv = lax.rsqrt(ss * (1.0 / QK_DIM) + EPS)
    spread = head_spread(inv)
    kr_rot = _rope_block(kr * kng[:, LANES:], cos_tab, sin_tab)
    for hd in range(N_HEADS):
        cols = slice(hd * LANES, (hd + 1) * LANES)
        f = spread[:, cols]
        k_ref[hd, :, 0:LANES] = (kn[:, cols] * f * kng[:, :LANES]).astype(BF16)
        k_ref[hd, :, LANES:HEAD_PAD] = (kr_rot * f).astype(BF16)
        v_ref[hd] = vv[:, cols].astype(BF16)

    hq = _rms(h, qng_ref[...]).astype(BF16)
    c_q = _rms(_dot(hq, wdq_ref[...]), qlg_ref[...]).astype(BF16)
    qn = _dot(c_q, wuq_ref[:, 0:N_HEADS * LANES])
    qr = _dot(c_q, wuq_ref[:, N_HEADS * LANES:2 * N_HEADS * LANES])
    qg = qg_ref[...]
    ssq = head_sums(qn * qn) + head_sums(qr * qr)
    invq = lax.rsqrt(ssq * (1.0 / QK_DIM) + EPS) * SOFTMAX_SCALE
    spreadq = head_spread(invq)
    for hd in range(N_HEADS):
        cols = slice(hd * LANES, (hd + 1) * LANES)
        f = spreadq[:, cols]
        q_ref[hd, :, 0:LANES] = (qn[:, cols] * f * qg[:, :LANES]).astype(BF16)
        rot = _rope_block(qr[:, cols] * qg[:, LANES:], cos_tab, sin_tab)
        q_ref[hd, :, LANES:HEAD_PAD] = (rot * f).astype(BF16)


def _head_major(w, per_head, lo, hi):
    k = w.shape[0]
    w3 = w.reshape(k, N_HEADS, per_head)[:, :, lo:hi]
    w3 = jnp.pad(w3, ((0, 0), (0, 0), (0, LANES - (hi - lo))))
    return w3.reshape(k, N_HEADS * LANES)


def _mla_project(h, seq, cos_tab, sin_tab, kv_norm_g, w_dkv, kv_latent_g, w_kr, w_ukv, k_norm_g,
                 b_norm_g, w_dq, q_latent_g, w_uq, q_norm_g):
    t, d = h.shape
    tm = min(TOKEN_TILE, seq)
    kv_rank = w_dkv.shape[1]
    q_rank = w_dq.shape[1]
    hw = N_HEADS * LANES
    row = lambda a: a.reshape(1, -1)
    wkr = jnp.pad(w_kr, ((0, 0), (0, LANES - QK_ROPE_DIM))).astype(BF16)
    wuk = _head_major(w_ukv, QK_NOPE_DIM + V_HEAD_DIM, 0, QK_NOPE_DIM).astype(BF16)
    wuv = _head_major(w_ukv, QK_NOPE_DIM + V_HEAD_DIM, QK_NOPE_DIM, QK_NOPE_DIM + V_HEAD_DIM).astype(BF16)
    wuq = jnp.concatenate([_head_major(w_uq, QK_DIM, 0, QK_NOPE_DIM),
                           _head_major(w_uq, QK_DIM, QK_NOPE_DIM, QK_DIM)], axis=1).astype(BF16)
    pad_gain = lambda g: jnp.pad(g, (0, HEAD_PAD - QK_DIM)).reshape(1, HEAD_PAD)
    head_of = jnp.arange(hw, dtype=I32) // LANES
    hsum = (head_of[:, None] == jnp.arange(LANES, dtype=I32)[None, :]).astype(BF16)
    hexp = hsum.T
    tok = lambda w: pl.BlockSpec((tm, w), lambda i: (i, 0))
    heads = lambda w: pl.BlockSpec((N_HEADS, tm, w), lambda i: (0, i, 0))
    return pl.pallas_call(
        _proj_kernel,
        grid=(t // tm,),
        in_specs=[tok(d), tok(LANES), tok(LANES), _full((1, d)), _full((d, kv_rank)), _full((1, kv_rank)),
                  _full((d, LANES)), _full((kv_rank, hw)), _full((kv_rank, hw)), _full((1, HEAD_PAD)),
                  _full((1, d)), _full((d, q_rank)), _full((1, q_rank)), _full((q_rank, 2 * hw)),
                  _full((1, HEAD_PAD)), _full((hw, LANES)), _full((LANES, hw))],
        out_specs=[heads(HEAD_PAD), heads(HEAD_PAD), heads(LANES)],
        out_shape=[jax.ShapeDtypeStruct((N_HEADS, t, HEAD_PAD), BF16),
                   jax.ShapeDtypeStruct((N_HEADS, t, HEAD_PAD), BF16),
                   jax.ShapeDtypeStruct((N_HEADS, t, LANES), BF16)],
        compiler_params=_cparams(("arbitrary",)),
        name="mla_project",
    )(h, cos_tab, sin_tab, row(kv_norm_g), w_dkv.astype(BF16), row(kv_latent_g), wkr, wuk, wuv,
      pad_gain(k_norm_g), row(b_norm_g), w_dq.astype(BF16), row(q_latent_g), wuq, pad_gain(q_norm_g),
      hsum, hexp)


def _attn_kernel(qi_ref, ki_ref, q_ref, k_ref, v_ref, o_ref, m_ref, acc_ref, vext_ref):
    s_idx = pl.program_id(1)
    qi = qi_ref[s_idx]
    ki = ki_ref[s_idx]
    tq = q_ref.shape[1]
    tk = k_ref.shape[1]
    dv = V_HEAD_DIM

    @pl.when(ki == 0)
    def _():
        m_ref[...] = jnp.full_like(m_ref, -jnp.inf)
        acc_ref[...] = jnp.zeros_like(acc_ref)
        vext_ref[:, :, dv:] = jnp.ones((N_HEADS, tk, LANES), BF16)

    vext_ref[:, :, :dv] = v_ref[...]

    def step(masked):
        if masked:
            qc = lax.broadcasted_iota(I32, (tq, tk), 0) // CHUNK
            kc = lax.broadcasted_iota(I32, (tq, tk), 1) // CHUNK
            allowed = kc <= qc
        for hd in range(N_HEADS):
            s = lax.dot_general(q_ref[hd], k_ref[hd], (((1,), (1,)), ((), ())), preferred_element_type=F32)
            if masked:
                s = jnp.where(allowed, s, MASK_VALUE)
            m_prev = m_ref[hd]
            m_new = jnp.maximum(m_prev, jnp.max(s, axis=-1, keepdims=True))
            alpha = jnp.exp(m_prev - m_new)
            p = jnp.exp(s - jnp.concatenate([m_new] * (tk // LANES), axis=1))
            pv = _dot(p.astype(BF16), vext_ref[hd])
            acc_ref[hd] = jnp.concatenate([alpha, alpha], axis=1) * acc_ref[hd] + pv
            m_ref[hd] = m_new

    @pl.when(ki < qi)
    def _():
        step(False)

    @pl.when(ki == qi)
    def _():
        step(True)
        for hd in range(N_HEADS):
            acc = acc_ref[hd]
            o_ref[:, hd * dv:(hd + 1) * dv] = (acc[:, :dv] / acc[:, dv:]).astype(o_ref.dtype)


def _attention(q, k, v, batch, seq):
    t = q.shape[1]
    tq = min(ATTN_TILE, seq)
    nq = seq // tq
    pairs = [(a, b) for a in range(nq) for b in range(a + 1)]
    qi_tab = jnp.array([p[0] for p in pairs], I32)
    ki_tab = jnp.array([p[1] for p in pairs], I32)
    grid_spec = pltpu.PrefetchScalarGridSpec(
        num_scalar_prefetch=2,
        grid=(batch, len(pairs)),
        in_specs=[
            pl.BlockSpec((N_HEADS, tq, HEAD_PAD), lambda b, s, qi, ki: (0, b * nq + qi[s], 0)),
            pl.BlockSpec((N_HEADS, tq, HEAD_PAD), lambda b, s, qi, ki: (0, b * nq + ki[s], 0)),
            pl.BlockSpec((N_HEADS, tq, V_HEAD_DIM), lambda b, s, qi, ki: (0, b * nq + ki[s], 0)),
        ],
        out_specs=pl.BlockSpec((tq, N_HEADS * V_HEAD_DIM), lambda b, s, qi, ki: (b * nq + qi[s], 0)),
        scratch_shapes=[pltpu.VMEM((N_HEADS, tq, LANES), F32),
                        pltpu.VMEM((N_HEADS, tq, V_HEAD_DIM + LANES), F32),
                        pltpu.VMEM((N_HEADS, tq, V_HEAD_DIM + LANES), BF16)],
    )
    return pl.pallas_call(
        _attn_kernel,
        grid_spec=grid_spec,
        out_shape=jax.ShapeDtypeStruct((t, N_HEADS * V_HEAD_DIM), BF16),
        compiler_params=_cparams(("arbitrary", "arbitrary")),
        name="attention",
    )(qi_tab, ki_tab, q, k, v)


def _oproj_kernel(o_ref, h_ref, wo_ref, fng_ref, wrh_ref, wrl_ref, br_ref, tri_ref,
                  h_out_ref, xt_ref, route_ref, cnt_out_ref, cnt_ref):
    @pl.when(pl.program_id(0) == 0)
    def _():
        cnt_ref[...] = jnp.zeros_like(cnt_ref)

    h = h_ref[...] + _dot(o_ref[...], wo_ref[...])
    h_out_ref[...] = h
    xt, route = _route(h, fng_ref[...], wrh_ref[...], wrl_ref[...], br_ref[...], tri_ref[...], cnt_ref)
    xt_ref[...] = xt
    route_ref[...] = route
    cnt_out_ref[...] = jnp.broadcast_to(cnt_ref[...], cnt_out_ref.shape)


def _oproj_layer(o, h, seq, w_o, fng, wrh, wrl, br):
    t, d = h.shape
    tm = min(TOKEN_TILE, seq)
    tok = lambda w: pl.BlockSpec((tm, w), lambda i: (i, 0))
    return pl.pallas_call(
        _oproj_kernel,
        grid=(t // tm,),
        in_specs=[tok(o.shape[1]), tok(d), _full(w_o.shape), _full((1, d)), _full((d, LANES)),
                  _full((d, LANES)), _full((1, LANES)), _full((tm, tm))],
        out_specs=[tok(d), tok(d), tok(LANES), _full((8, LANES))],
        out_shape=[jax.ShapeDtypeStruct((t, d), F32), jax.ShapeDtypeStruct((t, d), F32),
                   jax.ShapeDtypeStruct((t, LANES), F32), jax.ShapeDtypeStruct((8, LANES), F32)],
        scratch_shapes=[pltpu.VMEM((1, LANES), F32)],
        compiler_params=_cparams(("arbitrary",)),
        name="oproj_router",
    )(o, h, w_o.astype(BF16), fng.reshape(1, d), wrh, wrl, br, _strict_lower(tm))


def kernel(x, positions, a_norm_g, a_pw1_w, a_pw1_b, a_dw_w, a_dw_b, a_ln_g, a_ln_b, a_pw2_w, a_pw2_b, kv_norm_g, w_dkv, kv_latent_g, w_kr, w_ukv, k_norm_g, b_norm_g, w_dq, q_latent_g, w_uq, q_norm_g, w_o, ffn_norm_g, w_group, b_group, w_router, b_router, w_gate_up, w_down):
    batch, seq, d = x.shape
    t = batch * seq
    assert a_norm_g.shape[0] == 1 and b_norm_g.shape[0] == 1 and ffn_norm_g.shape[0] == 2
    cos_tab, sin_tab = _rope_tables(positions)

    wrh, wrl, br = _router_weights(w_group[0], b_group[0], w_router[0], b_router[0])
    h, xt, route, cnt = _conv_layer(x.reshape(t, d), seq, a_norm_g[0], a_pw1_w[0], a_pw1_b[0], a_dw_w[0],
                                    a_dw_b[0], a_ln_g[0], a_ln_b[0], a_pw2_w[0], a_pw2_b[0],
                                    ffn_norm_g[0], wrh, wrl, br)
    h = _moe_layer(h, xt, route, cnt, w_gate_up[0], w_down[0])

    q, k, v = _mla_project(h, seq, cos_tab, sin_tab, kv_norm_g, w_dkv, kv_latent_g, w_kr, w_ukv, k_norm_g,
                           b_norm_g[0], w_dq[0], q_latent_g[0], w_uq[0], q_norm_g[0])
    o = _attention(q, k, v, batch, seq)
    wrh, wrl, br = _router_weights(w_group[1], b_group[1], w_router[1], b_router[1])
    h, xt, route, cnt = _oproj_layer(o, h, seq, w_o[0], ffn_norm_g[1], wrh, wrl, br)
    h = _moe_layer(h, xt, route, cnt, w_gate_up[1], w_down[1])
    return h.reshape(batch, seq, d)
```

```python
import functools

import jax
import jax.numpy as jnp
from jax import lax
from jax.experimental import pallas as pl
from jax.experimental.pallas import tpu as pltpu

F32 = jnp.float32
BF16 = jnp.bfloat16
I32 = jnp.int32

EPS = 1e-6
CHUNK = 64
N_HEADS = 8
QK_NOPE_DIM = 128
QK_ROPE_DIM = 64
QK_DIM = QK_NOPE_DIM + QK_ROPE_DIM
V_HEAD_DIM = 128
ROPE_THETA = 10000.0
SOFTMAX_SCALE = QK_DIM ** -0.5
MASK_VALUE = -1e30
N_GROUPS = 4
EXPERTS_PER_GROUP = 8
N_EXPERTS = N_GROUPS * EXPERTS_PER_GROUP

LANES = 128
SUBLANES = 8
HEAD_PAD = 256
CONV_HALO = 32
CONV_ROWS = 64
TOKEN_TILE = 512
MOE_TILE = 256
ATTN_TILE = 512
VMEM_LIMIT = 56 * 1024 * 1024


def _cparams(sem):
    return pltpu.CompilerParams(dimension_semantics=sem, vmem_limit_bytes=VMEM_LIMIT)


def _rms(x, g):
    return x * lax.rsqrt(jnp.mean(x * x, axis=-1, keepdims=True) + EPS) * g


def _dot(a, b):
    return jnp.dot(a, b, preferred_element_type=F32)


def _split_bf16(x):
    hi = x.astype(BF16)
    lo = (x - hi.astype(F32)).astype(BF16)
    return hi, lo


def _full(shape):
    return pl.BlockSpec(shape, lambda *_: (0,) * len(shape))


def _local_rows(tm):
    return 2 * tm + N_EXPERTS * SUBLANES


def _rope_kernel(pos_ref, freq_ref, cos_ref, sin_ref):
    ang = pos_ref[...] * freq_ref[...]
    cos_ref[...] = jnp.cos(ang)
    sin_ref[...] = jnp.sin(ang)


def _rope_tables(positions):
    t = positions.size
    half = QK_ROPE_DIM // 2
    per_row = LANES // half
    inv_freq = ROPE_THETA ** (-jnp.arange(0, QK_ROPE_DIM, 2, dtype=F32) / QK_ROPE_DIM)
    pos = jnp.repeat(positions.reshape(t).astype(F32), half).reshape(t // per_row, LANES)
    freq = jnp.tile(inv_freq, per_row).reshape(1, LANES)
    rows = t // per_row
    blk = min(rows, 1024)
    cos, sin = pl.pallas_call(
        _rope_kernel,
        grid=(rows // blk,),
        in_specs=[pl.BlockSpec((blk, LANES), lambda i: (i, 0)), _full((1, LANES))],
        out_specs=[pl.BlockSpec((blk, LANES), lambda i: (i, 0))] * 2,
        out_shape=[jax.ShapeDtypeStruct((rows, LANES), F32)] * 2,
        compiler_params=_cparams(("arbitrary",)),
        name="rope_tables",
    )(pos, freq)
    cos = cos.reshape(t, half)
    sin = sin.reshape(t, half)
    zeros = jnp.zeros((t, LANES - QK_ROPE_DIM), F32)
    cos_tab = jnp.concatenate([cos, cos, zeros], axis=1)
    sin_tab = jnp.concatenate([-sin, sin, zeros], axis=1)
    return cos_tab, sin_tab


def _route(h, fng, wr_hi, wr_lo, br, tri, upper):
    tm = h.shape[0]
    xt = _rms(h, fng)
    x_hi, x_lo = _split_bf16(xt)
    logits = _dot(x_hi, wr_hi) + _dot(x_lo, wr_hi) + _dot(x_hi, wr_lo) + br
    lane = lax.broadcasted_iota(I32, (tm, LANES), 1).astype(F32)
    neg = -jnp.inf
    big = float(LANES)

    gl = jnp.where(lane >= N_EXPERTS, jnp.where(lane < N_EXPERTS + N_GROUPS, logits, neg), neg)
    gmax = jnp.max(gl, axis=1, keepdims=True)
    gidx = jnp.min(jnp.where(gl == gmax, lane, big), axis=1, keepdims=True) - N_EXPERTS
    g_w = 1.0 / jnp.sum(jnp.exp(gl - gmax), axis=1, keepdims=True)

    lo_lane = gidx * EXPERTS_PER_GROUP
    el = jnp.where(lane >= lo_lane, jnp.where(lane < lo_lane + EXPERTS_PER_GROUP, logits, neg), neg)
    m1 = jnp.max(el, axis=1, keepdims=True)
    i1 = jnp.min(jnp.where(el == m1, lane, big), axis=1, keepdims=True)
    el2 = jnp.where(lane == i1, neg, el)
    m2 = jnp.max(el2, axis=1, keepdims=True)
    i2 = jnp.min(jnp.where(el2 == m2, lane, big), axis=1, keepdims=True)
    p2 = jnp.exp(m2 - m1)
    den = 1.0 + p2
    w1 = g_w / den
    w2 = g_w * p2 / den

    sel1 = lane == i1
    sel2 = lane == i2
    onehot = jnp.where(sel1, 1.0, jnp.where(sel2, 1.0, 0.0))
    before = _dot(tri, onehot.astype(BF16))
    cnt = jnp.sum(onehot, axis=0, keepdims=True)
    groups = jnp.floor((cnt + (SUBLANES - 1)) * (1.0 / SUBLANES))
    start = SUBLANES * _dot(jnp.broadcast_to(groups, (SUBLANES, LANES)).astype(BF16), upper)[0:1]
    at = before + start
    l1 = jnp.sum(jnp.where(sel1, at, 0.0), axis=1, keepdims=True)
    l2 = jnp.sum(jnp.where(sel2, at, 0.0), axis=1, keepdims=True)

    route = jnp.where(lane == 0, i1, jnp.where(lane == 1, i2, jnp.where(lane == 2, w1, jnp.where(
        lane == 3, w2, jnp.where(lane == 4, l1, jnp.where(lane == 5, l2, 0.0))))))
    route_t = route.T
    rows = lax.broadcasted_iota(I32, (_local_rows(tm), tm), 0).astype(F32)
    perm = jnp.where(rows == route_t[4:5], 1.0, jnp.where(rows == route_t[5:6], 1.0, 0.0)).astype(BF16)
    return _dot(perm, x_hi), route, cnt


def _router_weights(w_group, b_group, w_router, b_router):
    d = w_group.shape[0]
    pad = LANES - N_EXPERTS - N_GROUPS
    w = jnp.concatenate([w_router, w_group, jnp.zeros((d, pad), F32)], axis=1)
    b = jnp.concatenate([b_router, b_group, jnp.zeros((pad,), F32)]).reshape(1, LANES)
    hi, lo = _split_bf16(w)
    return hi, lo, b


def _strict_lower(n):
    r = lax.broadcasted_iota(I32, (n, n), 0)
    c = lax.broadcasted_iota(I32, (n, n), 1)
    return (c < r).astype(BF16)


def _router_specs(tm, d):
    tok = lambda w: pl.BlockSpec((tm, w), lambda i: (i, 0))
    in_specs = [_full((1, d)), _full((d, LANES)), _full((d, LANES)), _full((1, LANES)), _full((tm, tm)),
                _full((LANES, LANES))]
    out_specs = [tok(d), pl.BlockSpec((_local_rows(tm), d), lambda i: (i, 0)), tok(LANES),
                 pl.BlockSpec((1, 1, LANES), lambda i: (i, 0, 0))]
    return in_specs, out_specs


def _router_out_shapes(t, tm, d):
    n = t // tm
    return [jax.ShapeDtypeStruct((t, d), F32), jax.ShapeDtypeStruct((n * _local_rows(tm), d), F32),
            jax.ShapeDtypeStruct((t, LANES), F32), jax.ShapeDtypeStruct((n, 1, LANES), F32)]


def _conv_kernel(x_ref, ng_ref, w1_ref, b1_ref, dw_ref, dwb_ref, lng_ref, lnb_ref, w2_ref, b2_ref,
                 fng_ref, wrh_ref, wrl_ref, br_ref, tri_ref, upper_ref,
                 h_ref, xl_ref, route_ref, cnt_ref,
                 ubuf, cbuf, *, tiles_per_seq, width):
    i = pl.program_id(0)
    tm, d = x_ref.shape
    n_strips = d // LANES

    @pl.when(i % tiles_per_seq == 0)
    def _():
        ubuf[:, 0:CONV_HALO, :] = jnp.zeros((n_strips, CONV_HALO, LANES), F32)

    @pl.when(i % tiles_per_seq != 0)
    def _():
        ubuf[:, 0:CONV_HALO, :] = ubuf[:, tm:tm + CONV_HALO, :]

    x = x_ref[...]
    hn = _rms(x, ng_ref[...])
    ag = _dot(hn.astype(BF16), w1_ref[...]) + b1_ref[...]
    u = ag[:, :d] * jax.nn.sigmoid(ag[:, d:])
    for c in range(n_strips):
        ubuf[c, CONV_HALO:CONV_HALO + tm, :] = u[:, c * LANES:(c + 1) * LANES]

    first = CONV_HALO - (width - 1)
    for c in range(n_strips):
        cols = slice(c * LANES, (c + 1) * LANES)

        def chunk(r, carry, c=c, cols=cols):
            base = pl.multiple_of(r * CONV_ROWS, CONV_ROWS)
            acc = jnp.zeros((CONV_ROWS, LANES), F32)
            for t in range(width):
                acc = acc + dw_ref[t:t + 1, cols] * ubuf[c, pl.ds(base + first + t, CONV_ROWS), :]
            cbuf[pl.ds(base, CONV_ROWS), cols] = acc
            return carry

        lax.fori_loop(0, tm // CONV_ROWS, chunk, 0)

    v = cbuf[...] + dwb_ref[...]
    mu = jnp.mean(v, axis=-1, keepdims=True)
    vc = v - mu
    var = jnp.mean(vc * vc, axis=-1, keepdims=True)
    y = vc * lax.rsqrt(var + EPS) * lng_ref[...] + lnb_ref[...]
    y = y * jax.nn.sigmoid(y)
    h = x + _dot(y.astype(BF16), w2_ref[...]) + b2_ref[...]
    h_ref[...] = h

    xl, route, cnt = _route(h, fng_ref[...], wrh_ref[...], wrl_ref[...], br_ref[...], tri_ref[...], upper_ref[...])
    xl_ref[...] = xl
    route_ref[...] = route
    cnt_ref[0] = cnt


def _conv_layer(x, seq, ng, w1, b1, dw, dwb, lng, lnb, w2, b2, fng, wrh, wrl, br):
    t, d = x.shape
    tm = min(TOKEN_TILE, seq)
    width = dw.shape[0]
    dw_p = jnp.concatenate([dw, jnp.zeros((CONV_HALO - width, d), F32)], axis=0)
    row = lambda a: a.reshape(1, -1)
    r_in, r_out = _router_specs(tm, d)
    kern = functools.partial(_conv_kernel, tiles_per_seq=seq // tm, width=width)
    return pl.pallas_call(
        kern,
        grid=(t // tm,),
        in_specs=[pl.BlockSpec((tm, d), lambda i: (i, 0)), _full((1, d)), _full((d, 2 * d)), _full((1, 2 * d)),
                  _full((CONV_HALO, d)), _full((1, d)), _full((1, d)), _full((1, d)), _full((d, d)),
                  _full((1, d))] + r_in,
        out_specs=r_out,
        out_shape=_router_out_shapes(t, tm, d),
        scratch_shapes=[pltpu.VMEM((d // LANES, CONV_HALO + tm, LANES), F32), pltpu.VMEM((tm, d), F32)],
        compiler_params=_cparams(("arbitrary",)),
        name="conv_router",
    )(x, row(ng), w1.astype(BF16), row(b1), dw_p, row(dwb), row(lng), row(lnb), w2.astype(BF16), row(b2),
      row(fng), wrh, wrl, br, _strict_lower(tm), _strict_lower(LANES).T)


def _moe_plan(cnt, tm):
    n_tt = cnt.shape[0]
    c = cnt[:, 0, :N_EXPERTS].astype(I32)
    run = (c + SUBLANES - 1) // SUBLANES
    local = jnp.cumsum(run, axis=1) - run
    per_tile = MOE_TILE // SUBLANES
    n_exp = jnp.sum(run, axis=0)
    n_pad = ((n_exp + per_tile - 1) // per_tile) * per_tile
    e_end = jnp.cumsum(n_pad)
    e_base = e_end - n_pad
    dst = e_base[None, :] + jnp.cumsum(run, axis=0) - run
    src = jnp.arange(n_tt, dtype=I32)[:, None] * (_local_rows(tm) // SUBLANES) + local
    max_groups = (2 * n_tt * tm) // SUBLANES + n_tt * N_EXPERTS + N_EXPERTS * (per_tile - 1)
    n_tiles = -(-max_groups // per_tile)
    tile_start = jnp.minimum(jnp.arange(n_tiles, dtype=I32) * per_tile, e_end[-1] - per_tile)
    tile_expert = jnp.sum((tile_start[:, None] >= e_end[None, :]).astype(I32), axis=1)
    return dict(run=run.reshape(-1), dst=dst.reshape(-1), src=src.reshape(-1),
                pad_start=(e_base + n_exp).astype(I32), pad_len=(n_pad - n_exp).astype(I32),
                tile_expert=tile_expert, n_used=(e_end[-1] // per_tile).reshape(1).astype(I32),
                n_tiles=n_tiles)


def _group(ref, g):
    return ref.at[pl.ds(pl.multiple_of(g * SUBLANES, SUBLANES), SUBLANES)]


def _dispatch_kernel(run_ref, dst_ref, src_ref, ps_ref, pn_ref, nu_ref, xl_hbm, zero_hbm, xs_hbm, sems):
    i = pl.program_id(0)
    n = pl.num_programs(0)
    zero_group = zero_hbm.at[pl.ds(0, SUBLANES)]

    def copy(s, d, sem):
        return pltpu.make_async_copy(_group(xl_hbm, s), _group(xs_hbm, d), sem)

    def tile_groups(tile):
        def body(e, tot):
            return tot + run_ref[tile * N_EXPERTS + e]
        return lax.fori_loop(0, N_EXPERTS, body, 0)

    def wait_tile(tile):
        def body(_, c):
            copy(0, 0, sems.at[tile % 2]).wait()
            return c
        lax.fori_loop(0, tile_groups(tile), body, 0)

    def issue_run(e, c):
        r = i * N_EXPERTS + e
        s0 = src_ref[r]
        d0 = dst_ref[r]

        def body(k, c2):
            copy(s0 + k, d0 + k, sems.at[i % 2]).start()
            return c2
        lax.fori_loop(0, run_ref[r], body, 0)
        return c

    lax.fori_loop(0, N_EXPERTS, issue_run, 0)

    def pad_expert(j, c):
        e = i + j * n
        d0 = ps_ref[e]
        cnt = pn_ref[e]

        def start(k, c2):
            pltpu.make_async_copy(zero_group, _group(xs_hbm, d0 + k), sems.at[2]).start()
            return c2

        def wait(k, c2):
            pltpu.make_async_copy(zero_group, _group(xs_hbm, 0), sems.at[2]).wait()
            return c2

        lax.fori_loop(0, cnt, start, 0)
        lax.fori_loop(0, cnt, wait, 0)
        return c

    lax.fori_loop(0, (N_EXPERTS - i + n - 1) // n, pad_expert, 0)

    def tail_tile(j, c):
        u = nu_ref[0] + i + j * n
        cp = pltpu.make_async_copy(
            zero_hbm, xs_hbm.at[pl.ds(pl.multiple_of(u * MOE_TILE, MOE_TILE), MOE_TILE)], sems.at[2])
        cp.start()
        cp.wait()
        return c

    n_tail = xs_hbm.shape[0] // MOE_TILE - nu_ref[0]
    lax.fori_loop(0, (n_tail - i + n - 1) // n, tail_tile, 0)

    @pl.when(i > 0)
    def _():
        wait_tile(i - 1)

    @pl.when(i == n - 1)
    def _():
        wait_tile(i)


def _moe_dispatch(xl, plan, n_tt):
    d = xl.shape[1]
    rows = plan["n_tiles"] * MOE_TILE
    grid_spec = pltpu.PrefetchScalarGridSpec(
        num_scalar_prefetch=6,
        grid=(n_tt,),
        in_specs=[pl.BlockSpec(memory_space=pl.ANY), pl.BlockSpec(memory_space=pl.ANY)],
        out_specs=pl.BlockSpec(memory_space=pl.ANY),
        scratch_shapes=[pltpu.SemaphoreType.DMA((3,))],
    )
    return pl.pallas_call(
        _dispatch_kernel,
        grid_spec=grid_spec,
        out_shape=jax.ShapeDtypeStruct((rows, d), F32),
        compiler_params=_cparams(("arbitrary",)),
        name="moe_dispatch",
    )(plan["run"], plan["dst"], plan["src"], plan["pad_start"], plan["pad_len"], plan["n_used"], xl,
      jnp.zeros((MOE_TILE, d), F32))


def _moe_kernel(te_ref, nu_ref, xs_ref, wgu_ref, wd_ref, y_ref, wgu_bf, wd_bf):
    j = pl.program_id(0)
    n_used = nu_ref[0]
    de = wd_ref.shape[1]

    @pl.when(j < n_used)
    def _():
        prev = te_ref[jnp.maximum(j - 1, 0)]

        @pl.when(jnp.logical_or(j == 0, te_ref[j] != prev))
        def _():
            wgu_bf[...] = wgu_ref[0].astype(BF16)
            wd_bf[...] = wd_ref[0].astype(BF16)

        gu = _dot(xs_ref[...].astype(BF16), wgu_bf[...])
        a = gu[:, :de]
        b = gu[:, de:]
        mid = (a * jax.nn.sigmoid(a) * b).astype(BF16)
        y_ref[...] = _dot(mid, wd_bf[...])

    @pl.when(j >= n_used)
    def _():
        y_ref[...] = jnp.zeros_like(y_ref)


def _moe_experts(xs, plan, w_gate_up, w_down):
    d = xs.shape[1]
    tm = MOE_TILE
    n_tiles = plan["n_tiles"]
    de = w_down.shape[1]
    used = lambda j, nu: jnp.minimum(j, nu[0] - 1)
    grid_spec = pltpu.PrefetchScalarGridSpec(
        num_scalar_prefetch=2,
        grid=(n_tiles,),
        in_specs=[
            pl.BlockSpec((tm, d), lambda j, te, nu: (used(j, nu), 0)),
            pl.BlockSpec((1, d, 2 * de), lambda j, te, nu: (te[j], 0, 0)),
            pl.BlockSpec((1, de, d), lambda j, te, nu: (te[j], 0, 0)),
        ],
        out_specs=pl.BlockSpec((tm, d), lambda j, te, nu: (j, 0)),
        scratch_shapes=[pltpu.VMEM((d, 2 * de), BF16), pltpu.VMEM((de, d), BF16)],
    )
    return pl.pallas_call(
        _moe_kernel,
        grid_spec=grid_spec,
        out_shape=jax.ShapeDtypeStruct((n_tiles * tm, d), F32),
        compiler_params=_cparams(("arbitrary",)),
        name="moe_experts",
    )(plan["tile_expert"], plan["n_used"], xs, w_gate_up, w_down)


def _combine_kernel(run_ref, dst_ref, h_ref, route_ref, y_hbm, o_ref, ybuf, sems):
    i = pl.program_id(0)
    n = pl.num_programs(0)
    tm = o_ref.shape[0]
    rows = ybuf.shape[1]
    slot = i % 2

    def copy(g_src, g_dst, s):
        return pltpu.make_async_copy(_group(y_hbm, g_src), _group(ybuf.at[s], g_dst), sems.at[s])

    def issue(tile, s):
        def per_expert(e, off):
            r = tile * N_EXPERTS + e
            d0 = dst_ref[r]
            cnt = run_ref[r]

            def body(k, c):
                copy(d0 + k, off + k, s).start()
                return c
            lax.fori_loop(0, cnt, body, 0)
            return off + cnt
        return lax.fori_loop(0, N_EXPERTS, per_expert, 0)

    @pl.when(i == 0)
    def _():
        issue(0, 0)

    @pl.when(i + 1 < n)
    def _():
        issue(i + 1, 1 - slot)

    def count(e, tot):
        return tot + run_ref[i * N_EXPERTS + e]
    used = lax.fori_loop(0, N_EXPERTS, count, 0)

    def wait(_, c):
        copy(0, 0, slot).wait()
        return c
    lax.fori_loop(0, used, wait, 0)

    def clear(g, c):
        ybuf[slot, pl.ds(pl.multiple_of(g * SUBLANES, SUBLANES), SUBLANES), :] = jnp.zeros(
            (SUBLANES, ybuf.shape[2]), F32)
        return c
    lax.fori_loop(used, rows // SUBLANES, clear, 0)

    route = route_ref[...]
    col = lax.broadcasted_iota(I32, (tm, rows), 1).astype(F32)
    gate = jnp.where(col == route[:, 4:5], route[:, 2:3], jnp.where(col == route[:, 5:6], route[:, 3:4], 0.0))
    g_hi, g_lo = _split_bf16(gate)
    y = ybuf[slot].astype(BF16)
    o_ref[...] = h_ref[...] + _dot(g_hi, y) + _dot(g_lo, y)


def _moe_combine(h, route, y, plan, tm):
    t, d = h.shape
    n = t // tm
    grid_spec = pltpu.PrefetchScalarGridSpec(
        num_scalar_prefetch=2,
        grid=(n,),
        in_specs=[pl.BlockSpec((tm, d), lambda i, r, s: (i, 0)), pl.BlockSpec((tm, LANES), lambda i, r, s: (i, 0)),
                  pl.BlockSpec(memory_space=pl.ANY)],
        out_specs=pl.BlockSpec((tm, d), lambda i, r, s: (i, 0)),
        scratch_shapes=[pltpu.VMEM((2, _local_rows(tm), d), F32), pltpu.SemaphoreType.DMA((2,))],
    )
    return pl.pallas_call(
        _combine_kernel,
        grid_spec=grid_spec,
        out_shape=jax.ShapeDtypeStruct((t, d), F32),
        compiler_params=_cparams(("arbitrary",)),
        name="moe_combine",
    )(plan["run"], plan["dst"], h, route, y)


def _moe_layer(h, xl, route, cnt, tm, w_gate_up, w_down):
    plan = _moe_plan(cnt, tm)
    xs = _moe_dispatch(xl, plan, cnt.shape[0])
    y = _moe_experts(xs, plan, w_gate_up, w_down)
    return _moe_combine(h, route, y, plan, tm)


def _rope_block(x, cos_tab, sin_tab):
    lane = lax.broadcasted_iota(I32, x.shape, 1)
    half = QK_ROPE_DIM // 2
    up = pltpu.roll(x, LANES - half, 1)
    down = pltpu.roll(x, half, 1)
    partner = jnp.where(lane < half, up, jnp.where(lane < QK_ROPE_DIM, down, 0.0))
    return x * cos_tab + partner * sin_tab


def _proj_kernel(h_ref, cos_ref, sin_ref, kvg_ref, wdkv_ref, kvlg_ref, wkr_ref, wuk_ref, wuv_ref, kng_ref,
                 qng_ref, wdq_ref, qlg_ref, wuq_ref, qg_ref, hsum_ref, hexp_ref,
                 q_ref, k_ref, v_ref):
    h = h_ref[...]
    cos_tab = cos_ref[...]
    sin_tab = sin_ref[...]
    hsum = hsum_ref[...]
    hexp = hexp_ref[...]

    def head_sums(sq):
        hi, lo = _split_bf16(sq)
        return _dot(hi, hsum) + _dot(lo, hsum)

    def head_spread(val):
        hi, lo = _split_bf16(val)
        return _dot(hi, hexp) + _dot(lo, hexp)

    hn = _rms(h, kvg_ref[...]).astype(BF16)
    c_kv = _rms(_dot(hn, wdkv_ref[...]), kvlg_ref[...]).astype(BF16)
    kn = _dot(c_kv, wuk_ref[...])
    vv = _dot(c_kv, wuv_ref[...])
    kr = _dot(hn, wkr_ref[...])
    kng = kng_ref[...]
    ss = head_sums(kn * kn) + jnp.sum(kr * kr, axis=-1, keepdims=True)
    inv = lax.rsqrt(ss * (1.0 / QK_DIM) + EPS)
    spread = head_spread(inv)
    kr_rot = _rope_block(kr * kng[:, LANES:], cos_tab, sin_tab)
    for hd in range(N_HEADS):
        cols = slice(hd * LANES, (hd + 1) * LANES)
        f = spread[:, cols]
        k_ref[hd, :, 0:LANES] = (kn[:, cols] * f * kng[:, :LANES]).astype(BF16)
        k_ref[hd, :, LANES:HEAD_PAD] = (kr_rot * f).astype(BF16)
        v_ref[hd] = vv[:, cols].astype(BF16)

    hq = _rms(h, qng_ref[...]).astype(BF16)
    c_q = _rms(_dot(hq, wdq_ref[...]), qlg_ref[...]).astype(BF16)
    qn = _dot(c_q, wuq_ref[:, 0:N_HEADS * LANES])
    qr = _dot(c_q, wuq_ref[:, N_HEADS * LANES:2 * N_HEADS * LANES])
    qg = qg_ref[...]
    ssq = head_sums(qn * qn) + head_sums(qr * qr)
    invq = lax.rsqrt(ssq * (1.0 / QK_DIM) + EPS) * SOFTMAX_SCALE
    spreadq = head_spread(invq)
    for hd in range(N_HEADS):
        cols = slice(hd * LANES, (hd + 1) * LANES)
        f = spreadq[:, cols]
        q_ref[hd, :, 0:LANES] = (qn[:, cols] * f * qg[:, :LANES]).astype(BF16)
        rot = _rope_block(qr[:, cols] * qg[:, LANES:], cos_tab, sin_tab)
        q_ref[hd, :, LANES:HEAD_PAD] = (rot * f).astype(BF16)


def _head_major(w, per_head, lo, hi):
    k = w.shape[0]
    w3 = w.reshape(k, N_HEADS, per_head)[:, :, lo:hi]
    w3 = jnp.pad(w3, ((0, 0), (0, 0), (0, LANES - (hi - lo))))
    return w3.reshape(k, N_HEADS * LANES)


def _mla_project(h, seq, cos_tab, sin_tab, kv_norm_g, w_dkv, kv_latent_g, w_kr, w_ukv, k_norm_g,
                 b_norm_g, w_dq, q_latent_g, w_uq, q_norm_g):
    t, d = h.shape
    tm = min(TOKEN_TILE, seq)
    kv_rank = w_dkv.shape[1]
    q_rank = w_dq.shape[1]
    hw = N_HEADS * LANES
    row = lambda a: a.reshape(1, -1)
    wkr = jnp.pad(w_kr, ((0, 0), (0, LANES - QK_ROPE_DIM))).astype(BF16)
    wuk = _head_major(w_ukv, QK_NOPE_DIM + V_HEAD_DIM, 0, QK_NOPE_DIM).astype(BF16)
    wuv = _head_major(w_ukv, QK_NOPE_DIM + V_HEAD_DIM, QK_NOPE_DIM, QK_NOPE_DIM + V_HEAD_DIM).astype(BF16)
    wuq = jnp.concatenate([_head_major(w_uq, QK_DIM, 0, QK_NOPE_DIM),
                           _head_major(w_uq, QK_DIM, QK_NOPE_DIM, QK_DIM)], axis=1).astype(BF16)
    pad_gain = lambda g: jnp.pad(g, (0, HEAD_PAD - QK_DIM)).reshape(1, HEAD_PAD)
    head_of = jnp.arange(hw, dtype=I32) // LANES
    hsum = (head_of[:, None] == jnp.arange(LANES, dtype=I32)[None, :]).astype(BF16)
    hexp = hsum.T
    tok = lambda w: pl.BlockSpec((tm, w), lambda i: (i, 0))
    heads = lambda w: pl.BlockSpec((N_HEADS, tm, w), lambda i: (0, i, 0))
    return pl.pallas_call(
        _proj_kernel,
        grid=(t // tm,),
        in_specs=[tok(d), tok(LANES), tok(LANES), _full((1, d)), _full((d, kv_rank)), _full((1, kv_rank)),
                  _full((d, LANES)), _full((kv_rank, hw)), _full((kv_rank, hw)), _full((1, HEAD_PAD)),
                  _full((1, d)), _full((d, q_rank)), _full((1, q_rank)), _full((q_rank, 2 * hw)),
                  _full((1, HEAD_PAD)), _full((hw, LANES)), _full((LANES, hw))],
        out_specs=[heads(HEAD_PAD), heads(HEAD_PAD), heads(LANES)],
        out_shape=[jax.ShapeDtypeStruct((N_HEADS, t, HEAD_PAD), BF16),
                   jax.ShapeDtypeStruct((N_HEADS, t, HEAD_PAD), BF16),
                   jax.ShapeDtypeStruct((N_HEADS, t, LANES), BF16)],
        compiler_params=_cparams(("arbitrary",)),
        name="mla_project",
    )(h, cos_tab, sin_tab, row(kv_norm_g), w_dkv.astype(BF16), row(kv_latent_g), wkr, wuk, wuv,
      pad_gain(k_norm_g), row(b_norm_g), w_dq.astype(BF16), row(q_latent_g), wuq, pad_gain(q_norm_g),
      hsum, hexp)


def _attn_kernel(qi_ref, ki_ref, q_ref, k_ref, v_ref, o_ref, m_ref, acc_ref, vext_ref):
    s_idx = pl.program_id(1)
    qi = qi_ref[s_idx]
    ki = ki_ref[s_idx]
    tq = q_ref.shape[1]
    tk = k_ref.shape[1]
    dv = V_HEAD_DIM

    @pl.when(ki == 0)
    def _():
        m_ref[...] = jnp.full_like(m_ref, -jnp.inf)
        acc_ref[...] = jnp.zeros_like(acc_ref)
        vext_ref[:, :, dv:] = jnp.ones((N_HEADS, tk, LANES), BF16)

    vext_ref[:, :, :dv] = v_ref[...]

    def step(masked):
        if masked:
            qc = lax.broadcasted_iota(I32, (tq, tk), 0) // CHUNK
            kc = lax.broadcasted_iota(I32, (tq, tk), 1) // CHUNK
            allowed = kc <= qc
        for hd in range(N_HEADS):
            s = lax.dot_general(q_ref[hd], k_ref[hd], (((1,), (1,)), ((), ())), preferred_element_type=F32)
            if masked:
                s = jnp.where(allowed, s, MASK_VALUE)
            m_prev = m_ref[hd]
            m_new = jnp.maximum(m_prev, jnp.max(s, axis=-1, keepdims=True))
            alpha = jnp.exp(m_prev - m_new)
            p = jnp.exp(s - jnp.concatenate([m_new] * (tk // LANES), axis=1))
            pv = _dot(p.astype(BF16), vext_ref[hd])
            acc_ref[hd] = jnp.concatenate([alpha, alpha], axis=1) * acc_ref[hd] + pv
            m_ref[hd] = m_new

    @pl.when(ki < qi)
    def _():
        step(False)

    @pl.when(ki == qi)
    def _():
        step(True)
        for hd in range(N_HEADS):
            acc = acc_ref[hd]
            o_ref[:, hd * dv:(hd + 1) * dv] = (acc[:, :dv] / acc[:, dv:]).astype(o_ref.dtype)


def _attention(q, k, v, batch, seq):
    t = q.shape[1]
    tq = min(ATTN_TILE, seq)
    nq = seq // tq
    pairs = [(a, b) for a in range(nq) for b in range(a + 1)]
    qi_tab = jnp.array([p[0] for p in pairs], I32)
    ki_tab = jnp.array([p[1] for p in pairs], I32)
    grid_spec = pltpu.PrefetchScalarGridSpec(
        num_scalar_prefetch=2,
        grid=(batch, len(pairs)),
        in_specs=[
            pl.BlockSpec((N_HEADS, tq, HEAD_PAD), lambda b, s, qi, ki: (0, b * nq + qi[s], 0)),
            pl.BlockSpec((N_HEADS, tq, HEAD_PAD), lambda b, s, qi, ki: (0, b * nq + ki[s], 0)),
            pl.BlockSpec((N_HEADS, tq, V_HEAD_DIM), lambda b, s, qi, ki: (0, b * nq + ki[s], 0)),
        ],
        out_specs=pl.BlockSpec((tq, N_HEADS * V_HEAD_DIM), lambda b, s, qi, ki: (b * nq + qi[s], 0)),
        scratch_shapes=[pltpu.VMEM((N_HEADS, tq, LANES), F32),
                        pltpu.VMEM((N_HEADS, tq, V_HEAD_DIM + LANES), F32),
                        pltpu.VMEM((N_HEADS, tq, V_HEAD_DIM + LANES), BF16)],
    )
    return pl.pallas_call(
        _attn_kernel,
        grid_spec=grid_spec,
        out_shape=jax.ShapeDtypeStruct((t, N_HEADS * V_HEAD_DIM), BF16),
        compiler_params=_cparams(("arbitrary", "arbitrary")),
        name="attention",
    )(qi_tab, ki_tab, q, k, v)


def _oproj_kernel(o_ref, h_ref, wo_ref, fng_ref, wrh_ref, wrl_ref, br_ref, tri_ref, upper_ref,
                  h_out_ref, xl_ref, route_ref, cnt_ref):
    h = h_ref[...] + _dot(o_ref[...], wo_ref[...])
    h_out_ref[...] = h
    xl, route, cnt = _route(h, fng_ref[...], wrh_ref[...], wrl_ref[...], br_ref[...], tri_ref[...], upper_ref[...])
    xl_ref[...] = xl
    route_ref[...] = route
    cnt_ref[0] = cnt


def _oproj_layer(o, h, seq, w_o, fng, wrh, wrl, br):
    t, d = h.shape
    tm = min(TOKEN_TILE, seq)
    r_in, r_out = _router_specs(tm, d)
    tok = lambda w: pl.BlockSpec((tm, w), lambda i: (i, 0))
    return pl.pallas_call(
        _oproj_kernel,
        grid=(t // tm,),
        in_specs=[tok(o.shape[1]), tok(d), _full(w_o.shape)] + r_in,
        out_specs=r_out,
        out_shape=_router_out_shapes(t, tm, d),
        compiler_params=_cparams(("arbitrary",)),
        name="oproj_router",
    )(o, h, w_o.astype(BF16), fng.reshape(1, d), wrh, wrl, br, _strict_lower(tm), _strict_lower(LANES).T)


def kernel(x, positions, a_norm_g, a_pw1_w, a_pw1_b, a_dw_w, a_dw_b, a_ln_g, a_ln_b, a_pw2_w, a_pw2_b, kv_norm_g, w_dkv, kv_latent_g, w_kr, w_ukv, k_norm_g, b_norm_g, w_dq, q_latent_g, w_uq, q_norm_g, w_o, ffn_norm_g, w_group, b_group, w_router, b_router, w_gate_up, w_down):
    batch, seq, d = x.shape
    t = batch * seq
    tm = min(TOKEN_TILE, seq)
    assert a_norm_g.shape[0] == 1 and b_norm_g.shape[0] == 1 and ffn_norm_g.shape[0] == 2
    cos_tab, sin_tab = _rope_tables(positions)

    wrh, wrl, br = _router_weights(w_group[0], b_group[0], w_router[0], b_router[0])
    h, xl, route, cnt = _conv_layer(x.reshape(t, d), seq, a_norm_g[0], a_pw1_w[0], a_pw1_b[0], a_dw_w[0],
                                    a_dw_b[0], a_ln_g[0], a_ln_b[0], a_pw2_w[0], a_pw2_b[0],
                                    ffn_norm_g[0], wrh, wrl, br)
    h = _moe_layer(h, xl, route, cnt, tm, w_gate_up[0], w_down[0])

    q, k, v = _mla_project(h, seq, cos_tab, sin_tab, kv_norm_g, w_dkv, kv_latent_g, w_kr, w_ukv, k_norm_g,
                           b_norm_g[0], w_dq[0], q_latent_g[0], w_uq[0], q_norm_g[0])
    o = _attention(q, k, v, batch, seq)
    wrh, wrl, br = _router_weights(w_group[1], b_group[1], w_router[1], b_router[1])
    h, xl, route, cnt = _oproj_layer(o, h, seq, w_o[0], ffn_norm_g[1], wrh, wrl, br)
    h = _moe_layer(h, xl, route, cnt, tm, w_gate_up[1], w_down[1])
    return h.reshape(batch, seq, d)
```

```python
import functools

import jax
import jax.numpy as jnp
from jax import lax
from jax.experimental import pallas as pl
from jax.experimental.pallas import tpu as pltpu

F32 = jnp.float32
BF16 = jnp.bfloat16
I32 = jnp.int32

EPS = 1e-6
CHUNK = 64
N_HEADS = 8
QK_NOPE_DIM = 128
QK_ROPE_DIM = 64
QK_DIM = QK_NOPE_DIM + QK_ROPE_DIM
V_HEAD_DIM = 128
ROPE_THETA = 10000.0
SOFTMAX_SCALE = QK_DIM ** -0.5
MASK_VALUE = -1e30
N_GROUPS = 4
EXPERTS_PER_GROUP = 8
N_EXPERTS = N_GROUPS * EXPERTS_PER_GROUP

LANES = 128
SUBLANES = 8
HEAD_PAD = 256
CONV_HALO = 32
CONV_ROWS = 64
TOKEN_TILE = 512
MOE_TILE = 256
ATTN_TILE = 512
VMEM_LIMIT = 56 * 1024 * 1024


def _cparams(sem):
    return pltpu.CompilerParams(dimension_semantics=sem, vmem_limit_bytes=VMEM_LIMIT)


def _rms(x, g):
    return x * lax.rsqrt(jnp.mean(x * x, axis=-1, keepdims=True) + EPS) * g


def _dot(a, b):
    return jnp.dot(a, b, preferred_element_type=F32)


def _split_bf16(x):
    hi = x.astype(BF16)
    lo = (x - hi.astype(F32)).astype(BF16)
    return hi, lo


def _full(shape):
    return pl.BlockSpec(shape, lambda *_: (0,) * len(shape))


def _local_rows(tm):
    return 2 * tm + N_EXPERTS * SUBLANES


def _rope_kernel(pos_ref, freq_ref, cos_ref, sin_ref):
    ang = pos_ref[...] * freq_ref[...]
    cos_ref[...] = jnp.cos(ang)
    sin_ref[...] = jnp.sin(ang)


def _rope_tables(positions):
    t = positions.size
    half = QK_ROPE_DIM // 2
    per_row = LANES // half
    inv_freq = ROPE_THETA ** (-jnp.arange(0, QK_ROPE_DIM, 2, dtype=F32) / QK_ROPE_DIM)
    pos = jnp.repeat(positions.reshape(t).astype(F32), half).reshape(t // per_row, LANES)
    freq = jnp.tile(inv_freq, per_row).reshape(1, LANES)
    rows = t // per_row
    blk = min(rows, 1024)
    cos, sin = pl.pallas_call(
        _rope_kernel,
        grid=(rows // blk,),
        in_specs=[pl.BlockSpec((blk, LANES), lambda i: (i, 0)), _full((1, LANES))],
        out_specs=[pl.BlockSpec((blk, LANES), lambda i: (i, 0))] * 2,
        out_shape=[jax.ShapeDtypeStruct((rows, LANES), F32)] * 2,
        compiler_params=_cparams(("arbitrary",)),
        name="rope_tables",
    )(pos, freq)
    cos = cos.reshape(t, half)
    sin = sin.reshape(t, half)
    zeros = jnp.zeros((t, LANES - QK_ROPE_DIM), F32)
    cos_tab = jnp.concatenate([cos, cos, zeros], axis=1)
    sin_tab = jnp.concatenate([-sin, sin, zeros], axis=1)
    return cos_tab, sin_tab


def _route(h, fng, wr_hi, wr_lo, br, tri, upper):
    tm = h.shape[0]
    xt = _rms(h, fng)
    x_hi, x_lo = _split_bf16(xt)
    logits = _dot(x_hi, wr_hi) + _dot(x_lo, wr_hi) + _dot(x_hi, wr_lo) + br
    lane = lax.broadcasted_iota(I32, (tm, LANES), 1).astype(F32)
    neg = -jnp.inf
    big = float(LANES)

    gl = jnp.where(lane >= N_EXPERTS, jnp.where(lane < N_EXPERTS + N_GROUPS, logits, neg), neg)
    gmax = jnp.max(gl, axis=1, keepdims=True)
    gidx = jnp.min(jnp.where(gl == gmax, lane, big), axis=1, keepdims=True) - N_EXPERTS
    g_w = 1.0 / jnp.sum(jnp.exp(gl - gmax), axis=1, keepdims=True)

    lo_lane = gidx * EXPERTS_PER_GROUP
    el = jnp.where(lane >= lo_lane, jnp.where(lane < lo_lane + EXPERTS_PER_GROUP, logits, neg), neg)
    m1 = jnp.max(el, axis=1, keepdims=True)
    i1 = jnp.min(jnp.where(el == m1, lane, big), axis=1, keepdims=True)
    el2 = jnp.where(lane == i1, neg, el)
    m2 = jnp.max(el2, axis=1, keepdims=True)
    i2 = jnp.min(jnp.where(el2 == m2, lane, big), axis=1, keepdims=True)
    p2 = jnp.exp(m2 - m1)
    den = 1.0 + p2
    w1 = g_w / den
    w2 = g_w * p2 / den

    sel1 = lane == i1
    sel2 = lane == i2
    onehot = jnp.where(sel1, 1.0, jnp.where(sel2, 1.0, 0.0))
    before = _dot(tri, onehot.astype(BF16))
    cnt = jnp.sum(onehot, axis=0, keepdims=True)
    groups = jnp.floor((cnt + (SUBLANES - 1)) * (1.0 / SUBLANES))
    start = SUBLANES * _dot(jnp.broadcast_to(groups, (SUBLANES, LANES)).astype(BF16), upper)[0:1]
    at = before + start
    l1 = jnp.sum(jnp.where(sel1, at, 0.0), axis=1, keepdims=True)
    l2 = jnp.sum(jnp.where(sel2, at, 0.0), axis=1, keepdims=True)

    route = jnp.where(lane == 0, i1, jnp.where(lane == 1, i2, jnp.where(lane == 2, w1, jnp.where(
        lane == 3, w2, jnp.where(lane == 4, l1, jnp.where(lane == 5, l2, 0.0))))))
    route_t = route.T
    rows = lax.broadcasted_iota(I32, (_local_rows(tm), tm), 0).astype(F32)
    perm = jnp.where(rows == route_t[4:5], 1.0, jnp.where(rows == route_t[5:6], 1.0, 0.0)).astype(BF16)
    return _dot(perm, x_hi), route, cnt


def _router_weights(w_group, b_group, w_router, b_router):
    d = w_group.shape[0]
    pad = LANES - N_EXPERTS - N_GROUPS
    w = jnp.concatenate([w_router, w_group, jnp.zeros((d, pad), F32)], axis=1)
    b = jnp.concatenate([b_router, b_group, jnp.zeros((pad,), F32)]).reshape(1, LANES)
    hi, lo = _split_bf16(w)
    return hi, lo, b


def _strict_lower(n):
    r = lax.broadcasted_iota(I32, (n, n), 0)
    c = lax.broadcasted_iota(I32, (n, n), 1)
    return (c < r).astype(BF16)


def _router_specs(tm, d):
    tok = lambda w: pl.BlockSpec((tm, w), lambda i: (i, 0))
    in_specs = [_full((1, d)), _full((d, LANES)), _full((d, LANES)), _full((1, LANES)), _full((tm, tm)),
                _full((LANES, LANES))]
    out_specs = [tok(d), pl.BlockSpec((_local_rows(tm), d), lambda i: (i, 0)), tok(LANES),
                 pl.BlockSpec((1, 1, LANES), lambda i: (i, 0, 0))]
    return in_specs, out_specs


def _router_out_shapes(t, tm, d):
    n = t // tm
    return [jax.ShapeDtypeStruct((t, d), F32), jax.ShapeDtypeStruct((n * _local_rows(tm), d), F32),
            jax.ShapeDtypeStruct((t, LANES), F32), jax.ShapeDtypeStruct((n, 1, LANES), F32)]


def _conv_kernel(x_ref, ng_ref, w1_ref, b1_ref, dw_ref, dwb_ref, lng_ref, lnb_ref, w2_ref, b2_ref,
                 fng_ref, wrh_ref, wrl_ref, br_ref, tri_ref, upper_ref,
                 h_ref, xl_ref, route_ref, cnt_ref,
                 ubuf, cbuf, *, tiles_per_seq, width):
    i = pl.program_id(0)
    tm, d = x_ref.shape
    n_strips = d // LANES

    @pl.when(i % tiles_per_seq == 0)
    def _():
        ubuf[:, 0:CONV_HALO, :] = jnp.zeros((n_strips, CONV_HALO, LANES), F32)

    @pl.when(i % tiles_per_seq != 0)
    def _():
        ubuf[:, 0:CONV_HALO, :] = ubuf[:, tm:tm + CONV_HALO, :]

    x = x_ref[...]
    hn = _rms(x, ng_ref[...])
    ag = _dot(hn.astype(BF16), w1_ref[...]) + b1_ref[...]
    u = ag[:, :d] * jax.nn.sigmoid(ag[:, d:])
    for c in range(n_strips):
        ubuf[c, CONV_HALO:CONV_HALO + tm, :] = u[:, c * LANES:(c + 1) * LANES]

    first = CONV_HALO - (width - 1)
    for c in range(n_strips):
        cols = slice(c * LANES, (c + 1) * LANES)

        def chunk(r, carry, c=c, cols=cols):
            base = pl.multiple_of(r * CONV_ROWS, CONV_ROWS)
            acc = jnp.zeros((CONV_ROWS, LANES), F32)
            for t in range(width):
                acc = acc + dw_ref[t:t + 1, cols] * ubuf[c, pl.ds(base + first + t, CONV_ROWS), :]
            cbuf[pl.ds(base, CONV_ROWS), cols] = acc
            return carry

        lax.fori_loop(0, tm // CONV_ROWS, chunk, 0)

    v = cbuf[...] + dwb_ref[...]
    mu = jnp.mean(v, axis=-1, keepdims=True)
    vc = v - mu
    var = jnp.mean(vc * vc, axis=-1, keepdims=True)
    y = vc * lax.rsqrt(var + EPS) * lng_ref[...] + lnb_ref[...]
    y = y * jax.nn.sigmoid(y)
    h = x + _dot(y.astype(BF16), w2_ref[...]) + b2_ref[...]
    h_ref[...] = h

    xl, route, cnt = _route(h, fng_ref[...], wrh_ref[...], wrl_ref[...], br_ref[...], tri_ref[...], upper_ref[...])
    xl_ref[...] = xl
    route_ref[...] = route
    cnt_ref[0] = cnt


def _conv_layer(x, seq, ng, w1, b1, dw, dwb, lng, lnb, w2, b2, fng, wrh, wrl, br):
    t, d = x.shape
    tm = min(TOKEN_TILE, seq)
    width = dw.shape[0]
    dw_p = jnp.concatenate([dw, jnp.zeros((CONV_HALO - width, d), F32)], axis=0)
    row = lambda a: a.reshape(1, -1)
    r_in, r_out = _router_specs(tm, d)
    kern = functools.partial(_conv_kernel, tiles_per_seq=seq // tm, width=width)
    return pl.pallas_call(
        kern,
        grid=(t // tm,),
        in_specs=[pl.BlockSpec((tm, d), lambda i: (i, 0)), _full((1, d)), _full((d, 2 * d)), _full((1, 2 * d)),
                  _full((CONV_HALO, d)), _full((1, d)), _full((1, d)), _full((1, d)), _full((d, d)),
                  _full((1, d))] + r_in,
        out_specs=r_out,
        out_shape=_router_out_shapes(t, tm, d),
        scratch_shapes=[pltpu.VMEM((d // LANES, CONV_HALO + tm, LANES), F32), pltpu.VMEM((tm, d), F32)],
        compiler_params=_cparams(("arbitrary",)),
        name="conv_router",
    )(x, row(ng), w1.astype(BF16), row(b1), dw_p, row(dwb), row(lng), row(lnb), w2.astype(BF16), row(b2),
      row(fng), wrh, wrl, br, _strict_lower(tm), _strict_lower(LANES).T)


def _moe_plan(cnt, tm):
    n_tt = cnt.shape[0]
    c = cnt[:, 0, :N_EXPERTS].astype(I32)
    run = (c + SUBLANES - 1) // SUBLANES
    local = jnp.cumsum(run, axis=1) - run
    per_tile = MOE_TILE // SUBLANES
    n_exp = jnp.sum(run, axis=0)
    n_pad = ((n_exp + per_tile - 1) // per_tile) * per_tile
    e_end = jnp.cumsum(n_pad)
    e_base = e_end - n_pad
    dst = e_base[None, :] + jnp.cumsum(run, axis=0) - run
    src = jnp.arange(n_tt, dtype=I32)[:, None] * (_local_rows(tm) // SUBLANES) + local
    max_groups = (2 * n_tt * tm) // SUBLANES + n_tt * N_EXPERTS + N_EXPERTS * (per_tile - 1)
    n_tiles = -(-max_groups // per_tile)
    tile_start = jnp.minimum(jnp.arange(n_tiles, dtype=I32) * per_tile, e_end[-1] - per_tile)
    tile_expert = jnp.sum((tile_start[:, None] >= e_end[None, :]).astype(I32), axis=1)
    g0 = jnp.arange(n_tiles, dtype=I32) * per_tile
    run_start = jnp.take(dst, tile_expert, axis=1)
    run_end = jnp.take(dst + run, tile_expert, axis=1)
    scan_lo = jnp.sum((run_end <= g0[None, :]).astype(I32), axis=0)
    scan_hi = jnp.sum((run_start < (g0 + per_tile)[None, :]).astype(I32), axis=0)
    return dict(run=run.reshape(-1), dst=dst.reshape(-1), src=src.reshape(-1),
                last=(e_base + n_exp).astype(I32), tile_expert=tile_expert, scan_lo=scan_lo, scan_hi=scan_hi,
                n_used=(e_end[-1] // per_tile).reshape(1).astype(I32), n_tiles=n_tiles)


def _group(ref, g):
    return ref.at[pl.ds(pl.multiple_of(g * SUBLANES, SUBLANES), SUBLANES)]


def _moe_kernel(te_ref, nu_ref, run_ref, dst_ref, src_ref, last_ref, lo_ref, hi_ref, xl_hbm, wgu_ref, wd_ref,
                y_ref, xbuf, sems, wgu_bf, wd_bf):
    j = pl.program_id(0)
    n_used = nu_ref[0]
    de = wd_ref.shape[2]
    per_tile = MOE_TILE // SUBLANES
    slot = j % 2

    def copy(g_src, g_dst, s):
        return pltpu.make_async_copy(_group(xl_hbm, g_src), _group(xbuf.at[s], g_dst), sems.at[s])

    def issue(tile, s):
        e = te_ref[tile]
        g0 = tile * per_tile

        def per_run(i, c):
            r = i * N_EXPERTS + e
            d0 = dst_ref[r]
            s0 = src_ref[r]

            def body(k, c2):
                copy(s0 + k - d0, k - g0, s).start()
                return c2
            lax.fori_loop(jnp.maximum(d0, g0), jnp.minimum(d0 + run_ref[r], g0 + per_tile), body, 0)
            return c
        lax.fori_loop(lo_ref[tile], hi_ref[tile], per_run, 0)

    @pl.when(j == 0)
    def _():
        xbuf[...] = jnp.zeros_like(xbuf)
        issue(0, 0)

    @pl.when(j + 1 < n_used)
    def _():
        issue(j + 1, 1 - slot)

    @pl.when(j < n_used)
    def _():
        e = te_ref[j]
        prev = te_ref[jnp.maximum(j - 1, 0)]

        @pl.when(jnp.logical_or(j == 0, e != prev))
        def _():
            wgu_bf[...] = wgu_ref[0, 0].astype(BF16)
            wd_bf[...] = wd_ref[0, 0].astype(BF16)

        def wait(_, c):
            copy(0, 0, slot).wait()
            return c
        lax.fori_loop(0, jnp.clip(last_ref[e] - j * per_tile, 0, per_tile), wait, 0)

        gu = _dot(xbuf[slot].astype(BF16), wgu_bf[...])
        a = gu[:, :de]
        b = gu[:, de:]
        mid = (a * jax.nn.sigmoid(a) * b).astype(BF16)
        y_ref[...] = _dot(mid, wd_bf[...])

    @pl.when(j >= n_used)
    def _():
        y_ref[...] = jnp.zeros_like(y_ref)


def _moe_experts(xl, plan, layer, w_gate_up, w_down):
    d = xl.shape[1]
    tm = MOE_TILE
    n_tiles = plan["n_tiles"]
    de = w_down.shape[2]
    grid_spec = pltpu.PrefetchScalarGridSpec(
        num_scalar_prefetch=8,
        grid=(n_tiles,),
        in_specs=[
            pl.BlockSpec(memory_space=pl.ANY),
            pl.BlockSpec((1, 1, d, 2 * de), lambda j, te, *_: (layer, te[j], 0, 0)),
            pl.BlockSpec((1, 1, de, d), lambda j, te, *_: (layer, te[j], 0, 0)),
        ],
        out_specs=pl.BlockSpec((tm, d), lambda j, *_: (j, 0)),
        scratch_shapes=[pltpu.VMEM((2, tm, d), F32), pltpu.SemaphoreType.DMA((2,)),
                        pltpu.VMEM((d, 2 * de), BF16), pltpu.VMEM((de, d), BF16)],
    )
    return pl.pallas_call(
        _moe_kernel,
        grid_spec=grid_spec,
        out_shape=jax.ShapeDtypeStruct((n_tiles * tm, d), F32),
        compiler_params=_cparams(("arbitrary",)),
        name="moe_experts",
    )(plan["tile_expert"], plan["n_used"], plan["run"], plan["dst"], plan["src"], plan["last"],
      plan["scan_lo"], plan["scan_hi"], xl, w_gate_up, w_down)


def _combine_kernel(run_ref, dst_ref, h_ref, route_ref, y_hbm, o_ref, ybuf, sems):
    i = pl.program_id(0)
    n = pl.num_programs(0)
    tm = o_ref.shape[0]
    rows = ybuf.shape[1]
    slot = i % 2

    def copy(g_src, g_dst, s):
        return pltpu.make_async_copy(_group(y_hbm, g_src), _group(ybuf.at[s], g_dst), sems.at[s])

    def issue(tile, s):
        def per_expert(e, off):
            r = tile * N_EXPERTS + e
            d0 = dst_ref[r]
            cnt = run_ref[r]

            def body(k, c):
                copy(d0 + k, off + k, s).start()
                return c
            lax.fori_loop(0, cnt, body, 0)
            return off + cnt
        return lax.fori_loop(0, N_EXPERTS, per_expert, 0)

    @pl.when(i == 0)
    def _():
        issue(0, 0)

    @pl.when(i + 1 < n)
    def _():
        issue(i + 1, 1 - slot)

    def count(e, tot):
        return tot + run_ref[i * N_EXPERTS + e]
    used = lax.fori_loop(0, N_EXPERTS, count, 0)

    def wait(_, c):
        copy(0, 0, slot).wait()
        return c
    lax.fori_loop(0, used, wait, 0)

    def clear(g, c):
        ybuf[slot, pl.ds(pl.multiple_of(g * SUBLANES, SUBLANES), SUBLANES), :] = jnp.zeros(
            (SUBLANES, ybuf.shape[2]), F32)
        return c
    lax.fori_loop(used, rows // SUBLANES, clear, 0)

    route = route_ref[...]
    col = lax.broadcasted_iota(I32, (tm, rows), 1).astype(F32)
    gate = jnp.where(col == route[:, 4:5], route[:, 2:3], jnp.where(col == route[:, 5:6], route[:, 3:4], 0.0))
    g_hi, g_lo = _split_bf16(gate)
    y = ybuf[slot].astype(BF16)
    o_ref[...] = h_ref[...] + _dot(g_hi, y) + _dot(g_lo, y)


def _moe_combine(h, route, y, plan, tm):
    t, d = h.shape
    n = t // tm
    grid_spec = pltpu.PrefetchScalarGridSpec(
        num_scalar_prefetch=2,
        grid=(n,),
        in_specs=[pl.BlockSpec((tm, d), lambda i, r, s: (i, 0)), pl.BlockSpec((tm, LANES), lambda i, r, s: (i, 0)),
                  pl.BlockSpec(memory_space=pl.ANY)],
        out_specs=pl.BlockSpec((tm, d), lambda i, r, s: (i, 0)),
        scratch_shapes=[pltpu.VMEM((2, _local_rows(tm), d), F32), pltpu.SemaphoreType.DMA((2,))],
    )
    return pl.pallas_call(
        _combine_kernel,
        grid_spec=grid_spec,
        out_shape=jax.ShapeDtypeStruct((t, d), F32),
        compiler_params=_cparams(("arbitrary",)),
        name="moe_combine",
    )(plan["run"], plan["dst"], h, route, y)


def _moe_layer(h, xl, route, cnt, tm, layer, w_gate_up, w_down):
    plan = _moe_plan(cnt, tm)
    y = _moe_experts(xl, plan, layer, w_gate_up, w_down)
    return _moe_combine(h, route, y, plan, tm)


def _rope_block(x, cos_tab, sin_tab):
    lane = lax.broadcasted_iota(I32, x.shape, 1)
    half = QK_ROPE_DIM // 2
    up = pltpu.roll(x, LANES - half, 1)
    down = pltpu.roll(x, half, 1)
    partner = jnp.where(lane < half, up, jnp.where(lane < QK_ROPE_DIM, down, 0.0))
    return x * cos_tab + partner * sin_tab


def _proj_kernel(h_ref, cos_ref, sin_ref, kvg_ref, wdkv_ref, kvlg_ref, wkr_ref, wuk_ref, wuv_ref, kng_ref,
                 qng_ref, wdq_ref, qlg_ref, wuq_ref, qg_ref, hsum_ref, hexp_ref,
                 q_ref, k_ref, v_ref):
    h = h_ref[...]
    cos_tab = cos_ref[...]
    sin_tab = sin_ref[...]
    hsum = hsum_ref[...]
    hexp = hexp_ref[...]

    def head_sums(sq):
        hi, lo = _split_bf16(sq)
        return _dot(hi, hsum) + _dot(lo, hsum)

    def head_spread(val):
        hi, lo = _split_bf16(val)
        return _dot(hi, hexp) + _dot(lo, hexp)

    hn = _rms(h, kvg_ref[...]).astype(BF16)
    c_kv = _rms(_dot(hn, wdkv_ref[...]), kvlg_ref[...]).astype(BF16)
    kn = _dot(c_kv, wuk_ref[...])
    vv = _dot(c_kv, wuv_ref[...])
    kr = _dot(hn, wkr_ref[...])
    kng = kng_ref[...]
    ss = head_sums(kn * kn) + jnp.sum(kr * kr, axis=-1, keepdims=True)
    inv = lax.rsqrt(ss * (1.0 / QK_DIM) + EPS)
    spread = head_spread(inv)
    kr_rot = _rope_block(kr * kng[:, LANES:], cos_tab, sin_tab)
    for hd in range(N_HEADS):
        cols = slice(hd * LANES, (hd + 1) * LANES)
        f = spread[:, cols]
        k_ref[hd, :, 0:LANES] = (kn[:, cols] * f * kng[:, :LANES]).astype(BF16)
        k_ref[hd, :, LANES:HEAD_PAD] = (kr_rot * f).astype(BF16)
        v_ref[hd] = vv[:, cols].astype(BF16)

    hq = _rms(h, qng_ref[...]).astype(BF16)
    c_q = _rms(_dot(hq, wdq_ref[...]), qlg_ref[...]).astype(BF16)
    qn = _dot(c_q, wuq_ref[:, 0:N_HEADS * LANES])
    qr = _dot(c_q, wuq_ref[:, N_HEADS * LANES:2 * N_HEADS * LANES])
    qg = qg_ref[...]
    ssq = head_sums(qn * qn) + head_sums(qr * qr)
    invq = lax.rsqrt(ssq * (1.0 / QK_DIM) + EPS) * SOFTMAX_SCALE
    spreadq = head_spread(invq)
    for hd in range(N_HEADS):
        cols = slice(hd * LANES, (hd + 1) * LANES)
        f = spreadq[:, cols]
        q_ref[hd, :, 0:LANES] = (qn[:, cols] * f * qg[:, :LANES]).astype(BF16)
        rot = _rope_block(qr[:, cols] * qg[:, LANES:], cos_tab, sin_tab)
        q_ref[hd, :, LANES:HEAD_PAD] = (rot * f).astype(BF16)


def _head_major(w, per_head, lo, hi):
    k = w.shape[0]
    w3 = w.reshape(k, N_HEADS, per_head)[:, :, lo:hi]
    w3 = jnp.pad(w3, ((0, 0), (0, 0), (0, LANES - (hi - lo))))
    return w3.reshape(k, N_HEADS * LANES)


def _mla_project(h, seq, cos_tab, sin_tab, kv_norm_g, w_dkv, kv_latent_g, w_kr, w_ukv, k_norm_g,
                 b_norm_g, w_dq, q_latent_g, w_uq, q_norm_g):
    t, d = h.shape
    tm = min(TOKEN_TILE, seq)
    kv_rank = w_dkv.shape[1]
    q_rank = w_dq.shape[1]
    hw = N_HEADS * LANES
    row = lambda a: a.reshape(1, -1)
    wkr = jnp.pad(w_kr, ((0, 0), (0, LANES - QK_ROPE_DIM))).astype(BF16)
    wuk = _head_major(w_ukv, QK_NOPE_DIM + V_HEAD_DIM, 0, QK_NOPE_DIM).astype(BF16)
    wuv = _head_major(w_ukv, QK_NOPE_DIM + V_HEAD_DIM, QK_NOPE_DIM, QK_NOPE_DIM + V_HEAD_DIM).astype(BF16)
    wuq = jnp.concatenate([_head_major(w_uq, QK_DIM, 0, QK_NOPE_DIM),
                           _head_major(w_uq, QK_DIM, QK_NOPE_DIM, QK_DIM)], axis=1).astype(BF16)
    pad_gain = lambda g: jnp.pad(g, (0, HEAD_PAD - QK_DIM)).reshape(1, HEAD_PAD)
    head_of = jnp.arange(hw, dtype=I32) // LANES
    hsum = (head_of[:, None] == jnp.arange(LANES, dtype=I32)[None, :]).astype(BF16)
    hexp = hsum.T
    tok = lambda w: pl.BlockSpec((tm, w), lambda i: (i, 0))
    heads = lambda w: pl.BlockSpec((N_HEADS, tm, w), lambda i: (0, i, 0))
    return pl.pallas_call(
        _proj_kernel,
        grid=(t // tm,),
        in_specs=[tok(d), tok(LANES), tok(LANES), _full((1, d)), _full((d, kv_rank)), _full((1, kv_rank)),
                  _full((d, LANES)), _full((kv_rank, hw)), _full((kv_rank, hw)), _full((1, HEAD_PAD)),
                  _full((1, d)), _full((d, q_rank)), _full((1, q_rank)), _full((q_rank, 2 * hw)),
                  _full((1, HEAD_PAD)), _full((hw, LANES)), _full((LANES, hw))],
        out_specs=[heads(HEAD_PAD), heads(HEAD_PAD), heads(LANES)],
        out_shape=[jax.ShapeDtypeStruct((N_HEADS, t, HEAD_PAD), BF16),
                   jax.ShapeDtypeStruct((N_HEADS, t, HEAD_PAD), BF16),
                   jax.ShapeDtypeStruct((N_HEADS, t, LANES), BF16)],
        compiler_params=_cparams(("arbitrary",)),
        name="mla_project",
    )(h, cos_tab, sin_tab, row(kv_norm_g), w_dkv.astype(BF16), row(kv_latent_g), wkr, wuk, wuv,
      pad_gain(k_norm_g), row(b_norm_g), w_dq.astype(BF16), row(q_latent_g), wuq, pad_gain(q_norm_g),
      hsum, hexp)


def _attn_kernel(qi_ref, ki_ref, q_ref, k_ref, v_ref, o_ref, m_ref, acc_ref, vext_ref):
    s_idx = pl.program_id(1)
    qi = qi_ref[s_idx]
    ki = ki_ref[s_idx]
    tq = q_ref.shape[1]
    tk = k_ref.shape[1]
    dv = V_HEAD_DIM

    @pl.when(ki == 0)
    def _():
        m_ref[...] = jnp.full_like(m_ref, -jnp.inf)
        acc_ref[...] = jnp.zeros_like(acc_ref)
        vext_ref[:, :, dv:] = jnp.ones((N_HEADS, tk, LANES), BF16)

    vext_ref[:, :, :dv] = v_ref[...]

    def step(masked):
        if masked:
            qc = lax.broadcasted_iota(I32, (tq, tk), 0) // CHUNK
            kc = lax.broadcasted_iota(I32, (tq, tk), 1) // CHUNK
            allowed = kc <= qc
        for hd in range(N_HEADS):
            s = lax.dot_general(q_ref[hd], k_ref[hd], (((1,), (1,)), ((), ())), preferred_element_type=F32)
            if masked:
                s = jnp.where(allowed, s, MASK_VALUE)
            m_prev = m_ref[hd]
            m_new = jnp.maximum(m_prev, jnp.max(s, axis=-1, keepdims=True))
            alpha = jnp.exp(m_prev - m_new)
            p = jnp.exp(s - jnp.concatenate([m_new] * (tk // LANES), axis=1))
            pv = _dot(p.astype(BF16), vext_ref[hd])
            acc_ref[hd] = jnp.concatenate([alpha, alpha], axis=1) * acc_ref[hd] + pv
            m_ref[hd] = m_new

    @pl.when(ki < qi)
    def _():
        step(False)

    @pl.when(ki == qi)
    def _():
        step(True)
        for hd in range(N_HEADS):
            acc = acc_ref[hd]
            o_ref[:, hd * dv:(hd + 1) * dv] = (acc[:, :dv] / acc[:, dv:]).astype(o_ref.dtype)


def _attention(q, k, v, batch, seq):
    t = q.shape[1]
    tq = min(ATTN_TILE, seq)
    nq = seq // tq
    pairs = [(a, b) for a in range(nq) for b in range(a + 1)]
    qi_tab = jnp.array([p[0] for p in pairs], I32)
    ki_tab = jnp.array([p[1] for p in pairs], I32)
    grid_spec = pltpu.PrefetchScalarGridSpec(
        num_scalar_prefetch=2,
        grid=(batch, len(pairs)),
        in_specs=[
            pl.BlockSpec((N_HEADS, tq, HEAD_PAD), lambda b, s, qi, ki: (0, b * nq + qi[s], 0)),
            pl.BlockSpec((N_HEADS, tq, HEAD_PAD), lambda b, s, qi, ki: (0, b * nq + ki[s], 0)),
            pl.BlockSpec((N_HEADS, tq, V_HEAD_DIM), lambda b, s, qi, ki: (0, b * nq + ki[s], 0)),
        ],
        out_specs=pl.BlockSpec((tq, N_HEADS * V_HEAD_DIM), lambda b, s, qi, ki: (b * nq + qi[s], 0)),
        scratch_shapes=[pltpu.VMEM((N_HEADS, tq, LANES), F32),
                        pltpu.VMEM((N_HEADS, tq, V_HEAD_DIM + LANES), F32),
                        pltpu.VMEM((N_HEADS, tq, V_HEAD_DIM + LANES), BF16)],
    )
    return pl.pallas_call(
        _attn_kernel,
        grid_spec=grid_spec,
        out_shape=jax.ShapeDtypeStruct((t, N_HEADS * V_HEAD_DIM), BF16),
        compiler_params=_cparams(("arbitrary", "arbitrary")),
        name="attention",
    )(qi_tab, ki_tab, q, k, v)


def _oproj_kernel(o_ref, h_ref, wo_ref, fng_ref, wrh_ref, wrl_ref, br_ref, tri_ref, upper_ref,
                  h_out_ref, xl_ref, route_ref, cnt_ref):
    h = h_ref[...] + _dot(o_ref[...], wo_ref[...])
    h_out_ref[...] = h
    xl, route, cnt = _route(h, fng_ref[...], wrh_ref[...], wrl_ref[...], br_ref[...], tri_ref[...], upper_ref[...])
    xl_ref[...] = xl
    route_ref[...] = route
    cnt_ref[0] = cnt


def _oproj_layer(o, h, seq, w_o, fng, wrh, wrl, br):
    t, d = h.shape
    tm = min(TOKEN_TILE, seq)
    r_in, r_out = _router_specs(tm, d)
    tok = lambda w: pl.BlockSpec((tm, w), lambda i: (i, 0))
    return pl.pallas_call(
        _oproj_kernel,
        grid=(t // tm,),
        in_specs=[tok(o.shape[1]), tok(d), _full(w_o.shape)] + r_in,
        out_specs=r_out,
        out_shape=_router_out_shapes(t, tm, d),
        compiler_params=_cparams(("arbitrary",)),
        name="oproj_router",
    )(o, h, w_o.astype(BF16), fng.reshape(1, d), wrh, wrl, br, _strict_lower(tm), _strict_lower(LANES).T)


def kernel(x, positions, a_norm_g, a_pw1_w, a_pw1_b, a_dw_w, a_dw_b, a_ln_g, a_ln_b, a_pw2_w, a_pw2_b, kv_norm_g, w_dkv, kv_latent_g, w_kr, w_ukv, k_norm_g, b_norm_g, w_dq, q_latent_g, w_uq, q_norm_g, w_o, ffn_norm_g, w_group, b_group, w_router, b_router, w_gate_up, w_down):
    batch, seq, d = x.shape
    t = batch * seq
    tm = min(TOKEN_TILE, seq)
    assert a_norm_g.shape[0] == 1 and b_norm_g.shape[0] == 1 and ffn_norm_g.shape[0] == 2
    cos_tab, sin_tab = _rope_tables(positions)

    wrh, wrl, br = _router_weights(w_group[0], b_group[0], w_router[0], b_router[0])
    h, xl, route, cnt = _conv_layer(x.reshape(t, d), seq, a_norm_g[0], a_pw1_w[0], a_pw1_b[0], a_dw_w[0],
                                    a_dw_b[0], a_ln_g[0], a_ln_b[0], a_pw2_w[0], a_pw2_b[0],
                                    ffn_norm_g[0], wrh, wrl, br)
    h = _moe_layer(h, xl, route, cnt, tm, 0, w_gate_up, w_down)

    q, k, v = _mla_project(h, seq, cos_tab, sin_tab, kv_norm_g, w_dkv, kv_latent_g, w_kr, w_ukv, k_norm_g,
                           b_norm_g[0], w_dq[0], q_latent_g[0], w_uq[0], q_norm_g[0])
    o = _attention(q, k, v, batch, seq)
    wrh, wrl, br = _router_weights(w_group[1], b_group[1], w_router[1], b_router[1])
    h, xl, route, cnt = _oproj_layer(o, h, seq, w_o[0], ffn_norm_g[1], wrh, wrl, br)
    h = _moe_layer(h, xl, route, cnt, tm, 1, w_gate_up, w_down)
    return h.reshape(batch, seq, d)
```

```python
import functools

import jax
import jax.numpy as jnp
from jax import lax
from jax.experimental import pallas as pl
from jax.experimental.pallas import tpu as pltpu

F32 = jnp.float32
BF16 = jnp.bfloat16
I32 = jnp.int32

EPS = 1e-6
CHUNK = 64
N_HEADS = 8
QK_NOPE_DIM = 128
QK_ROPE_DIM = 64
QK_DIM = QK_NOPE_DIM + QK_ROPE_DIM
V_HEAD_DIM = 128
ROPE_THETA = 10000.0
SOFTMAX_SCALE = QK_DIM ** -0.5
MASK_VALUE = -1e30
N_GROUPS = 4
EXPERTS_PER_GROUP = 8
N_EXPERTS = N_GROUPS * EXPERTS_PER_GROUP

LANES = 128
SUBLANES = 8
HEAD_PAD = 256
CONV_HALO = 32
CONV_ROWS = 64
TOKEN_TILE = 512
MOE_TILE = 512
ATTN_QUERY_TILE = 1024
ATTN_KEY_TILE = 512
VMEM_LIMIT = 56 * 1024 * 1024


def _cparams(sem):
    return pltpu.CompilerParams(dimension_semantics=sem, vmem_limit_bytes=VMEM_LIMIT)


def _rms(x, g):
    return x * lax.rsqrt(jnp.mean(x * x, axis=-1, keepdims=True) + EPS) * g


def _dot(a, b):
    return jnp.dot(a, b, preferred_element_type=F32)


def _split_bf16(x):
    hi = x.astype(BF16)
    lo = (x - hi.astype(F32)).astype(BF16)
    return hi, lo


def _full(shape):
    return pl.BlockSpec(shape, lambda *_: (0,) * len(shape))


def _local_rows(tm):
    return 2 * tm + N_EXPERTS * SUBLANES


def _rope_kernel(pos_ref, freq_ref, cos_ref, sin_ref):
    ang = pos_ref[...] * freq_ref[...]
    cos_ref[...] = jnp.cos(ang)
    sin_ref[...] = jnp.sin(ang)


def _rope_tables(positions):
    t = positions.size
    half = QK_ROPE_DIM // 2
    per_row = LANES // half
    inv_freq = ROPE_THETA ** (-jnp.arange(0, QK_ROPE_DIM, 2, dtype=F32) / QK_ROPE_DIM)
    pos = jnp.repeat(positions.reshape(t).astype(F32), half).reshape(t // per_row, LANES)
    freq = jnp.tile(inv_freq, per_row).reshape(1, LANES)
    rows = t // per_row
    blk = min(rows, 1024)
    cos, sin = pl.pallas_call(
        _rope_kernel,
        grid=(rows // blk,),
        in_specs=[pl.BlockSpec((blk, LANES), lambda i: (i, 0)), _full((1, LANES))],
        out_specs=[pl.BlockSpec((blk, LANES), lambda i: (i, 0))] * 2,
        out_shape=[jax.ShapeDtypeStruct((rows, LANES), F32)] * 2,
        compiler_params=_cparams(("arbitrary",)),
        name="rope_tables",
    )(pos, freq)
    cos = cos.reshape(t, half)
    sin = sin.reshape(t, half)
    zeros = jnp.zeros((t, LANES - QK_ROPE_DIM), F32)
    cos_tab = jnp.concatenate([cos, cos, zeros], axis=1)
    sin_tab = jnp.concatenate([-sin, sin, zeros], axis=1)
    return cos_tab, sin_tab


def _route(h, fng, wr_hi, wr_lo, br, tri, upper):
    tm = h.shape[0]
    xt = _rms(h, fng)
    x_hi, x_lo = _split_bf16(xt)
    logits = _dot(x_hi, wr_hi) + _dot(x_lo, wr_hi) + _dot(x_hi, wr_lo) + br
    lane = lax.broadcasted_iota(I32, (tm, LANES), 1).astype(F32)
    neg = -jnp.inf
    big = float(LANES)

    gl = jnp.where(lane >= N_EXPERTS, jnp.where(lane < N_EXPERTS + N_GROUPS, logits, neg), neg)
    gmax = jnp.max(gl, axis=1, keepdims=True)
    gidx = jnp.min(jnp.where(gl == gmax, lane, big), axis=1, keepdims=True) - N_EXPERTS
    g_w = 1.0 / jnp.sum(jnp.exp(gl - gmax), axis=1, keepdims=True)

    lo_lane = gidx * EXPERTS_PER_GROUP
    el = jnp.where(lane >= lo_lane, jnp.where(lane < lo_lane + EXPERTS_PER_GROUP, logits, neg), neg)
    m1 = jnp.max(el, axis=1, keepdims=True)
    i1 = jnp.min(jnp.where(el == m1, lane, big), axis=1, keepdims=True)
    el2 = jnp.where(lane == i1, neg, el)
    m2 = jnp.max(el2, axis=1, keepdims=True)
    i2 = jnp.min(jnp.where(el2 == m2, lane, big), axis=1, keepdims=True)
    p2 = jnp.exp(m2 - m1)
    den = 1.0 + p2
    w1 = g_w / den
    w2 = g_w * p2 / den

    sel1 = lane == i1
    sel2 = lane == i2
    onehot = jnp.where(sel1, 1.0, jnp.where(sel2, 1.0, 0.0))
    before = _dot(tri, onehot.astype(BF16))
    cnt = jnp.sum(onehot, axis=0, keepdims=True)
    groups = jnp.floor((cnt + (SUBLANES - 1)) * (1.0 / SUBLANES))
    start = SUBLANES * _dot(jnp.broadcast_to(groups, (SUBLANES, LANES)).astype(BF16), upper)[0:1]
    at = before + start
    l1 = jnp.sum(jnp.where(sel1, at, 0.0), axis=1, keepdims=True)
    l2 = jnp.sum(jnp.where(sel2, at, 0.0), axis=1, keepdims=True)

    route = jnp.where(lane == 0, i1, jnp.where(lane == 1, i2, jnp.where(lane == 2, w1, jnp.where(
        lane == 3, w2, jnp.where(lane == 4, l1, jnp.where(lane == 5, l2, 0.0))))))
    route_t = route.T
    rows = lax.broadcasted_iota(I32, (_local_rows(tm), tm), 0).astype(F32)
    perm = jnp.where(rows == route_t[4:5], 1.0, jnp.where(rows == route_t[5:6], 1.0, 0.0)).astype(BF16)
    return _dot(perm, x_hi), route, cnt


def _router_weights(w_group, b_group, w_router, b_router):
    d = w_group.shape[0]
    pad = LANES - N_EXPERTS - N_GROUPS
    w = jnp.concatenate([w_router, w_group, jnp.zeros((d, pad), F32)], axis=1)
    b = jnp.concatenate([b_router, b_group, jnp.zeros((pad,), F32)]).reshape(1, LANES)
    hi, lo = _split_bf16(w)
    return hi, lo, b


def _strict_lower(n):
    r = lax.broadcasted_iota(I32, (n, n), 0)
    c = lax.broadcasted_iota(I32, (n, n), 1)
    return (c < r).astype(BF16)


def _router_specs(tm, d):
    tok = lambda w: pl.BlockSpec((tm, w), lambda i: (i, 0))
    in_specs = [_full((1, d)), _full((d, LANES)), _full((d, LANES)), _full((1, LANES)), _full((tm, tm)),
                _full((LANES, LANES))]
    out_specs = [tok(d), pl.BlockSpec((_local_rows(tm), d), lambda i: (i, 0)), tok(LANES),
                 pl.BlockSpec((1, 1, LANES), lambda i: (i, 0, 0))]
    return in_specs, out_specs


def _router_out_shapes(t, tm, d):
    n = t // tm
    return [jax.ShapeDtypeStruct((t, d), F32), jax.ShapeDtypeStruct((n * _local_rows(tm), d), F32),
            jax.ShapeDtypeStruct((t, LANES), F32), jax.ShapeDtypeStruct((n, 1, LANES), F32)]


def _conv_kernel(x_ref, ng_ref, w1_ref, b1_ref, dw_ref, dwb_ref, lng_ref, lnb_ref, w2_ref, b2_ref,
                 fng_ref, wrh_ref, wrl_ref, br_ref, tri_ref, upper_ref,
                 h_ref, xl_ref, route_ref, cnt_ref,
                 ubuf, cbuf, *, tiles_per_seq, width):
    i = pl.program_id(0)
    tm, d = x_ref.shape
    n_strips = d // LANES

    @pl.when(i % tiles_per_seq == 0)
    def _():
        ubuf[:, 0:CONV_HALO, :] = jnp.zeros((n_strips, CONV_HALO, LANES), F32)

    @pl.when(i % tiles_per_seq != 0)
    def _():
        ubuf[:, 0:CONV_HALO, :] = ubuf[:, tm:tm + CONV_HALO, :]

    x = x_ref[...]
    hn = _rms(x, ng_ref[...])
    ag = _dot(hn.astype(BF16), w1_ref[...]) + b1_ref[...]
    u = ag[:, :d] * jax.nn.sigmoid(ag[:, d:])
    for c in range(n_strips):
        ubuf[c, CONV_HALO:CONV_HALO + tm, :] = u[:, c * LANES:(c + 1) * LANES]

    first = CONV_HALO - (width - 1)
    for c in range(n_strips):
        cols = slice(c * LANES, (c + 1) * LANES)

        def chunk(r, carry, c=c, cols=cols):
            base = pl.multiple_of(r * CONV_ROWS, CONV_ROWS)
            acc = jnp.zeros((CONV_ROWS, LANES), F32)
            for t in range(width):
                acc = acc + dw_ref[t:t + 1, cols] * ubuf[c, pl.ds(base + first + t, CONV_ROWS), :]
            cbuf[pl.ds(base, CONV_ROWS), cols] = acc
            return carry

        lax.fori_loop(0, tm // CONV_ROWS, chunk, 0)

    v = cbuf[...] + dwb_ref[...]
    mu = jnp.mean(v, axis=-1, keepdims=True)
    vc = v - mu
    var = jnp.mean(vc * vc, axis=-1, keepdims=True)
    y = vc * lax.rsqrt(var + EPS) * lng_ref[...] + lnb_ref[...]
    y = y * jax.nn.sigmoid(y)
    h = x + _dot(y.astype(BF16), w2_ref[...]) + b2_ref[...]
    h_ref[...] = h

    xl, route, cnt = _route(h, fng_ref[...], wrh_ref[...], wrl_ref[...], br_ref[...], tri_ref[...], upper_ref[...])
    xl_ref[...] = xl
    route_ref[...] = route
    cnt_ref[0] = cnt


def _conv_layer(x, seq, ng, w1, b1, dw, dwb, lng, lnb, w2, b2, fng, wrh, wrl, br):
    t, d = x.shape
    tm = min(TOKEN_TILE, seq)
    width = dw.shape[0]
    dw_p = jnp.concatenate([dw, jnp.zeros((CONV_HALO - width, d), F32)], axis=0)
    row = lambda a: a.reshape(1, -1)
    r_in, r_out = _router_specs(tm, d)
    kern = functools.partial(_conv_kernel, tiles_per_seq=seq // tm, width=width)
    return pl.pallas_call(
        kern,
        grid=(t // tm,),
        in_specs=[pl.BlockSpec((tm, d), lambda i: (i, 0)), _full((1, d)), _full((d, 2 * d)), _full((1, 2 * d)),
                  _full((CONV_HALO, d)), _full((1, d)), _full((1, d)), _full((1, d)), _full((d, d)),
                  _full((1, d))] + r_in,
        out_specs=r_out,
        out_shape=_router_out_shapes(t, tm, d),
        scratch_shapes=[pltpu.VMEM((d // LANES, CONV_HALO + tm, LANES), F32), pltpu.VMEM((tm, d), F32)],
        compiler_params=_cparams(("arbitrary",)),
        name="conv_router",
    )(x, row(ng), w1.astype(BF16), row(b1), dw_p, row(dwb), row(lng), row(lnb), w2.astype(BF16), row(b2),
      row(fng), wrh, wrl, br, _strict_lower(tm), _strict_lower(LANES).T)


def _moe_plan(cnt, tm):
    n_tt = cnt.shape[0]
    c = cnt[:, 0, :N_EXPERTS].astype(I32)
    run = (c + SUBLANES - 1) // SUBLANES
    local = jnp.cumsum(run, axis=1) - run
    per_tile = MOE_TILE // SUBLANES
    n_exp = jnp.sum(run, axis=0)
    n_pad = ((n_exp + per_tile - 1) // per_tile) * per_tile
    e_end = jnp.cumsum(n_pad)
    e_base = e_end - n_pad
    dst = e_base[None, :] + jnp.cumsum(run, axis=0) - run
    src = jnp.arange(n_tt, dtype=I32)[:, None] * (_local_rows(tm) // SUBLANES) + local
    max_groups = (2 * n_tt * tm) // SUBLANES + n_tt * N_EXPERTS + N_EXPERTS * (per_tile - 1)
    n_tiles = -(-max_groups // per_tile)
    tile_start = jnp.minimum(jnp.arange(n_tiles, dtype=I32) * per_tile, e_end[-1] - per_tile)
    tile_expert = jnp.sum((tile_start[:, None] >= e_end[None, :]).astype(I32), axis=1)
    g0 = jnp.arange(n_tiles, dtype=I32) * per_tile
    run_start = jnp.take(dst, tile_expert, axis=1)
    run_end = jnp.take(dst + run, tile_expert, axis=1)
    scan_lo = jnp.sum((run_end <= g0[None, :]).astype(I32), axis=0)
    scan_hi = jnp.sum((run_start < (g0 + per_tile)[None, :]).astype(I32), axis=0)
    return dict(run=run.reshape(-1), dst=dst.reshape(-1), src=src.reshape(-1),
                last=(e_base + n_exp).astype(I32), tile_expert=tile_expert, scan_lo=scan_lo, scan_hi=scan_hi,
                n_used=(e_end[-1] // per_tile).reshape(1).astype(I32), n_tiles=n_tiles)


def _group(ref, g):
    return ref.at[pl.ds(pl.multiple_of(g * SUBLANES, SUBLANES), SUBLANES)]


def _moe_kernel(te_ref, nu_ref, run_ref, dst_ref, src_ref, last_ref, lo_ref, hi_ref, xl_hbm, wgu_ref, wd_ref,
                y_ref, xbuf, sems, wgu_bf, wd_bf):
    j = pl.program_id(0)
    n_used = nu_ref[0]
    de = wd_ref.shape[2]
    per_tile = MOE_TILE // SUBLANES
    slot = j % 2

    def copy(g_src, g_dst, s):
        return pltpu.make_async_copy(_group(xl_hbm, g_src), _group(xbuf.at[s], g_dst), sems.at[s])

    def issue(tile, s):
        e = te_ref[tile]
        g0 = tile * per_tile

        def per_run(i, c):
            r = i * N_EXPERTS + e
            d0 = dst_ref[r]
            s0 = src_ref[r]

            def body(k, c2):
                copy(s0 + k - d0, k - g0, s).start()
                return c2
            lax.fori_loop(jnp.maximum(d0, g0), jnp.minimum(d0 + run_ref[r], g0 + per_tile), body, 0)
            return c
        lax.fori_loop(lo_ref[tile], hi_ref[tile], per_run, 0)

    @pl.when(j == 0)
    def _():
        xbuf[...] = jnp.zeros_like(xbuf)
        issue(0, 0)

    @pl.when(j + 1 < n_used)
    def _():
        issue(j + 1, 1 - slot)

    @pl.when(j < n_used)
    def _():
        e = te_ref[j]
        prev = te_ref[jnp.maximum(j - 1, 0)]

        @pl.when(jnp.logical_or(j == 0, e != prev))
        def _():
            wgu_bf[...] = wgu_ref[0, 0].astype(BF16)
            wd_bf[...] = wd_ref[0, 0].astype(BF16)

        def wait(_, c):
            copy(0, 0, slot).wait()
            return c
        lax.fori_loop(0, jnp.clip(last_ref[e] - j * per_tile, 0, per_tile), wait, 0)

        gu = _dot(xbuf[slot].astype(BF16), wgu_bf[...])
        a = gu[:, :de]
        b = gu[:, de:]
        mid = (a * jax.nn.sigmoid(a) * b).astype(BF16)
        y_ref[...] = _dot(mid, wd_bf[...])

    @pl.when(j >= n_used)
    def _():
        y_ref[...] = jnp.zeros_like(y_ref)


def _moe_experts(xl, plan, layer, w_gate_up, w_down):
    d = xl.shape[1]
    tm = MOE_TILE
    n_tiles = plan["n_tiles"]
    de = w_down.shape[2]
    grid_spec = pltpu.PrefetchScalarGridSpec(
        num_scalar_prefetch=8,
        grid=(n_tiles,),
        in_specs=[
            pl.BlockSpec(memory_space=pl.ANY),
            pl.BlockSpec((1, 1, d, 2 * de), lambda j, te, *_: (layer, te[j], 0, 0)),
            pl.BlockSpec((1, 1, de, d), lambda j, te, *_: (layer, te[j], 0, 0)),
        ],
        out_specs=pl.BlockSpec((tm, d), lambda j, *_: (j, 0)),
        scratch_shapes=[pltpu.VMEM((2, tm, d), F32), pltpu.SemaphoreType.DMA((2,)),
                        pltpu.VMEM((d, 2 * de), BF16), pltpu.VMEM((de, d), BF16)],
    )
    return pl.pallas_call(
        _moe_kernel,
        grid_spec=grid_spec,
        out_shape=jax.ShapeDtypeStruct((n_tiles * tm, d), F32),
        compiler_params=_cparams(("arbitrary",)),
        name="moe_experts",
    )(plan["tile_expert"], plan["n_used"], plan["run"], plan["dst"], plan["src"], plan["last"],
      plan["scan_lo"], plan["scan_hi"], xl, w_gate_up, w_down)


def _combine_kernel(run_ref, dst_ref, h_ref, route_ref, y_hbm, o_ref, ybuf, sems):
    i = pl.program_id(0)
    n = pl.num_programs(0)
    tm = o_ref.shape[0]
    rows = ybuf.shape[1]
    slot = i % 2

    def copy(g_src, g_dst, s):
        return pltpu.make_async_copy(_group(y_hbm, g_src), _group(ybuf.at[s], g_dst), sems.at[s])

    def issue(tile, s):
        def per_expert(e, off):
            r = tile * N_EXPERTS + e
            d0 = dst_ref[r]
            cnt = run_ref[r]

            def body(k, c):
                copy(d0 + k, off + k, s).start()
                return c
            lax.fori_loop(0, cnt, body, 0)
            return off + cnt
        return lax.fori_loop(0, N_EXPERTS, per_expert, 0)

    @pl.when(i == 0)
    def _():
        issue(0, 0)

    @pl.when(i + 1 < n)
    def _():
        issue(i + 1, 1 - slot)

    def count(e, tot):
        return tot + run_ref[i * N_EXPERTS + e]
    used = lax.fori_loop(0, N_EXPERTS, count, 0)

    def wait(_, c):
        copy(0, 0, slot).wait()
        return c
    lax.fori_loop(0, used, wait, 0)

    def clear(g, c):
        ybuf[slot, pl.ds(pl.multiple_of(g * SUBLANES, SUBLANES), SUBLANES), :] = jnp.zeros(
            (SUBLANES, ybuf.shape[2]), F32)
        return c
    lax.fori_loop(used, rows // SUBLANES, clear, 0)

    route = route_ref[...]
    col = lax.broadcasted_iota(I32, (tm, rows), 1).astype(F32)
    gate = jnp.where(col == route[:, 4:5], route[:, 2:3], jnp.where(col == route[:, 5:6], route[:, 3:4], 0.0))
    g_hi, g_lo = _split_bf16(gate)
    y = ybuf[slot].astype(BF16)
    o_ref[...] = h_ref[...] + _dot(g_hi, y) + _dot(g_lo, y)


def _moe_combine(h, route, y, plan, tm):
    t, d = h.shape
    n = t // tm
    grid_spec = pltpu.PrefetchScalarGridSpec(
        num_scalar_prefetch=2,
        grid=(n,),
        in_specs=[pl.BlockSpec((tm, d), lambda i, r, s: (i, 0)), pl.BlockSpec((tm, LANES), lambda i, r, s: (i, 0)),
                  pl.BlockSpec(memory_space=pl.ANY)],
        out_specs=pl.BlockSpec((tm, d), lambda i, r, s: (i, 0)),
        scratch_shapes=[pltpu.VMEM((2, _local_rows(tm), d), F32), pltpu.SemaphoreType.DMA((2,))],
    )
    return pl.pallas_call(
        _combine_kernel,
        grid_spec=grid_spec,
        out_shape=jax.ShapeDtypeStruct((t, d), F32),
        compiler_params=_cparams(("arbitrary",)),
        name="moe_combine",
    )(plan["run"], plan["dst"], h, route, y)


def _moe_layer(h, xl, route, cnt, tm, layer, w_gate_up, w_down):
    plan = _moe_plan(cnt, tm)
    y = _moe_experts(xl, plan, layer, w_gate_up, w_down)
    return _moe_combine(h, route, y, plan, tm)


def _rope_block(x, cos_tab, sin_tab):
    lane = lax.broadcasted_iota(I32, x.shape, 1)
    half = QK_ROPE_DIM // 2
    up = pltpu.roll(x, LANES - half, 1)
    down = pltpu.roll(x, half, 1)
    partner = jnp.where(lane < half, up, jnp.where(lane < QK_ROPE_DIM, down, 0.0))
    return x * cos_tab + partner * sin_tab


def _proj_kernel(h_ref, cos_ref, sin_ref, kvg_ref, wdkv_ref, kvlg_ref, wkr_ref, wuk_ref, wuv_ref, kng_ref,
                 qng_ref, wdq_ref, qlg_ref, wuq_ref, qg_ref, hsum_ref, hexp_ref,
                 q_ref, k_ref, v_ref):
    h = h_ref[...]
    cos_tab = cos_ref[...]
    sin_tab = sin_ref[...]
    hsum = hsum_ref[...]
    hexp = hexp_ref[...]

    def head_sums(sq):
        return _dot(sq.astype(BF16), hsum)

    def head_spread(val):
        hi, lo = _split_bf16(val)
        return _dot(hi, hexp) + _dot(lo, hexp)

    hn = _rms(h, kvg_ref[...]).astype(BF16)
    c_kv = _rms(_dot(hn, wdkv_ref[...]), kvlg_ref[...]).astype(BF16)
    kn = _dot(c_kv, wuk_ref[...])
    vv = _dot(c_kv, wuv_ref[...])
    kr = _dot(hn, wkr_ref[...])
    kng = kng_ref[...]
    ss = head_sums(kn * kn) + jnp.sum(kr * kr, axis=-1, keepdims=True)
    inv = lax.rsqrt(ss * (1.0 / QK_DIM) + EPS)
    spread = head_spread(inv)
    kr_rot = _rope_block(kr * kng[:, LANES:], cos_tab, sin_tab)
    for hd in range(N_HEADS):
        cols = slice(hd * LANES, (hd + 1) * LANES)
        f = spread[:, cols]
        k_ref[hd, :, 0:LANES] = (kn[:, cols] * f * kng[:, :LANES]).astype(BF16)
        k_ref[hd, :, LANES:HEAD_PAD] = (kr_rot * f).astype(BF16)
        v_ref[hd] = vv[:, cols].astype(BF16)

    hq = _rms(h, qng_ref[...]).astype(BF16)
    c_q = _rms(_dot(hq, wdq_ref[...]), qlg_ref[...]).astype(BF16)
    qn = _dot(c_q, wuq_ref[:, 0:N_HEADS * LANES])
    qr = _dot(c_q, wuq_ref[:, N_HEADS * LANES:2 * N_HEADS * LANES])
    qg = qg_ref[...]
    ssq = head_sums(qn * qn) + head_sums(qr * qr)
    invq = lax.rsqrt(ssq * (1.0 / QK_DIM) + EPS) * SOFTMAX_SCALE
    spreadq = head_spread(invq)
    for hd in range(N_HEADS):
        cols = slice(hd * LANES, (hd + 1) * LANES)
        f = spreadq[:, cols]
        q_ref[hd, :, 0:LANES] = (qn[:, cols] * f * qg[:, :LANES]).astype(BF16)
        rot = _rope_block(qr[:, cols] * qg[:, LANES:], cos_tab, sin_tab)
        q_ref[hd, :, LANES:HEAD_PAD] = (rot * f).astype(BF16)


def _head_major(w, per_head, lo, hi):
    k = w.shape[0]
    w3 = w.reshape(k, N_HEADS, per_head)[:, :, lo:hi]
    w3 = jnp.pad(w3, ((0, 0), (0, 0), (0, LANES - (hi - lo))))
    return w3.reshape(k, N_HEADS * LANES)


def _mla_project(h, seq, cos_tab, sin_tab, kv_norm_g, w_dkv, kv_latent_g, w_kr, w_ukv, k_norm_g,
                 b_norm_g, w_dq, q_latent_g, w_uq, q_norm_g):
    t, d = h.shape
    tm = min(TOKEN_TILE, seq)
    kv_rank = w_dkv.shape[1]
    q_rank = w_dq.shape[1]
    hw = N_HEADS * LANES
    row = lambda a: a.reshape(1, -1)
    wkr = jnp.pad(w_kr, ((0, 0), (0, LANES - QK_ROPE_DIM))).astype(BF16)
    wuk = _head_major(w_ukv, QK_NOPE_DIM + V_HEAD_DIM, 0, QK_NOPE_DIM).astype(BF16)
    wuv = _head_major(w_ukv, QK_NOPE_DIM + V_HEAD_DIM, QK_NOPE_DIM, QK_NOPE_DIM + V_HEAD_DIM).astype(BF16)
    wuq = jnp.concatenate([_head_major(w_uq, QK_DIM, 0, QK_NOPE_DIM),
                           _head_major(w_uq, QK_DIM, QK_NOPE_DIM, QK_DIM)], axis=1).astype(BF16)
    pad_gain = lambda g: jnp.pad(g, (0, HEAD_PAD - QK_DIM)).reshape(1, HEAD_PAD)
    head_of = jnp.arange(hw, dtype=I32) // LANES
    hsum = (head_of[:, None] == jnp.arange(LANES, dtype=I32)[None, :]).astype(BF16)
    hexp = hsum.T
    tok = lambda w: pl.BlockSpec((tm, w), lambda i: (i, 0))
    heads = lambda w: pl.BlockSpec((N_HEADS, tm, w), lambda i: (0, i, 0))
    return pl.pallas_call(
        _proj_kernel,
        grid=(t // tm,),
        in_specs=[tok(d), tok(LANES), tok(LANES), _full((1, d)), _full((d, kv_rank)), _full((1, kv_rank)),
                  _full((d, LANES)), _full((kv_rank, hw)), _full((kv_rank, hw)), _full((1, HEAD_PAD)),
                  _full((1, d)), _full((d, q_rank)), _full((1, q_rank)), _full((q_rank, 2 * hw)),
                  _full((1, HEAD_PAD)), _full((hw, LANES)), _full((LANES, hw))],
        out_specs=[heads(HEAD_PAD), heads(HEAD_PAD), heads(LANES)],
        out_shape=[jax.ShapeDtypeStruct((N_HEADS, t, HEAD_PAD), BF16),
                   jax.ShapeDtypeStruct((N_HEADS, t, HEAD_PAD), BF16),
                   jax.ShapeDtypeStruct((N_HEADS, t, LANES), BF16)],
        compiler_params=_cparams(("arbitrary",)),
        name="mla_project",
    )(h, cos_tab, sin_tab, row(kv_norm_g), w_dkv.astype(BF16), row(kv_latent_g), wkr, wuk, wuv,
      pad_gain(k_norm_g), row(b_norm_g), w_dq.astype(BF16), row(q_latent_g), wuq, pad_gain(q_norm_g),
      hsum, hexp)


def _attn_kernel(qi_ref, ki_ref, q_ref, k_ref, v_ref, o_ref, m_ref, acc_ref, vext_ref):
    s_idx = pl.program_id(1)
    qi = qi_ref[s_idx]
    ki = ki_ref[s_idx]
    tq = q_ref.shape[1]
    tk = k_ref.shape[1]
    dv = V_HEAD_DIM
    k_lo = ki * tk - qi * tq

    @pl.when(ki == 0)
    def _():
        m_ref[...] = jnp.full_like(m_ref, -jnp.inf)
        acc_ref[...] = jnp.zeros_like(acc_ref)
        vext_ref[:, :, dv:] = jnp.ones((N_HEADS, tk, LANES), BF16)

    vext_ref[:, :, :dv] = v_ref[...]

    def step(masked):
        if masked:
            qc = lax.broadcasted_iota(I32, (tq, tk), 0) // CHUNK
            kc = (lax.broadcasted_iota(I32, (tq, tk), 1) + k_lo) // CHUNK
            allowed = kc <= qc
        for hd in range(N_HEADS):
            s = lax.dot_general(q_ref[hd], k_ref[hd], (((1,), (1,)), ((), ())), preferred_element_type=F32)
            if masked:
                s = jnp.where(allowed, s, MASK_VALUE)
            m_prev = m_ref[hd]
            m_new = jnp.maximum(m_prev, jnp.max(s, axis=-1, keepdims=True))
            alpha = jnp.exp(m_prev - m_new)
            p = jnp.exp(s - jnp.concatenate([m_new] * (tk // LANES), axis=1))
            pv = _dot(p.astype(BF16), vext_ref[hd])
            acc_ref[hd] = jnp.concatenate([alpha, alpha], axis=1) * acc_ref[hd] + pv
            m_ref[hd] = m_new

    @pl.when(k_lo < 0)
    def _():
        step(False)

    @pl.when(k_lo >= 0)
    def _():
        step(True)

    @pl.when(k_lo + tk == tq)
    def _():
        for hd in range(N_HEADS):
            acc = acc_ref[hd]
            o_ref[:, hd * dv:(hd + 1) * dv] = (acc[:, :dv] / acc[:, dv:]).astype(o_ref.dtype)


def _attention(q, k, v, batch, seq):
    t = q.shape[1]
    tk = min(ATTN_KEY_TILE, seq)
    tq = min(ATTN_QUERY_TILE, seq)
    nq = seq // tq
    nk = seq // tk
    per_q = tq // tk
    pairs = [(a, b) for a in range(nq) for b in range((a + 1) * per_q)]
    qi_tab = jnp.array([p[0] for p in pairs], I32)
    ki_tab = jnp.array([p[1] for p in pairs], I32)
    grid_spec = pltpu.PrefetchScalarGridSpec(
        num_scalar_prefetch=2,
        grid=(batch, len(pairs)),
        in_specs=[
            pl.BlockSpec((N_HEADS, tq, HEAD_PAD), lambda b, s, qi, ki: (0, b * nq + qi[s], 0)),
            pl.BlockSpec((N_HEADS, tk, HEAD_PAD), lambda b, s, qi, ki: (0, b * nk + ki[s], 0)),
            pl.BlockSpec((N_HEADS, tk, V_HEAD_DIM), lambda b, s, qi, ki: (0, b * nk + ki[s], 0)),
        ],
        out_specs=pl.BlockSpec((tq, N_HEADS * V_HEAD_DIM), lambda b, s, qi, ki: (b * nq + qi[s], 0)),
        scratch_shapes=[pltpu.VMEM((N_HEADS, tq, LANES), F32),
                        pltpu.VMEM((N_HEADS, tq, V_HEAD_DIM + LANES), F32),
                        pltpu.VMEM((N_HEADS, tk, V_HEAD_DIM + LANES), BF16)],
    )
    return pl.pallas_call(
        _attn_kernel,
        grid_spec=grid_spec,
        out_shape=jax.ShapeDtypeStruct((t, N_HEADS * V_HEAD_DIM), BF16),
        compiler_params=_cparams(("arbitrary", "arbitrary")),
        name="attention",
    )(qi_tab, ki_tab, q, k, v)


def _oproj_kernel(o_ref, h_ref, wo_ref, fng_ref, wrh_ref, wrl_ref, br_ref, tri_ref, upper_ref,
                  h_out_ref, xl_ref, route_ref, cnt_ref):
    h = h_ref[...] + _dot(o_ref[...], wo_ref[...])
    h_out_ref[...] = h
    xl, route, cnt = _route(h, fng_ref[...], wrh_ref[...], wrl_ref[...], br_ref[...], tri_ref[...], upper_ref[...])
    xl_ref[...] = xl
    route_ref[...] = route
    cnt_ref[0] = cnt


def _oproj_layer(o, h, seq, w_o, fng, wrh, wrl, br):
    t, d = h.shape
    tm = min(TOKEN_TILE, seq)
    r_in, r_out = _router_specs(tm, d)
    tok = lambda w: pl.BlockSpec((tm, w), lambda i: (i, 0))
    return pl.pallas_call(
        _oproj_kernel,
        grid=(t // tm,),
        in_specs=[tok(o.shape[1]), tok(d), _full(w_o.shape)] + r_in,
        out_specs=r_out,
        out_shape=_router_out_shapes(t, tm, d),
        compiler_params=_cparams(("arbitrary",)),
        name="oproj_router",
    )(o, h, w_o.astype(BF16), fng.reshape(1, d), wrh, wrl, br, _strict_lower(tm), _strict_lower(LANES).T)


def kernel(x, positions, a_norm_g, a_pw1_w, a_pw1_b, a_dw_w, a_dw_b, a_ln_g, a_ln_b, a_pw2_w, a_pw2_b, kv_norm_g, w_dkv, kv_latent_g, w_kr, w_ukv, k_norm_g, b_norm_g, w_dq, q_latent_g, w_uq, q_norm_g, w_o, ffn_norm_g, w_group, b_group, w_router, b_router, w_gate_up, w_down):
    batch, seq, d = x.shape
    t = batch * seq
    tm = min(TOKEN_TILE, seq)
    assert a_norm_g.shape[0] == 1 and b_norm_g.shape[0] == 1 and ffn_norm_g.shape[0] == 2
    cos_tab, sin_tab = _rope_tables(positions)

    wrh, wrl, br = _router_weights(w_group[0], b_group[0], w_router[0], b_router[0])
    h, xl, route, cnt = _conv_layer(x.reshape(t, d), seq, a_norm_g[0], a_pw1_w[0], a_pw1_b[0], a_dw_w[0],
                                    a_dw_b[0], a_ln_g[0], a_ln_b[0], a_pw2_w[0], a_pw2_b[0],
                                    ffn_norm_g[0], wrh, wrl, br)
    h = _moe_layer(h, xl, route, cnt, tm, 0, w_gate_up, w_down)

    q, k, v = _mla_project(h, seq, cos_tab, sin_tab, kv_norm_g, w_dkv, kv_latent_g, w_kr, w_ukv, k_norm_g,
                           b_norm_g[0], w_dq[0], q_latent_g[0], w_uq[0], q_norm_g[0])
    o = _attention(q, k, v, batch, seq)
    wrh, wrl, br = _router_weights(w_group[1], b_group[1], w_router[1], b_router[1])
    h, xl, route, cnt = _oproj_layer(o, h, seq, w_o[0], ffn_norm_g[1], wrh, wrl, br)
    h = _moe_layer(h, xl, route, cnt, tm, 1, w_gate_up, w_down)
    return h.reshape(batch, seq, d)
```

```python
import functools

import jax
import jax.numpy as jnp
from jax import lax
from jax.experimental import pallas as pl
from jax.experimental.pallas import tpu as pltpu

F32 = jnp.float32
BF16 = jnp.bfloat16
I32 = jnp.int32

EPS = 1e-6
CHUNK = 64
N_HEADS = 8
QK_NOPE_DIM = 128
QK_ROPE_DIM = 64
QK_DIM = QK_NOPE_DIM + QK_ROPE_DIM
V_HEAD_DIM = 128
ROPE_THETA = 10000.0
SOFTMAX_SCALE = QK_DIM ** -0.5
MASK_VALUE = -1e30
N_GROUPS = 4
EXPERTS_PER_GROUP = 8
N_EXPERTS = N_GROUPS * EXPERTS_PER_GROUP

LANES = 128
SUBLANES = 8
HEAD_PAD = 256
CONV_HALO = 32
CONV_ROWS = 64
TOKEN_TILE = 512
MOE_TILE = 512
ATTN_QUERY_TILE = 1024
ATTN_KEY_TILE = 512
VMEM_LIMIT = 56 * 1024 * 1024


def _cparams(sem):
    return pltpu.CompilerParams(dimension_semantics=sem, vmem_limit_bytes=VMEM_LIMIT)


def _rms(x, g):
    return x * lax.rsqrt(jnp.mean(x * x, axis=-1, keepdims=True) + EPS) * g


def _dot(a, b):
    return jnp.dot(a, b, preferred_element_type=F32)


def _split_bf16(x):
    hi = x.astype(BF16)
    lo = (x - hi.astype(F32)).astype(BF16)
    return hi, lo


def _full(shape):
    return pl.BlockSpec(shape, lambda *_: (0,) * len(shape))


def _local_rows(tm):
    return 2 * tm + N_EXPERTS * SUBLANES


def _rope_kernel(pos_ref, freq_ref, cos_ref, sin_ref):
    ang = pos_ref[...] * freq_ref[...]
    cos_ref[...] = jnp.cos(ang)
    sin_ref[...] = jnp.sin(ang)


def _rope_tables(positions):
    t = positions.size
    half = QK_ROPE_DIM // 2
    per_row = LANES // half
    inv_freq = ROPE_THETA ** (-jnp.arange(0, QK_ROPE_DIM, 2, dtype=F32) / QK_ROPE_DIM)
    pos = jnp.repeat(positions.reshape(t).astype(F32), half).reshape(t // per_row, LANES)
    freq = jnp.tile(inv_freq, per_row).reshape(1, LANES)
    rows = t // per_row
    blk = min(rows, 1024)
    cos, sin = pl.pallas_call(
        _rope_kernel,
        grid=(rows // blk,),
        in_specs=[pl.BlockSpec((blk, LANES), lambda i: (i, 0)), _full((1, LANES))],
        out_specs=[pl.BlockSpec((blk, LANES), lambda i: (i, 0))] * 2,
        out_shape=[jax.ShapeDtypeStruct((rows, LANES), F32)] * 2,
        compiler_params=_cparams(("arbitrary",)),
        name="rope_tables",
    )(pos, freq)
    cos = cos.reshape(t, half)
    sin = sin.reshape(t, half)
    zeros = jnp.zeros((t, LANES - QK_ROPE_DIM), F32)
    cos_tab = jnp.concatenate([cos, cos, zeros], axis=1)
    sin_tab = jnp.concatenate([-sin, sin, zeros], axis=1)
    return cos_tab, sin_tab


def _route(h, fng, wr_hi, wr_lo, br, tri, upper):
    tm = h.shape[0]
    xt = _rms(h, fng)
    x_hi, x_lo = _split_bf16(xt)
    logits = _dot(x_hi, wr_hi) + _dot(x_lo, wr_hi) + _dot(x_hi, wr_lo) + br
    lane = lax.broadcasted_iota(I32, (tm, LANES), 1).astype(F32)
    neg = -jnp.inf
    big = float(LANES)

    gl = jnp.where(lane >= N_EXPERTS, jnp.where(lane < N_EXPERTS + N_GROUPS, logits, neg), neg)
    gmax = jnp.max(gl, axis=1, keepdims=True)
    gidx = jnp.min(jnp.where(gl == gmax, lane, big), axis=1, keepdims=True) - N_EXPERTS
    g_w = 1.0 / jnp.sum(jnp.exp(gl - gmax), axis=1, keepdims=True)

    lo_lane = gidx * EXPERTS_PER_GROUP
    el = jnp.where(lane >= lo_lane, jnp.where(lane < lo_lane + EXPERTS_PER_GROUP, logits, neg), neg)
    m1 = jnp.max(el, axis=1, keepdims=True)
    i1 = jnp.min(jnp.where(el == m1, lane, big), axis=1, keepdims=True)
    el2 = jnp.where(lane == i1, neg, el)
    m2 = jnp.max(el2, axis=1, keepdims=True)
    i2 = jnp.min(jnp.where(el2 == m2, lane, big), axis=1, keepdims=True)
    p2 = jnp.exp(m2 - m1)
    den = 1.0 + p2
    w1 = g_w / den
    w2 = g_w * p2 / den

    sel1 = lane == i1
    sel2 = lane == i2
    onehot = jnp.where(sel1, 1.0, jnp.where(sel2, 1.0, 0.0))
    before = _dot(tri, onehot.astype(BF16))
    cnt = jnp.sum(onehot, axis=0, keepdims=True)
    groups = jnp.floor((cnt + (SUBLANES - 1)) * (1.0 / SUBLANES))
    start = SUBLANES * _dot(jnp.broadcast_to(groups, (SUBLANES, LANES)).astype(BF16), upper)[0:1]
    at = before + start
    l1 = jnp.sum(jnp.where(sel1, at, 0.0), axis=1, keepdims=True)
    l2 = jnp.sum(jnp.where(sel2, at, 0.0), axis=1, keepdims=True)

    route = jnp.where(lane == 0, i1, jnp.where(lane == 1, i2, jnp.where(lane == 2, w1, jnp.where(
        lane == 3, w2, jnp.where(lane == 4, l1, jnp.where(lane == 5, l2, 0.0))))))
    route_t = route.T
    rows = lax.broadcasted_iota(I32, (_local_rows(tm), tm), 0).astype(F32)
    perm = jnp.where(rows == route_t[4:5], 1.0, jnp.where(rows == route_t[5:6], 1.0, 0.0)).astype(BF16)
    return _dot(perm, x_hi), route, cnt


def _router_weights(w_group, b_group, w_router, b_router):
    d = w_group.shape[0]
    pad = LANES - N_EXPERTS - N_GROUPS
    w = jnp.concatenate([w_router, w_group, jnp.zeros((d, pad), F32)], axis=1)
    b = jnp.concatenate([b_router, b_group, jnp.zeros((pad,), F32)]).reshape(1, LANES)
    hi, lo = _split_bf16(w)
    return hi, lo, b


def _strict_lower(n):
    r = lax.broadcasted_iota(I32, (n, n), 0)
    c = lax.broadcasted_iota(I32, (n, n), 1)
    return (c < r).astype(BF16)


def _router_specs(tm, d):
    tok = lambda w: pl.BlockSpec((tm, w), lambda i: (i, 0))
    in_specs = [_full((1, d)), _full((d, LANES)), _full((d, LANES)), _full((1, LANES)), _full((tm, tm)),
                _full((LANES, LANES))]
    out_specs = [tok(d), pl.BlockSpec((_local_rows(tm), d), lambda i: (i, 0)), tok(LANES),
                 pl.BlockSpec((1, 1, LANES), lambda i: (i, 0, 0))]
    return in_specs, out_specs


def _router_out_shapes(t, tm, d):
    n = t // tm
    return [jax.ShapeDtypeStruct((t, d), F32), jax.ShapeDtypeStruct((n * _local_rows(tm), d), F32),
            jax.ShapeDtypeStruct((t, LANES), F32), jax.ShapeDtypeStruct((n, 1, LANES), F32)]


def _conv_kernel(x_ref, ng_ref, w1_ref, b1_ref, dw_ref, dwb_ref, lng_ref, lnb_ref, w2_ref, b2_ref,
                 fng_ref, wrh_ref, wrl_ref, br_ref, tri_ref, upper_ref,
                 h_ref, xl_ref, route_ref, cnt_ref,
                 ubuf, cbuf, *, tiles_per_seq, width):
    i = pl.program_id(0)
    tm, d = x_ref.shape
    n_strips = d // LANES

    @pl.when(i % tiles_per_seq == 0)
    def _():
        ubuf[:, 0:CONV_HALO, :] = jnp.zeros((n_strips, CONV_HALO, LANES), F32)

    @pl.when(i % tiles_per_seq != 0)
    def _():
        ubuf[:, 0:CONV_HALO, :] = ubuf[:, tm:tm + CONV_HALO, :]

    x = x_ref[...]
    hn = _rms(x, ng_ref[...])
    ag = _dot(hn.astype(BF16), w1_ref[...]) + b1_ref[...]
    u = ag[:, :d] * jax.nn.sigmoid(ag[:, d:])
    for c in range(n_strips):
        ubuf[c, CONV_HALO:CONV_HALO + tm, :] = u[:, c * LANES:(c + 1) * LANES]

    first = CONV_HALO - (width - 1)
    for c in range(n_strips):
        cols = slice(c * LANES, (c + 1) * LANES)

        for r in range(tm // CONV_ROWS):
            base = r * CONV_ROWS
            acc = jnp.zeros((CONV_ROWS, LANES), F32)
            for t in range(width):
                acc = acc + dw_ref[t:t + 1, cols] * ubuf[c, base + first + t:base + first + t + CONV_ROWS, :]
            cbuf[base:base + CONV_ROWS, cols] = acc

    v = cbuf[...] + dwb_ref[...]
    mu = jnp.mean(v, axis=-1, keepdims=True)
    vc = v - mu
    var = jnp.mean(vc * vc, axis=-1, keepdims=True)
    y = vc * lax.rsqrt(var + EPS) * lng_ref[...] + lnb_ref[...]
    y = y * jax.nn.sigmoid(y)
    h = x + _dot(y.astype(BF16), w2_ref[...]) + b2_ref[...]
    h_ref[...] = h

    xl, route, cnt = _route(h, fng_ref[...], wrh_ref[...], wrl_ref[...], br_ref[...], tri_ref[...], upper_ref[...])
    xl_ref[...] = xl
    route_ref[...] = route
    cnt_ref[0] = cnt


def _conv_layer(x, seq, ng, w1, b1, dw, dwb, lng, lnb, w2, b2, fng, wrh, wrl, br):
    t, d = x.shape
    tm = min(TOKEN_TILE, seq)
    width = dw.shape[0]
    dw_p = jnp.concatenate([dw, jnp.zeros((CONV_HALO - width, d), F32)], axis=0)
    row = lambda a: a.reshape(1, -1)
    r_in, r_out = _router_specs(tm, d)
    kern = functools.partial(_conv_kernel, tiles_per_seq=seq // tm, width=width)
    return pl.pallas_call(
        kern,
        grid=(t // tm,),
        in_specs=[pl.BlockSpec((tm, d), lambda i: (i, 0)), _full((1, d)), _full((d, 2 * d)), _full((1, 2 * d)),
                  _full((CONV_HALO, d)), _full((1, d)), _full((1, d)), _full((1, d)), _full((d, d)),
                  _full((1, d))] + r_in,
        out_specs=r_out,
        out_shape=_router_out_shapes(t, tm, d),
        scratch_shapes=[pltpu.VMEM((d // LANES, CONV_HALO + tm, LANES), F32), pltpu.VMEM((tm, d), F32)],
        compiler_params=_cparams(("arbitrary",)),
        name="conv_router",
    )(x, row(ng), w1.astype(BF16), row(b1), dw_p, row(dwb), row(lng), row(lnb), w2.astype(BF16), row(b2),
      row(fng), wrh, wrl, br, _strict_lower(tm), _strict_lower(LANES).T)


def _moe_plan(cnt, tm):
    n_tt = cnt.shape[0]
    c = cnt[:, 0, :N_EXPERTS].astype(I32)
    run = (c + SUBLANES - 1) // SUBLANES
    local = jnp.cumsum(run, axis=1) - run
    per_tile = MOE_TILE // SUBLANES
    n_exp = jnp.sum(run, axis=0)
    n_pad = ((n_exp + per_tile - 1) // per_tile) * per_tile
    e_end = jnp.cumsum(n_pad)
    e_base = e_end - n_pad
    dst = e_base[None, :] + jnp.cumsum(run, axis=0) - run
    src = jnp.arange(n_tt, dtype=I32)[:, None] * (_local_rows(tm) // SUBLANES) + local
    max_groups = (2 * n_tt * tm) // SUBLANES + n_tt * N_EXPERTS + N_EXPERTS * (per_tile - 1)
    n_tiles = -(-max_groups // per_tile)
    tile_start = jnp.minimum(jnp.arange(n_tiles, dtype=I32) * per_tile, e_end[-1] - per_tile)
    tile_expert = jnp.sum((tile_start[:, None] >= e_end[None, :]).astype(I32), axis=1)
    g0 = jnp.arange(n_tiles, dtype=I32) * per_tile
    run_start = jnp.take(dst, tile_expert, axis=1)
    run_end = jnp.take(dst + run, tile_expert, axis=1)
    scan_lo = jnp.sum((run_end <= g0[None, :]).astype(I32), axis=0)
    scan_hi = jnp.sum((run_start < (g0 + per_tile)[None, :]).astype(I32), axis=0)
    return dict(run=run.reshape(-1), dst=dst.reshape(-1), src=src.reshape(-1),
                last=(e_base + n_exp).astype(I32), tile_expert=tile_expert, scan_lo=scan_lo, scan_hi=scan_hi,
                n_used=(e_end[-1] // per_tile).reshape(1).astype(I32), n_tiles=n_tiles)


def _group(ref, g):
    return ref.at[pl.ds(pl.multiple_of(g * SUBLANES, SUBLANES), SUBLANES)]


def _moe_kernel(te_ref, nu_ref, run_ref, dst_ref, src_ref, last_ref, lo_ref, hi_ref, xl_hbm, wgu_ref, wd_ref,
                y_ref, xbuf, sems, wgu_bf, wd_bf):
    j = pl.program_id(0)
    n_used = nu_ref[0]
    de = wd_ref.shape[2]
    per_tile = MOE_TILE // SUBLANES
    slot = j % 2

    def copy(g_src, g_dst, s):
        return pltpu.make_async_copy(_group(xl_hbm, g_src), _group(xbuf.at[s], g_dst), sems.at[s])

    def issue(tile, s):
        e = te_ref[tile]
        g0 = tile * per_tile

        def per_run(i, c):
            r = i * N_EXPERTS + e
            d0 = dst_ref[r]
            s0 = src_ref[r]

            def body(k, c2):
                copy(s0 + k - d0, k - g0, s).start()
                return c2
            lax.fori_loop(jnp.maximum(d0, g0), jnp.minimum(d0 + run_ref[r], g0 + per_tile), body, 0)
            return c
        lax.fori_loop(lo_ref[tile], hi_ref[tile], per_run, 0)

    @pl.when(j == 0)
    def _():
        xbuf[...] = jnp.zeros_like(xbuf)
        issue(0, 0)

    @pl.when(j + 1 < n_used)
    def _():
        issue(j + 1, 1 - slot)

    @pl.when(j < n_used)
    def _():
        e = te_ref[j]
        prev = te_ref[jnp.maximum(j - 1, 0)]

        @pl.when(jnp.logical_or(j == 0, e != prev))
        def _():
            wgu_bf[...] = wgu_ref[0, 0].astype(BF16)
            wd_bf[...] = wd_ref[0, 0].astype(BF16)

        def wait(_, c):
            copy(0, 0, slot).wait()
            return c
        lax.fori_loop(0, jnp.clip(last_ref[e] - j * per_tile, 0, per_tile), wait, 0)

        gu = _dot(xbuf[slot].astype(BF16), wgu_bf[...])
        a = gu[:, :de]
        b = gu[:, de:]
        mid = (a * jax.nn.sigmoid(a) * b).astype(BF16)
        y_ref[...] = _dot(mid, wd_bf[...])

    @pl.when(j >= n_used)
    def _():
        y_ref[...] = jnp.zeros_like(y_ref)


def _moe_experts(xl, plan, layer, w_gate_up, w_down):
    d = xl.shape[1]
    tm = MOE_TILE
    n_tiles = plan["n_tiles"]
    de = w_down.shape[2]
    grid_spec = pltpu.PrefetchScalarGridSpec(
        num_scalar_prefetch=8,
        grid=(n_tiles,),
        in_specs=[
            pl.BlockSpec(memory_space=pl.ANY),
            pl.BlockSpec((1, 1, d, 2 * de), lambda j, te, *_: (layer, te[j], 0, 0)),
            pl.BlockSpec((1, 1, de, d), lambda j, te, *_: (layer, te[j], 0, 0)),
        ],
        out_specs=pl.BlockSpec((tm, d), lambda j, *_: (j, 0)),
        scratch_shapes=[pltpu.VMEM((2, tm, d), F32), pltpu.SemaphoreType.DMA((2,)),
                        pltpu.VMEM((d, 2 * de), BF16), pltpu.VMEM((de, d), BF16)],
    )
    return pl.pallas_call(
        _moe_kernel,
        grid_spec=grid_spec,
        out_shape=jax.ShapeDtypeStruct((n_tiles * tm, d), F32),
        compiler_params=_cparams(("arbitrary",)),
        name="moe_experts",
    )(plan["tile_expert"], plan["n_used"], plan["run"], plan["dst"], plan["src"], plan["last"],
      plan["scan_lo"], plan["scan_hi"], xl, w_gate_up, w_down)


def _combine_kernel(run_ref, dst_ref, h_ref, route_ref, y_hbm, o_ref, ybuf, sems):
    i = pl.program_id(0)
    n = pl.num_programs(0)
    tm = o_ref.shape[0]
    rows = ybuf.shape[1]
    slot = i % 2

    def copy(g_src, g_dst, s):
        return pltpu.make_async_copy(_group(y_hbm, g_src), _group(ybuf.at[s], g_dst), sems.at[s])

    def issue(tile, s):
        def per_expert(e, off):
            r = tile * N_EXPERTS + e
            d0 = dst_ref[r]
            cnt = run_ref[r]

            def body(k, c):
                copy(d0 + k, off + k, s).start()
                return c
            lax.fori_loop(0, cnt, body, 0)
            return off + cnt
        return lax.fori_loop(0, N_EXPERTS, per_expert, 0)

    @pl.when(i == 0)
    def _():
        issue(0, 0)

    @pl.when(i + 1 < n)
    def _():
        issue(i + 1, 1 - slot)

    def count(e, tot):
        return tot + run_ref[i * N_EXPERTS + e]
    used = lax.fori_loop(0, N_EXPERTS, count, 0)

    def wait(_, c):
        copy(0, 0, slot).wait()
        return c
    lax.fori_loop(0, used, wait, 0)

    def clear(g, c):
        ybuf[slot, pl.ds(pl.multiple_of(g * SUBLANES, SUBLANES), SUBLANES), :] = jnp.zeros(
            (SUBLANES, ybuf.shape[2]), F32)
        return c
    lax.fori_loop(used, rows // SUBLANES, clear, 0)

    route = route_ref[...]
    route_t = route.T
    row = lax.broadcasted_iota(I32, (rows, tm), 0).astype(F32)
    row_gate = jnp.sum(jnp.where(row == route_t[4:5], route_t[2:3],
                                 jnp.where(row == route_t[5:6], route_t[3:4], 0.0)), axis=1, keepdims=True)
    y = (ybuf[slot] * row_gate).astype(BF16)
    col = lax.broadcasted_iota(I32, (tm, rows), 1).astype(F32)
    pick = jnp.where(col == route[:, 4:5], 1.0, jnp.where(col == route[:, 5:6], 1.0, 0.0)).astype(BF16)
    o_ref[...] = h_ref[...] + _dot(pick, y)


def _moe_combine(h, route, y, plan, tm):
    t, d = h.shape
    n = t // tm
    grid_spec = pltpu.PrefetchScalarGridSpec(
        num_scalar_prefetch=2,
        grid=(n,),
        in_specs=[pl.BlockSpec((tm, d), lambda i, r, s: (i, 0)), pl.BlockSpec((tm, LANES), lambda i, r, s: (i, 0)),
                  pl.BlockSpec(memory_space=pl.ANY)],
        out_specs=pl.BlockSpec((tm, d), lambda i, r, s: (i, 0)),
        scratch_shapes=[pltpu.VMEM((2, _local_rows(tm), d), F32), pltpu.SemaphoreType.DMA((2,))],
    )
    return pl.pallas_call(
        _combine_kernel,
        grid_spec=grid_spec,
        out_shape=jax.ShapeDtypeStruct((t, d), F32),
        compiler_params=_cparams(("arbitrary",)),
        name="moe_combine",
    )(plan["run"], plan["dst"], h, route, y)


def _moe_layer(h, xl, route, cnt, tm, layer, w_gate_up, w_down):
    plan = _moe_plan(cnt, tm)
    y = _moe_experts(xl, plan, layer, w_gate_up, w_down)
    return _moe_combine(h, route, y, plan, tm)


def _rope_block(x, cos_tab, sin_tab):
    lane = lax.broadcasted_iota(I32, x.shape, 1)
    half = QK_ROPE_DIM // 2
    up = pltpu.roll(x, LANES - half, 1)
    down = pltpu.roll(x, half, 1)
    partner = jnp.where(lane < half, up, jnp.where(lane < QK_ROPE_DIM, down, 0.0))
    return x * cos_tab + partner * sin_tab


def _proj_kernel(h_ref, cos_ref, sin_ref, kvg_ref, wdkv_ref, kvlg_ref, wkr_ref, wuk_ref, wuv_ref, kng_ref,
                 qng_ref, wdq_ref, qlg_ref, wuq_ref, qg_ref, hsum_ref, hexp_ref,
                 q_ref, k_ref, v_ref):
    h = h_ref[...]
    cos_tab = cos_ref[...]
    sin_tab = sin_ref[...]
    hsum = hsum_ref[...]
    hexp = hexp_ref[...]

    def head_sums(sq):
        return _dot(sq.astype(BF16), hsum)

    def head_spread(val):
        hi, lo = _split_bf16(val)
        return _dot(hi, hexp) + _dot(lo, hexp)

    hn = _rms(h, kvg_ref[...]).astype(BF16)
    c_kv = _rms(_dot(hn, wdkv_ref[...]), kvlg_ref[...]).astype(BF16)
    kn = _dot(c_kv, wuk_ref[...])
    vv = _dot(c_kv, wuv_ref[...])
    kr = _dot(hn, wkr_ref[...])
    kng = kng_ref[...]
    ss = head_sums(kn * kn) + jnp.sum(kr * kr, axis=-1, keepdims=True)
    inv = lax.rsqrt(ss * (1.0 / QK_DIM) + EPS)
    spread = head_spread(inv)
    kr_rot = _rope_block(kr * kng[:, LANES:], cos_tab, sin_tab)
    for hd in range(N_HEADS):
        cols = slice(hd * LANES, (hd + 1) * LANES)
        f = spread[:, cols]
        k_ref[hd, :, 0:LANES] = (kn[:, cols] * f * kng[:, :LANES]).astype(BF16)
        k_ref[hd, :, LANES:HEAD_PAD] = (kr_rot * f).astype(BF16)
        v_ref[hd] = vv[:, cols].astype(BF16)

    hq = _rms(h, qng_ref[...]).astype(BF16)
    c_q = _rms(_dot(hq, wdq_ref[...]), qlg_ref[...]).astype(BF16)
    qn = _dot(c_q, wuq_ref[:, 0:N_HEADS * LANES])
    qr = _dot(c_q, wuq_ref[:, N_HEADS * LANES:2 * N_HEADS * LANES])
    qg = qg_ref[...]
    ssq = head_sums(qn * qn) + head_sums(qr * qr)
    invq = lax.rsqrt(ssq * (1.0 / QK_DIM) + EPS) * SOFTMAX_SCALE
    spreadq = head_spread(invq)
    for hd in range(N_HEADS):
        cols = slice(hd * LANES, (hd + 1) * LANES)
        f = spreadq[:, cols]
        q_ref[hd, :, 0:LANES] = (qn[:, cols] * f * qg[:, :LANES]).astype(BF16)
        rot = _rope_block(qr[:, cols] * qg[:, LANES:], cos_tab, sin_tab)
        q_ref[hd, :, LANES:HEAD_PAD] = (rot * f).astype(BF16)


def _head_major(w, per_head, lo, hi):
    k = w.shape[0]
    w3 = w.reshape(k, N_HEADS, per_head)[:, :, lo:hi]
    w3 = jnp.pad(w3, ((0, 0), (0, 0), (0, LANES - (hi - lo))))
    return w3.reshape(k, N_HEADS * LANES)


def _mla_project(h, seq, cos_tab, sin_tab, kv_norm_g, w_dkv, kv_latent_g, w_kr, w_ukv, k_norm_g,
                 b_norm_g, w_dq, q_latent_g, w_uq, q_norm_g):
    t, d = h.shape
    tm = min(TOKEN_TILE, seq)
    kv_rank = w_dkv.shape[1]
    q_rank = w_dq.shape[1]
    hw = N_HEADS * LANES
    row = lambda a: a.reshape(1, -1)
    wkr = jnp.pad(w_kr, ((0, 0), (0, LANES - QK_ROPE_DIM))).astype(BF16)
    wuk = _head_major(w_ukv, QK_NOPE_DIM + V_HEAD_DIM, 0, QK_NOPE_DIM).astype(BF16)
    wuv = _head_major(w_ukv, QK_NOPE_DIM + V_HEAD_DIM, QK_NOPE_DIM, QK_NOPE_DIM + V_HEAD_DIM).astype(BF16)
    wuq = jnp.concatenate([_head_major(w_uq, QK_DIM, 0, QK_NOPE_DIM),
                           _head_major(w_uq, QK_DIM, QK_NOPE_DIM, QK_DIM)], axis=1).astype(BF16)
    pad_gain = lambda g: jnp.pad(g, (0, HEAD_PAD - QK_DIM)).reshape(1, HEAD_PAD)
    head_of = jnp.arange(hw, dtype=I32) // LANES
    hsum = (head_of[:, None] == jnp.arange(LANES, dtype=I32)[None, :]).astype(BF16)
    hexp = hsum.T
    tok = lambda w: pl.BlockSpec((tm, w), lambda i: (i, 0))
    heads = lambda w: pl.BlockSpec((N_HEADS, tm, w), lambda i: (0, i, 0))
    return pl.pallas_call(
        _proj_kernel,
        grid=(t // tm,),
        in_specs=[tok(d), tok(LANES), tok(LANES), _full((1, d)), _full((d, kv_rank)), _full((1, kv_rank)),
                  _full((d, LANES)), _full((kv_rank, hw)), _full((kv_rank, hw)), _full((1, HEAD_PAD)),
                  _full((1, d)), _full((d, q_rank)), _full((1, q_rank)), _full((q_rank, 2 * hw)),
                  _full((1, HEAD_PAD)), _full((hw, LANES)), _full((LANES, hw))],
        out_specs=[heads(HEAD_PAD), heads(HEAD_PAD), heads(LANES)],
        out_shape=[jax.ShapeDtypeStruct((N_HEADS, t, HEAD_PAD), BF16),
                   jax.ShapeDtypeStruct((N_HEADS, t, HEAD_PAD), BF16),
                   jax.ShapeDtypeStruct((N_HEADS, t, LANES), BF16)],
        compiler_params=_cparams(("arbitrary",)),
        name="mla_project",
    )(h, cos_tab, sin_tab, row(kv_norm_g), w_dkv.astype(BF16), row(kv_latent_g), wkr, wuk, wuv,
      pad_gain(k_norm_g), row(b_norm_g), w_dq.astype(BF16), row(q_latent_g), wuq, pad_gain(q_norm_g),
      hsum, hexp)


def _attn_kernel(qi_ref, ki_ref, q_ref, k_ref, v_ref, o_ref, m_ref, acc_ref, vext_ref):
    s_idx = pl.program_id(1)
    qi = qi_ref[s_idx]
    ki = ki_ref[s_idx]
    tq = q_ref.shape[1]
    tk = k_ref.shape[1]
    dv = V_HEAD_DIM
    k_lo = ki * tk - qi * tq

    @pl.when(ki == 0)
    def _():
        m_ref[...] = jnp.full_like(m_ref, -jnp.inf)
        acc_ref[...] = jnp.zeros_like(acc_ref)
        vext_ref[:, :, dv:] = jnp.ones((N_HEADS, tk, LANES), BF16)

    vext_ref[:, :, :dv] = v_ref[...]

    def step(masked):
        if masked:
            qc = lax.broadcasted_iota(I32, (tq, tk), 0) // CHUNK
            kc = (lax.broadcasted_iota(I32, (tq, tk), 1) + k_lo) // CHUNK
            allowed = kc <= qc
        for hd in range(N_HEADS):
            s = lax.dot_general(q_ref[hd], k_ref[hd], (((1,), (1,)), ((), ())), preferred_element_type=F32)
            if masked:
                s = jnp.where(allowed, s, MASK_VALUE)
            m_prev = m_ref[hd]
            m_new = jnp.maximum(m_prev, jnp.max(s, axis=-1, keepdims=True))
            alpha = jnp.exp(m_prev - m_new)
            p = jnp.exp(s - jnp.concatenate([m_new] * (tk // LANES), axis=1))
            pv = _dot(p.astype(BF16), vext_ref[hd])
            acc_ref[hd] = jnp.concatenate([alpha, alpha], axis=1) * acc_ref[hd] + pv
            m_ref[hd] = m_new

    @pl.when(k_lo < 0)
    def _():
        step(False)

    @pl.when(k_lo >= 0)
    def _():
        step(True)

    @pl.when(k_lo + tk == tq)
    def _():
        for hd in range(N_HEADS):
            acc = acc_ref[hd]
            o_ref[:, hd * dv:(hd + 1) * dv] = (acc[:, :dv] / acc[:, dv:]).astype(o_ref.dtype)


def _attention(q, k, v, batch, seq):
    t = q.shape[1]
    tk = min(ATTN_KEY_TILE, seq)
    tq = min(ATTN_QUERY_TILE, seq)
    nq = seq // tq
    nk = seq // tk
    per_q = tq // tk
    pairs = [(a, b) for a in range(nq) for b in range((a + 1) * per_q)]
    qi_tab = jnp.array([p[0] for p in pairs], I32)
    ki_tab = jnp.array([p[1] for p in pairs], I32)
    grid_spec = pltpu.PrefetchScalarGridSpec(
        num_scalar_prefetch=2,
        grid=(batch, len(pairs)),
        in_specs=[
            pl.BlockSpec((N_HEADS, tq, HEAD_PAD), lambda b, s, qi, ki: (0, b * nq + qi[s], 0)),
            pl.BlockSpec((N_HEADS, tk, HEAD_PAD), lambda b, s, qi, ki: (0, b * nk + ki[s], 0)),
            pl.BlockSpec((N_HEADS, tk, V_HEAD_DIM), lambda b, s, qi, ki: (0, b * nk + ki[s], 0)),
        ],
        out_specs=pl.BlockSpec((tq, N_HEADS * V_HEAD_DIM), lambda b, s, qi, ki: (b * nq + qi[s], 0)),
        scratch_shapes=[pltpu.VMEM((N_HEADS, tq, LANES), F32),
                        pltpu.VMEM((N_HEADS, tq, V_HEAD_DIM + LANES), F32),
                        pltpu.VMEM((N_HEADS, tk, V_HEAD_DIM + LANES), BF16)],
    )
    return pl.pallas_call(
        _attn_kernel,
        grid_spec=grid_spec,
        out_shape=jax.ShapeDtypeStruct((t, N_HEADS * V_HEAD_DIM), BF16),
        compiler_params=_cparams(("arbitrary", "arbitrary")),
        name="attention",
    )(qi_tab, ki_tab, q, k, v)


def _oproj_kernel(o_ref, h_ref, wo_ref, fng_ref, wrh_ref, wrl_ref, br_ref, tri_ref, upper_ref,
                  h_out_ref, xl_ref, route_ref, cnt_ref):
    h = h_ref[...] + _dot(o_ref[...], wo_ref[...])
    h_out_ref[...] = h
    xl, route, cnt = _route(h, fng_ref[...], wrh_ref[...], wrl_ref[...], br_ref[...], tri_ref[...], upper_ref[...])
    xl_ref[...] = xl
    route_ref[...] = route
    cnt_ref[0] = cnt


def _oproj_layer(o, h, seq, w_o, fng, wrh, wrl, br):
    t, d = h.shape
    tm = min(TOKEN_TILE, seq)
    r_in, r_out = _router_specs(tm, d)
    tok = lambda w: pl.BlockSpec((tm, w), lambda i: (i, 0))
    return pl.pallas_call(
        _oproj_kernel,
        grid=(t // tm,),
        in_specs=[tok(o.shape[1]), tok(d), _full(w_o.shape)] + r_in,
        out_specs=r_out,
        out_shape=_router_out_shapes(t, tm, d),
        compiler_params=_cparams(("arbitrary",)),
        name="oproj_router",
    )(o, h, w_o.astype(BF16), fng.reshape(1, d), wrh, wrl, br, _strict_lower(tm), _strict_lower(LANES).T)


def kernel(x, positions, a_norm_g, a_pw1_w, a_pw1_b, a_dw_w, a_dw_b, a_ln_g, a_ln_b, a_pw2_w, a_pw2_b, kv_norm_g, w_dkv, kv_latent_g, w_kr, w_ukv, k_norm_g, b_norm_g, w_dq, q_latent_g, w_uq, q_norm_g, w_o, ffn_norm_g, w_group, b_group, w_router, b_router, w_gate_up, w_down):
    batch, seq, d = x.shape
    t = batch * seq
    tm = min(TOKEN_TILE, seq)
    assert a_norm_g.shape[0] == 1 and b_norm_g.shape[0] == 1 and ffn_norm_g.shape[0] == 2
    cos_tab, sin_tab = _rope_tables(positions)

    wrh, wrl, br = _router_weights(w_group[0], b_group[0], w_router[0], b_router[0])
    h, xl, route, cnt = _conv_layer(x.reshape(t, d), seq, a_norm_g[0], a_pw1_w[0], a_pw1_b[0], a_dw_w[0],
                                    a_dw_b[0], a_ln_g[0], a_ln_b[0], a_pw2_w[0], a_pw2_b[0],
                                    ffn_norm_g[0], wrh, wrl, br)
    h = _moe_layer(h, xl, route, cnt, tm, 0, w_gate_up, w_down)

    q, k, v = _mla_project(h, seq, cos_tab, sin_tab, kv_norm_g, w_dkv, kv_latent_g, w_kr, w_ukv, k_norm_g,
                           b_norm_g[0], w_dq[0], q_latent_g[0], w_uq[0], q_norm_g[0])
    o = _attention(q, k, v, batch, seq)
    wrh, wrl, br = _router_weights(w_group[1], b_group[1], w_router[1], b_router[1])
    h, xl, route, cnt = _oproj_layer(o, h, seq, w_o[0], ffn_norm_g[1], wrh, wrl, br)
    h = _moe_layer(h, xl, route, cnt, tm, 1, w_gate_up, w_down)
    return h.reshape(batch, seq, d)
```

```python
import functools

import jax
import jax.numpy as jnp
from jax import lax
from jax.experimental import pallas as pl
from jax.experimental.pallas import tpu as pltpu

F32 = jnp.float32
BF16 = jnp.bfloat16
I32 = jnp.int32

EPS = 1e-6
CHUNK = 64
N_HEADS = 8
QK_NOPE_DIM = 128
QK_ROPE_DIM = 64
QK_DIM = QK_NOPE_DIM + QK_ROPE_DIM
V_HEAD_DIM = 128
ROPE_THETA = 10000.0
SOFTMAX_SCALE = QK_DIM ** -0.5
MASK_VALUE = -1e30
N_GROUPS = 4
EXPERTS_PER_GROUP = 8
N_EXPERTS = N_GROUPS * EXPERTS_PER_GROUP

LANES = 128
SUBLANES = 8
HEAD_PAD = 256
CONV_HALO = 32
CONV_ROWS = 64
TOKEN_TILE = 512
MOE_TILE = 512
ATTN_QUERY_TILE = 1024
ATTN_KEY_TILE = 512
VMEM_LIMIT = 56 * 1024 * 1024


def _cparams(sem):
    return pltpu.CompilerParams(dimension_semantics=sem, vmem_limit_bytes=VMEM_LIMIT)


def _rms(x, g):
    return x * lax.rsqrt(jnp.mean(x * x, axis=-1, keepdims=True) + EPS) * g


def _dot(a, b):
    return jnp.dot(a, b, preferred_element_type=F32)


def _split_bf16(x):
    hi = x.astype(BF16)
    lo = (x - hi.astype(F32)).astype(BF16)
    return hi, lo


def _full(shape):
    return pl.BlockSpec(shape, lambda *_: (0,) * len(shape))


def _local_rows(tm):
    return 2 * tm + N_EXPERTS * SUBLANES


def _rope_kernel(pos_ref, freq_ref, cos_ref, sin_ref):
    ang = pos_ref[...] * freq_ref[...]
    cos_ref[...] = jnp.cos(ang)
    sin_ref[...] = jnp.sin(ang)


def _rope_tables(positions):
    t = positions.size
    half = QK_ROPE_DIM // 2
    per_row = LANES // half
    inv_freq = ROPE_THETA ** (-jnp.arange(0, QK_ROPE_DIM, 2, dtype=F32) / QK_ROPE_DIM)
    pos = jnp.repeat(positions.reshape(t).astype(F32), half).reshape(t // per_row, LANES)
    freq = jnp.tile(inv_freq, per_row).reshape(1, LANES)
    rows = t // per_row
    blk = min(rows, 1024)
    cos, sin = pl.pallas_call(
        _rope_kernel,
        grid=(rows // blk,),
        in_specs=[pl.BlockSpec((blk, LANES), lambda i: (i, 0)), _full((1, LANES))],
        out_specs=[pl.BlockSpec((blk, LANES), lambda i: (i, 0))] * 2,
        out_shape=[jax.ShapeDtypeStruct((rows, LANES), F32)] * 2,
        compiler_params=_cparams(("arbitrary",)),
        name="rope_tables",
    )(pos, freq)
    cos = cos.reshape(t, half)
    sin = sin.reshape(t, half)
    zeros = jnp.zeros((t, LANES - QK_ROPE_DIM), F32)
    cos_tab = jnp.concatenate([cos, cos, zeros], axis=1)
    sin_tab = jnp.concatenate([-sin, sin, zeros], axis=1)
    return cos_tab, sin_tab


def _route(h, fng, wr_hi, wr_lo, br, tri, upper):
    tm = h.shape[0]
    xt = _rms(h, fng)
    x_hi, x_lo = _split_bf16(xt)
    logits = _dot(x_hi, wr_hi) + _dot(x_lo, wr_hi) + _dot(x_hi, wr_lo) + br
    lane = lax.broadcasted_iota(I32, (tm, LANES), 1).astype(F32)
    neg = -jnp.inf
    big = float(LANES)

    gl = jnp.where(lane >= N_EXPERTS, jnp.where(lane < N_EXPERTS + N_GROUPS, logits, neg), neg)
    gmax = jnp.max(gl, axis=1, keepdims=True)
    gidx = jnp.min(jnp.where(gl == gmax, lane, big), axis=1, keepdims=True) - N_EXPERTS
    g_w = 1.0 / jnp.sum(jnp.exp(gl - gmax), axis=1, keepdims=True)

    lo_lane = gidx * EXPERTS_PER_GROUP
    el = jnp.where(lane >= lo_lane, jnp.where(lane < lo_lane + EXPERTS_PER_GROUP, logits, neg), neg)
    m1 = jnp.max(el, axis=1, keepdims=True)
    i1 = jnp.min(jnp.where(el == m1, lane, big), axis=1, keepdims=True)
    el2 = jnp.where(lane == i1, neg, el)
    m2 = jnp.max(el2, axis=1, keepdims=True)
    i2 = jnp.min(jnp.where(el2 == m2, lane, big), axis=1, keepdims=True)
    p2 = jnp.exp(m2 - m1)
    den = 1.0 + p2
    w1 = g_w / den
    w2 = g_w * p2 / den

    sel1 = lane == i1
    sel2 = lane == i2
    onehot = jnp.where(sel1, 1.0, jnp.where(sel2, 1.0, 0.0))
    before = _dot(tri, onehot.astype(BF16))
    cnt = jnp.sum(onehot, axis=0, keepdims=True)
    groups = jnp.floor((cnt + (SUBLANES - 1)) * (1.0 / SUBLANES))
    start = SUBLANES * _dot(jnp.broadcast_to(groups, (SUBLANES, LANES)).astype(BF16), upper)[0:1]
    at = before + start
    l1 = jnp.sum(jnp.where(sel1, at, 0.0), axis=1, keepdims=True)
    l2 = jnp.sum(jnp.where(sel2, at, 0.0), axis=1, keepdims=True)

    route = jnp.where(lane == 0, i1, jnp.where(lane == 1, i2, jnp.where(lane == 2, w1, jnp.where(
        lane == 3, w2, jnp.where(lane == 4, l1, jnp.where(lane == 5, l2, 0.0))))))
    route_t = route.T
    rows = lax.broadcasted_iota(I32, (_local_rows(tm), tm), 0).astype(F32)
    perm = jnp.where(rows == route_t[4:5], 1.0, jnp.where(rows == route_t[5:6], 1.0, 0.0)).astype(BF16)
    return _dot(perm, x_hi), route, cnt


def _router_weights(w_group, b_group, w_router, b_router):
    d = w_group.shape[0]
    pad = LANES - N_EXPERTS - N_GROUPS
    w = jnp.concatenate([w_router, w_group, jnp.zeros((d, pad), F32)], axis=1)
    b = jnp.concatenate([b_router, b_group, jnp.zeros((pad,), F32)]).reshape(1, LANES)
    hi, lo = _split_bf16(w)
    return hi, lo, b


def _strict_lower(n):
    r = lax.broadcasted_iota(I32, (n, n), 0)
    c = lax.broadcasted_iota(I32, (n, n), 1)
    return (c < r).astype(BF16)


def _router_specs(tm, d):
    tok = lambda w: pl.BlockSpec((tm, w), lambda i: (i, 0))
    in_specs = [_full((1, d)), _full((d, LANES)), _full((d, LANES)), _full((1, LANES)), _full((tm, tm)),
                _full((LANES, LANES))]
    out_specs = [tok(d), pl.BlockSpec((_local_rows(tm), d), lambda i: (i, 0)), tok(LANES),
                 pl.BlockSpec((1, 1, LANES), lambda i: (i, 0, 0))]
    return in_specs, out_specs


def _router_out_shapes(t, tm, d):
    n = t // tm
    return [jax.ShapeDtypeStruct((t, d), F32), jax.ShapeDtypeStruct((n * _local_rows(tm), d), F32),
            jax.ShapeDtypeStruct((t, LANES), F32), jax.ShapeDtypeStruct((n, 1, LANES), F32)]


def _conv_kernel(x_ref, ng_ref, w1_ref, b1_ref, dw_ref, dwb_ref, lng_ref, lnb_ref, w2_ref, b2_ref,
                 fng_ref, wrh_ref, wrl_ref, br_ref, tri_ref, upper_ref,
                 h_ref, xl_ref, route_ref, cnt_ref,
                 ubuf, cbuf, *, tiles_per_seq, width):
    i = pl.program_id(0)
    tm, d = x_ref.shape
    n_strips = d // LANES

    @pl.when(i % tiles_per_seq == 0)
    def _():
        ubuf[:, 0:CONV_HALO, :] = jnp.zeros((n_strips, CONV_HALO, LANES), F32)

    @pl.when(i % tiles_per_seq != 0)
    def _():
        ubuf[:, 0:CONV_HALO, :] = ubuf[:, tm:tm + CONV_HALO, :]

    x = x_ref[...]
    hn = _rms(x, ng_ref[...])
    ag = _dot(hn.astype(BF16), w1_ref[...]) + b1_ref[...]
    u = ag[:, :d] * jax.nn.sigmoid(ag[:, d:])
    for c in range(n_strips):
        ubuf[c, CONV_HALO:CONV_HALO + tm, :] = u[:, c * LANES:(c + 1) * LANES]

    first = CONV_HALO - (width - 1)
    for c in range(n_strips):
        cols = slice(c * LANES, (c + 1) * LANES)

        for r in range(tm // CONV_ROWS):
            base = r * CONV_ROWS
            acc = jnp.zeros((CONV_ROWS, LANES), F32)
            for t in range(width):
                acc = acc + dw_ref[t:t + 1, cols] * ubuf[c, base + first + t:base + first + t + CONV_ROWS, :]
            cbuf[base:base + CONV_ROWS, cols] = acc

    v = cbuf[...] + dwb_ref[...]
    mu = jnp.mean(v, axis=-1, keepdims=True)
    vc = v - mu
    var = jnp.mean(vc * vc, axis=-1, keepdims=True)
    y = vc * lax.rsqrt(var + EPS) * lng_ref[...] + lnb_ref[...]
    y = y * jax.nn.sigmoid(y)
    h = x + _dot(y.astype(BF16), w2_ref[...]) + b2_ref[...]
    h_ref[...] = h

    xl, route, cnt = _route(h, fng_ref[...], wrh_ref[...], wrl_ref[...], br_ref[...], tri_ref[...], upper_ref[...])
    xl_ref[...] = xl
    route_ref[...] = route
    cnt_ref[0] = cnt


def _conv_layer(x, seq, ng, w1, b1, dw, dwb, lng, lnb, w2, b2, fng, wrh, wrl, br):
    t, d = x.shape
    tm = min(TOKEN_TILE, seq)
    width = dw.shape[0]
    dw_p = jnp.concatenate([dw, jnp.zeros((CONV_HALO - width, d), F32)], axis=0)
    row = lambda a: a.reshape(1, -1)
    r_in, r_out = _router_specs(tm, d)
    kern = functools.partial(_conv_kernel, tiles_per_seq=seq // tm, width=width)
    return pl.pallas_call(
        kern,
        grid=(t // tm,),
        in_specs=[pl.BlockSpec((tm, d), lambda i: (i, 0)), _full((1, d)), _full((d, 2 * d)), _full((1, 2 * d)),
                  _full((CONV_HALO, d)), _full((1, d)), _full((1, d)), _full((1, d)), _full((d, d)),
                  _full((1, d))] + r_in,
        out_specs=r_out,
        out_shape=_router_out_shapes(t, tm, d),
        scratch_shapes=[pltpu.VMEM((d // LANES, CONV_HALO + tm, LANES), F32), pltpu.VMEM((tm, d), F32)],
        compiler_params=_cparams(("arbitrary",)),
        name="conv_router",
    )(x, row(ng), w1.astype(BF16), row(b1), dw_p, row(dwb), row(lng), row(lnb), w2.astype(BF16), row(b2),
      row(fng), wrh, wrl, br, _strict_lower(tm), _strict_lower(LANES).T)


def _moe_plan(cnt, tm):
    n_tt = cnt.shape[0]
    c = cnt[:, 0, :N_EXPERTS].astype(I32)
    run = (c + SUBLANES - 1) // SUBLANES
    local = jnp.cumsum(run, axis=1) - run
    per_tile = MOE_TILE // SUBLANES
    n_exp = jnp.sum(run, axis=0)
    n_pad = ((n_exp + per_tile - 1) // per_tile) * per_tile
    e_end = jnp.cumsum(n_pad)
    e_base = e_end - n_pad
    dst = e_base[None, :] + jnp.cumsum(run, axis=0) - run
    src = jnp.arange(n_tt, dtype=I32)[:, None] * (_local_rows(tm) // SUBLANES) + local
    max_groups = (2 * n_tt * tm) // SUBLANES + n_tt * N_EXPERTS + N_EXPERTS * (per_tile - 1)
    n_tiles = -(-max_groups // per_tile)
    tile_start = jnp.minimum(jnp.arange(n_tiles, dtype=I32) * per_tile, e_end[-1] - per_tile)
    tile_expert = jnp.sum((tile_start[:, None] >= e_end[None, :]).astype(I32), axis=1)
    g = jnp.arange(n_tiles * per_tile, dtype=I32).reshape(n_tiles, per_tile)
    rs = jnp.take(dst, tile_expert, axis=1)[:, :, None]
    re = rs + jnp.take(run, tile_expert, axis=1)[:, :, None]
    shift = jnp.take(src - dst, tile_expert, axis=1)[:, :, None]
    fetch_x = jnp.sum(jnp.where(jnp.logical_and(rs <= g[None], g[None] < re), shift, 0), axis=0) + g
    n_x = jnp.clip(jnp.take(e_base + n_exp, tile_expert) - g[:, 0], 0, per_tile)
    lg = _local_rows(tm) // SUBLANES
    p = jnp.arange(lg, dtype=I32)[None, :, None]
    lo = local[:, None, :]
    inside = jnp.logical_and(lo <= p, p < lo + run[:, None, :])
    fetch_y = jnp.sum(jnp.where(inside, (dst - local)[:, None, :], 0), axis=2) + p[:, :, 0]
    return dict(fetch_x=fetch_x.reshape(n_tiles, 1, per_tile), n_x=n_x.astype(I32),
                fetch_y=fetch_y.reshape(n_tt, 1, lg), n_y=jnp.sum(run, axis=1).astype(I32),
                tile_expert=tile_expert, n_used=(e_end[-1] // per_tile).reshape(1).astype(I32), n_tiles=n_tiles)


def _group(ref, g):
    return ref.at[pl.ds(pl.multiple_of(g * SUBLANES, SUBLANES), SUBLANES)]


def _wait_groups(src_hbm, dst, sem, count, max_count):
    bit = 1
    while bit <= max_count:
        @pl.when((count & bit) != 0)
        def _(bit=bit):
            n = bit * SUBLANES
            pltpu.make_async_copy(src_hbm.at[pl.ds(0, n)], dst.at[pl.ds(0, n)], sem).wait()
        bit *= 2


def _moe_kernel(te_ref, nu_ref, nx_ref, fx_ref, fx_next_ref, xl_hbm, wgu_ref, wd_ref,
                y_ref, xbuf, sems, wgu_bf, wd_bf):
    j = pl.program_id(0)
    n_used = nu_ref[0]
    de = wd_ref.shape[2]
    slot = j % 2

    def copy(g_src, g_dst, s):
        return pltpu.make_async_copy(_group(xl_hbm, g_src), _group(xbuf.at[s], g_dst), sems.at[s])

    def issue(tab_ref, tile, s):
        def body(k, c):
            copy(tab_ref[0, 0, k], k, s).start()
            return c
        lax.fori_loop(0, nx_ref[tile], body, 0)

    @pl.when(j == 0)
    def _():
        xbuf[...] = jnp.zeros_like(xbuf)
        issue(fx_ref, 0, 0)

    @pl.when(j + 1 < n_used)
    def _():
        issue(fx_next_ref, j + 1, 1 - slot)

    @pl.when(j < n_used)
    def _():
        e = te_ref[j]
        prev = te_ref[jnp.maximum(j - 1, 0)]

        @pl.when(jnp.logical_or(j == 0, e != prev))
        def _():
            wgu_bf[...] = wgu_ref[0, 0].astype(BF16)
            wd_bf[...] = wd_ref[0, 0].astype(BF16)

        _wait_groups(xl_hbm, xbuf.at[slot], sems.at[slot], nx_ref[j], MOE_TILE // SUBLANES)

        gu = _dot(xbuf[slot].astype(BF16), wgu_bf[...])
        a = gu[:, :de]
        b = gu[:, de:]
        mid = (a * jax.nn.sigmoid(a) * b).astype(BF16)
        y_ref[...] = _dot(mid, wd_bf[...])

    @pl.when(j >= n_used)
    def _():
        y_ref[...] = jnp.zeros_like(y_ref)


def _moe_experts(xl, plan, layer, w_gate_up, w_down):
    d = xl.shape[1]
    tm = MOE_TILE
    n_tiles = plan["n_tiles"]
    de = w_down.shape[2]
    per_tile = tm // SUBLANES
    grid_spec = pltpu.PrefetchScalarGridSpec(
        num_scalar_prefetch=3,
        grid=(n_tiles,),
        in_specs=[
            pl.BlockSpec((1, 1, per_tile), lambda j, *_: (j, 0, 0), memory_space=pltpu.SMEM),
            pl.BlockSpec((1, 1, per_tile), lambda j, *_: (jnp.minimum(j + 1, n_tiles - 1), 0, 0),
                         memory_space=pltpu.SMEM),
            pl.BlockSpec(memory_space=pl.ANY),
            pl.BlockSpec((1, 1, d, 2 * de), lambda j, te, *_: (layer, te[j], 0, 0)),
            pl.BlockSpec((1, 1, de, d), lambda j, te, *_: (layer, te[j], 0, 0)),
        ],
        out_specs=pl.BlockSpec((tm, d), lambda j, *_: (j, 0)),
        scratch_shapes=[pltpu.VMEM((2, tm, d), F32), pltpu.SemaphoreType.DMA((2,)),
                        pltpu.VMEM((d, 2 * de), BF16), pltpu.VMEM((de, d), BF16)],
    )
    return pl.pallas_call(
        _moe_kernel,
        grid_spec=grid_spec,
        out_shape=jax.ShapeDtypeStruct((n_tiles * tm, d), F32),
        compiler_params=_cparams(("arbitrary",)),
        name="moe_experts",
    )(plan["tile_expert"], plan["n_used"], plan["n_x"], plan["fetch_x"], plan["fetch_x"], xl, w_gate_up, w_down)


def _combine_kernel(ny_ref, fy_ref, fy_next_ref, h_ref, route_ref, y_hbm, o_ref, ybuf, sems):
    i = pl.program_id(0)
    n = pl.num_programs(0)
    tm = o_ref.shape[0]
    rows = ybuf.shape[1]
    slot = i % 2

    def copy(g_src, g_dst, s):
        return pltpu.make_async_copy(_group(y_hbm, g_src), _group(ybuf.at[s], g_dst), sems.at[s])

    def issue(tab_ref, tile, s):
        def body(k, c):
            copy(tab_ref[0, 0, k], k, s).start()
            return c
        lax.fori_loop(0, ny_ref[tile], body, 0)

    @pl.when(i == 0)
    def _():
        issue(fy_ref, 0, 0)

    @pl.when(i + 1 < n)
    def _():
        issue(fy_next_ref, i + 1, 1 - slot)

    used = ny_ref[i]

    _wait_groups(y_hbm, ybuf.at[slot], sems.at[slot], used, rows // SUBLANES)

    def clear(g, c):
        ybuf[slot, pl.ds(pl.multiple_of(g * SUBLANES, SUBLANES), SUBLANES), :] = jnp.zeros(
            (SUBLANES, ybuf.shape[2]), F32)
        return c
    lax.fori_loop(used, rows // SUBLANES, clear, 0)

    route = route_ref[...]
    route_t = route.T
    row = lax.broadcasted_iota(I32, (rows, tm), 0).astype(F32)
    row_gate = jnp.sum(jnp.where(row == route_t[4:5], route_t[2:3],
                                 jnp.where(row == route_t[5:6], route_t[3:4], 0.0)), axis=1, keepdims=True)
    y = (ybuf[slot] * row_gate).astype(BF16)
    col = lax.broadcasted_iota(I32, (tm, rows), 1).astype(F32)
    pick = jnp.where(col == route[:, 4:5], 1.0, jnp.where(col == route[:, 5:6], 1.0, 0.0)).astype(BF16)
    o_ref[...] = h_ref[...] + _dot(pick, y)


def _moe_combine(h, route, y, plan, tm):
    t, d = h.shape
    n = t // tm
    lg = _local_rows(tm) // SUBLANES
    grid_spec = pltpu.PrefetchScalarGridSpec(
        num_scalar_prefetch=1,
        grid=(n,),
        in_specs=[pl.BlockSpec((1, 1, lg), lambda i, c: (i, 0, 0), memory_space=pltpu.SMEM),
                  pl.BlockSpec((1, 1, lg), lambda i, c: (jnp.minimum(i + 1, n - 1), 0, 0),
                               memory_space=pltpu.SMEM),
                  pl.BlockSpec((tm, d), lambda i, c: (i, 0)), pl.BlockSpec((tm, LANES), lambda i, c: (i, 0)),
                  pl.BlockSpec(memory_space=pl.ANY)],
        out_specs=pl.BlockSpec((tm, d), lambda i, c: (i, 0)),
        scratch_shapes=[pltpu.VMEM((2, _local_rows(tm), d), F32), pltpu.SemaphoreType.DMA((2,))],
    )
    return pl.pallas_call(
        _combine_kernel,
        grid_spec=grid_spec,
        out_shape=jax.ShapeDtypeStruct((t, d), F32),
        compiler_params=_cparams(("arbitrary",)),
        name="moe_combine",
    )(plan["n_y"], plan["fetch_y"], plan["fetch_y"], h, route, y)


def _moe_layer(h, xl, route, cnt, tm, layer, w_gate_up, w_down):
    plan = _moe_plan(cnt, tm)
    y = _moe_experts(xl, plan, layer, w_gate_up, w_down)
    return _moe_combine(h, route, y, plan, tm)


def _rope_block(x, cos_tab, sin_tab):
    lane = lax.broadcasted_iota(I32, x.shape, 1)
    half = QK_ROPE_DIM // 2
    up = pltpu.roll(x, LANES - half, 1)
    down = pltpu.roll(x, half, 1)
    partner = jnp.where(lane < half, up, jnp.where(lane < QK_ROPE_DIM, down, 0.0))
    return x * cos_tab + partner * sin_tab


def _proj_kernel(h_ref, cos_ref, sin_ref, kvg_ref, wdkv_ref, kvlg_ref, wkr_ref, wuk_ref, wuv_ref, kng_ref,
                 qng_ref, wdq_ref, qlg_ref, wuq_ref, qg_ref, hsum_ref, hexp_ref,
                 q_ref, k_ref, v_ref):
    h = h_ref[...]
    cos_tab = cos_ref[...]
    sin_tab = sin_ref[...]
    hsum = hsum_ref[...]
    hexp = hexp_ref[...]

    def head_sums(sq):
        return _dot(sq.astype(BF16), hsum)

    def head_spread(val):
        hi, lo = _split_bf16(val)
        return _dot(hi, hexp) + _dot(lo, hexp)

    hn = _rms(h, kvg_ref[...]).astype(BF16)
    c_kv = _rms(_dot(hn, wdkv_ref[...]), kvlg_ref[...]).astype(BF16)
    kn = _dot(c_kv, wuk_ref[...])
    vv = _dot(c_kv, wuv_ref[...])
    kr = _dot(hn, wkr_ref[...])
    kng = kng_ref[...]
    ss = head_sums(kn * kn) + jnp.sum(kr * kr, axis=-1, keepdims=True)
    inv = lax.rsqrt(ss * (1.0 / QK_DIM) + EPS)
    spread = head_spread(inv)
    kr_rot = _rope_block(kr * kng[:, LANES:], cos_tab, sin_tab)
    for hd in range(N_HEADS):
        cols = slice(hd * LANES, (hd + 1) * LANES)
        f = spread[:, cols]
        k_ref[hd, :, 0:LANES] = (kn[:, cols] * f * kng[:, :LANES]).astype(BF16)
        k_ref[hd, :, LANES:HEAD_PAD] = (kr_rot * f).astype(BF16)
        v_ref[hd] = vv[:, cols].astype(BF16)

    hq = _rms(h, qng_ref[...]).astype(BF16)
    c_q = _rms(_dot(hq, wdq_ref[...]), qlg_ref[...]).astype(BF16)
    qn = _dot(c_q, wuq_ref[:, 0:N_HEADS * LANES])
    qr = _dot(c_q, wuq_ref[:, N_HEADS * LANES:2 * N_HEADS * LANES])
    qg = qg_ref[...]
    ssq = head_sums(qn * qn) + head_sums(qr * qr)
    invq = lax.rsqrt(ssq * (1.0 / QK_DIM) + EPS) * SOFTMAX_SCALE
    spreadq = head_spread(invq)
    for hd in range(N_HEADS):
        cols = slice(hd * LANES, (hd + 1) * LANES)
        f = spreadq[:, cols]
        q_ref[hd, :, 0:LANES] = (qn[:, cols] * f * qg[:, :LANES]).astype(BF16)
        rot = _rope_block(qr[:, cols] * qg[:, LANES:], cos_tab, sin_tab)
        q_ref[hd, :, LANES:HEAD_PAD] = (rot * f).astype(BF16)


def _head_major(w, per_head, lo, hi):
    k = w.shape[0]
    w3 = w.reshape(k, N_HEADS, per_head)[:, :, lo:hi]
    w3 = jnp.pad(w3, ((0, 0), (0, 0), (0, LANES - (hi - lo))))
    return w3.reshape(k, N_HEADS * LANES)


def _mla_project(h, seq, cos_tab, sin_tab, kv_norm_g, w_dkv, kv_latent_g, w_kr, w_ukv, k_norm_g,
                 b_norm_g, w_dq, q_latent_g, w_uq, q_norm_g):
    t, d = h.shape
    tm = min(TOKEN_TILE, seq)
    kv_rank = w_dkv.shape[1]
    q_rank = w_dq.shape[1]
    hw = N_HEADS * LANES
    row = lambda a: a.reshape(1, -1)
    wkr = jnp.pad(w_kr, ((0, 0), (0, LANES - QK_ROPE_DIM))).astype(BF16)
    wuk = _head_major(w_ukv, QK_NOPE_DIM + V_HEAD_DIM, 0, QK_NOPE_DIM).astype(BF16)
    wuv = _head_major(w_ukv, QK_NOPE_DIM + V_HEAD_DIM, QK_NOPE_DIM, QK_NOPE_DIM + V_HEAD_DIM).astype(BF16)
    wuq = jnp.concatenate([_head_major(w_uq, QK_DIM, 0, QK_NOPE_DIM),
                           _head_major(w_uq, QK_DIM, QK_NOPE_DIM, QK_DIM)], axis=1).astype(BF16)
    pad_gain = lambda g: jnp.pad(g, (0, HEAD_PAD - QK_DIM)).reshape(1, HEAD_PAD)
    head_of = jnp.arange(hw, dtype=I32) // LANES
    hsum = (head_of[:, None] == jnp.arange(LANES, dtype=I32)[None, :]).astype(BF16)
    hexp = hsum.T
    tok = lambda w: pl.BlockSpec((tm, w), lambda i: (i, 0))
    heads = lambda w: pl.BlockSpec((N_HEADS, tm, w), lambda i: (0, i, 0))
    return pl.pallas_call(
        _proj_kernel,
        grid=(t // tm,),
        in_specs=[tok(d), tok(LANES), tok(LANES), _full((1, d)), _full((d, kv_rank)), _full((1, kv_rank)),
                  _full((d, LANES)), _full((kv_rank, hw)), _full((kv_rank, hw)), _full((1, HEAD_PAD)),
                  _full((1, d)), _full((d, q_rank)), _full((1, q_rank)), _full((q_rank, 2 * hw)),
                  _full((1, HEAD_PAD)), _full((hw, LANES)), _full((LANES, hw))],
        out_specs=[heads(HEAD_PAD), heads(HEAD_PAD), heads(LANES)],
        out_shape=[jax.ShapeDtypeStruct((N_HEADS, t, HEAD_PAD), BF16),
                   jax.ShapeDtypeStruct((N_HEADS, t, HEAD_PAD), BF16),
                   jax.ShapeDtypeStruct((N_HEADS, t, LANES), BF16)],
        compiler_params=_cparams(("arbitrary",)),
        name="mla_project",
    )(h, cos_tab, sin_tab, row(kv_norm_g), w_dkv.astype(BF16), row(kv_latent_g), wkr, wuk, wuv,
      pad_gain(k_norm_g), row(b_norm_g), w_dq.astype(BF16), row(q_latent_g), wuq, pad_gain(q_norm_g),
      hsum, hexp)


def _attn_kernel(qi_ref, ki_ref, q_ref, k_ref, v_ref, o_ref, m_ref, acc_ref, vext_ref):
    s_idx = pl.program_id(1)
    qi = qi_ref[s_idx]
    ki = ki_ref[s_idx]
    tq = q_ref.shape[1]
    tk = k_ref.shape[1]
    dv = V_HEAD_DIM
    k_lo = ki * tk - qi * tq

    @pl.when(ki == 0)
    def _():
        m_ref[...] = jnp.full_like(m_ref, -jnp.inf)
        acc_ref[...] = jnp.zeros_like(acc_ref)
        vext_ref[:, :, dv:] = jnp.ones((N_HEADS, tk, LANES), BF16)

    vext_ref[:, :, :dv] = v_ref[...]

    def step(masked):
        if masked:
            qc = lax.broadcasted_iota(I32, (tq, tk), 0) // CHUNK
            kc = (lax.broadcasted_iota(I32, (tq, tk), 1) + k_lo) // CHUNK
            allowed = kc <= qc
        for hd in range(N_HEADS):
            s = lax.dot_general(q_ref[hd], k_ref[hd], (((1,), (1,)), ((), ())), preferred_element_type=F32)
            if masked:
                s = jnp.where(allowed, s, MASK_VALUE)
            m_prev = m_ref[hd]
            m_new = jnp.maximum(m_prev, jnp.max(s, axis=-1, keepdims=True))
            alpha = jnp.exp(m_prev - m_new)
            p = jnp.exp(s - jnp.concatenate([m_new] * (tk // LANES), axis=1))
            pv = _dot(p.astype(BF16), vext_ref[hd])
            acc_ref[hd] = jnp.concatenate([alpha, alpha], axis=1) * acc_ref[hd] + pv
            m_ref[hd] = m_new

    @pl.when(k_lo < 0)
    def _():
        step(False)

    @pl.when(k_lo >= 0)
    def _():
        step(True)

    @pl.when(k_lo + tk == tq)
    def _():
        for hd in range(N_HEADS):
            acc = acc_ref[hd]
            o_ref[:, hd * dv:(hd + 1) * dv] = (acc[:, :dv] / acc[:, dv:]).astype(o_ref.dtype)


def _attention(q, k, v, batch, seq):
    t = q.shape[1]
    tk = min(ATTN_KEY_TILE, seq)
    tq = min(ATTN_QUERY_TILE, seq)
    nq = seq // tq
    nk = seq // tk
    per_q = tq // tk
    pairs = [(a, b) for a in range(nq) for b in range((a + 1) * per_q)]
    qi_tab = jnp.array([p[0] for p in pairs], I32)
    ki_tab = jnp.array([p[1] for p in pairs], I32)
    grid_spec = pltpu.PrefetchScalarGridSpec(
        num_scalar_prefetch=2,
        grid=(batch, len(pairs)),
        in_specs=[
            pl.BlockSpec((N_HEADS, tq, HEAD_PAD), lambda b, s, qi, ki: (0, b * nq + qi[s], 0)),
            pl.BlockSpec((N_HEADS, tk, HEAD_PAD), lambda b, s, qi, ki: (0, b * nk + ki[s], 0)),
            pl.BlockSpec((N_HEADS, tk, V_HEAD_DIM), lambda b, s, qi, ki: (0, b * nk + ki[s], 0)),
        ],
        out_specs=pl.BlockSpec((tq, N_HEADS * V_HEAD_DIM), lambda b, s, qi, ki: (b * nq + qi[s], 0)),
        scratch_shapes=[pltpu.VMEM((N_HEADS, tq, LANES), F32),
                        pltpu.VMEM((N_HEADS, tq, V_HEAD_DIM + LANES), F32),
                        pltpu.VMEM((N_HEADS, tk, V_HEAD_DIM + LANES), BF16)],
    )
    return pl.pallas_call(
        _attn_kernel,
        grid_spec=grid_spec,
        out_shape=jax.ShapeDtypeStruct((t, N_HEADS * V_HEAD_DIM), BF16),
        compiler_params=_cparams(("arbitrary", "arbitrary")),
        name="attention",
    )(qi_tab, ki_tab, q, k, v)


def _oproj_kernel(o_ref, h_ref, wo_ref, fng_ref, wrh_ref, wrl_ref, br_ref, tri_ref, upper_ref,
                  h_out_ref, xl_ref, route_ref, cnt_ref):
    h = h_ref[...] + _dot(o_ref[...], wo_ref[...])
    h_out_ref[...] = h
    xl, route, cnt = _route(h, fng_ref[...], wrh_ref[...], wrl_ref[...], br_ref[...], tri_ref[...], upper_ref[...])
    xl_ref[...] = xl
    route_ref[...] = route
    cnt_ref[0] = cnt


def _oproj_layer(o, h, seq, w_o, fng, wrh, wrl, br):
    t, d = h.shape
    tm = min(TOKEN_TILE, seq)
    r_in, r_out = _router_specs(tm, d)
    tok = lambda w: pl.BlockSpec((tm, w), lambda i: (i, 0))
    return pl.pallas_call(
        _oproj_kernel,
        grid=(t // tm,),
        in_specs=[tok(o.shape[1]), tok(d), _full(w_o.shape)] + r_in,
        out_specs=r_out,
        out_shape=_router_out_shapes(t, tm, d),
        compiler_params=_cparams(("arbitrary",)),
        name="oproj_router",
    )(o, h, w_o.astype(BF16), fng.reshape(1, d), wrh, wrl, br, _strict_lower(tm), _strict_lower(LANES).T)


def kernel(x, positions, a_norm_g, a_pw1_w, a_pw1_b, a_dw_w, a_dw_b, a_ln_g, a_ln_b, a_pw2_w, a_pw2_b, kv_norm_g, w_dkv, kv_latent_g, w_kr, w_ukv, k_norm_g, b_norm_g, w_dq, q_latent_g, w_uq, q_norm_g, w_o, ffn_norm_g, w_group, b_group, w_router, b_router, w_gate_up, w_down):
    batch, seq, d = x.shape
    t = batch * seq
    tm = min(TOKEN_TILE, seq)
    assert a_norm_g.shape[0] == 1 and b_norm_g.shape[0] == 1 and ffn_norm_g.shape[0] == 2
    cos_tab, sin_tab = _rope_tables(positions)

    wrh, wrl, br = _router_weights(w_group[0], b_group[0], w_router[0], b_router[0])
    h, xl, route, cnt = _conv_layer(x.reshape(t, d), seq, a_norm_g[0], a_pw1_w[0], a_pw1_b[0], a_dw_w[0],
                                    a_dw_b[0], a_ln_g[0], a_ln_b[0], a_pw2_w[0], a_pw2_b[0],
                                    ffn_norm_g[0], wrh, wrl, br)
    h = _moe_layer(h, xl, route, cnt, tm, 0, w_gate_up, w_down)

    q, k, v = _mla_project(h, seq, cos_tab, sin_tab, kv_norm_g, w_dkv, kv_latent_g, w_kr, w_ukv, k_norm_g,
                           b_norm_g[0], w_dq[0], q_latent_g[0], w_uq[0], q_norm_g[0])
    o = _attention(q, k, v, batch, seq)
    wrh, wrl, br = _router_weights(w_group[1], b_group[1], w_router[1], b_router[1])
    h, xl, route, cnt = _oproj_layer(o, h, seq, w_o[0], ffn_norm_g[1], wrh, wrl, br)
    h = _moe_layer(h, xl, route, cnt, tm, 1, w_gate_up, w_down)
    return h.reshape(batch, seq, d)
```

```python
import functools

import jax
import jax.numpy as jnp
from jax import lax
from jax.experimental import pallas as pl
from jax.experimental.pallas import tpu as pltpu

F32 = jnp.float32
BF16 = jnp.bfloat16
I32 = jnp.int32

EPS = 1e-6
CHUNK = 64
N_HEADS = 8
QK_NOPE_DIM = 128
QK_ROPE_DIM = 64
QK_DIM = QK_NOPE_DIM + QK_ROPE_DIM
V_HEAD_DIM = 128
ROPE_THETA = 10000.0
SOFTMAX_SCALE = QK_DIM ** -0.5
MASK_VALUE = -1e30
N_GROUPS = 4
EXPERTS_PER_GROUP = 8
N_EXPERTS = N_GROUPS * EXPERTS_PER_GROUP

LANES = 128
SUBLANES = 8
HEAD_PAD = 256
CONV_HALO = 32
CONV_ROWS = 64
TOKEN_TILE = 512
MOE_TILE = 512
ATTN_QUERY_TILE = 1024
ATTN_KEY_TILE = 512
VMEM_LIMIT = 56 * 1024 * 1024


def _cparams(sem):
    return pltpu.CompilerParams(dimension_semantics=sem, vmem_limit_bytes=VMEM_LIMIT)


def _rms(x, g):
    return x * lax.rsqrt(jnp.mean(x * x, axis=-1, keepdims=True) + EPS) * g


def _dot(a, b):
    return jnp.dot(a, b, preferred_element_type=F32)


def _split_bf16(x):
    hi = x.astype(BF16)
    lo = (x - hi.astype(F32)).astype(BF16)
    return hi, lo


def _full(shape):
    return pl.BlockSpec(shape, lambda *_: (0,) * len(shape))


def _local_rows(tm):
    return 2 * tm + N_EXPERTS * SUBLANES


def _rope_kernel(pos_ref, freq_ref, cos_ref, sin_ref):
    ang = pos_ref[...] * freq_ref[...]
    cos_ref[...] = jnp.cos(ang)
    sin_ref[...] = jnp.sin(ang)


def _rope_tables(positions):
    t = positions.size
    half = QK_ROPE_DIM // 2
    per_row = LANES // half
    inv_freq = ROPE_THETA ** (-jnp.arange(0, QK_ROPE_DIM, 2, dtype=F32) / QK_ROPE_DIM)
    pos = jnp.repeat(positions.reshape(t).astype(F32), half).reshape(t // per_row, LANES)
    freq = jnp.tile(inv_freq, per_row).reshape(1, LANES)
    rows = t // per_row
    blk = min(rows, 1024)
    cos, sin = pl.pallas_call(
        _rope_kernel,
        grid=(rows // blk,),
        in_specs=[pl.BlockSpec((blk, LANES), lambda i: (i, 0)), _full((1, LANES))],
        out_specs=[pl.BlockSpec((blk, LANES), lambda i: (i, 0))] * 2,
        out_shape=[jax.ShapeDtypeStruct((rows, LANES), F32)] * 2,
        compiler_params=_cparams(("arbitrary",)),
        name="rope_tables",
    )(pos, freq)
    cos = cos.reshape(t, half)
    sin = sin.reshape(t, half)
    zeros = jnp.zeros((t, LANES - QK_ROPE_DIM), F32)
    cos_tab = jnp.concatenate([cos, cos, zeros], axis=1)
    sin_tab = jnp.concatenate([-sin, sin, zeros], axis=1)
    return cos_tab, sin_tab


def _route(h, fng, wr_hi, wr_lo, br, tri, upper):
    tm = h.shape[0]
    xt = _rms(h, fng)
    x_hi, x_lo = _split_bf16(xt)
    logits = _dot(x_hi, wr_hi) + _dot(x_lo, wr_hi) + _dot(x_hi, wr_lo) + br
    lane = lax.broadcasted_iota(I32, (tm, LANES), 1).astype(F32)
    neg = -jnp.inf
    big = float(LANES)

    gl = jnp.where(lane >= N_EXPERTS, jnp.where(lane < N_EXPERTS + N_GROUPS, logits, neg), neg)
    gmax = jnp.max(gl, axis=1, keepdims=True)
    gidx = jnp.min(jnp.where(gl == gmax, lane, big), axis=1, keepdims=True) - N_EXPERTS
    g_w = 1.0 / jnp.sum(jnp.exp(gl - gmax), axis=1, keepdims=True)

    lo_lane = gidx * EXPERTS_PER_GROUP
    el = jnp.where(lane >= lo_lane, jnp.where(lane < lo_lane + EXPERTS_PER_GROUP, logits, neg), neg)
    m1 = jnp.max(el, axis=1, keepdims=True)
    i1 = jnp.min(jnp.where(el == m1, lane, big), axis=1, keepdims=True)
    el2 = jnp.where(lane == i1, neg, el)
    m2 = jnp.max(el2, axis=1, keepdims=True)
    i2 = jnp.min(jnp.where(el2 == m2, lane, big), axis=1, keepdims=True)
    p2 = jnp.exp(m2 - m1)
    den = 1.0 + p2
    w1 = g_w / den
    w2 = g_w * p2 / den

    sel1 = lane == i1
    sel2 = lane == i2
    onehot = jnp.where(sel1, 1.0, jnp.where(sel2, 1.0, 0.0))
    before = _dot(tri, onehot.astype(BF16))
    cnt = jnp.sum(onehot, axis=0, keepdims=True)
    groups = jnp.floor((cnt + (SUBLANES - 1)) * (1.0 / SUBLANES))
    start = SUBLANES * _dot(jnp.broadcast_to(groups, (SUBLANES, LANES)).astype(BF16), upper)[0:1]
    at = before + start
    l1 = jnp.sum(jnp.where(sel1, at, 0.0), axis=1, keepdims=True)
    l2 = jnp.sum(jnp.where(sel2, at, 0.0), axis=1, keepdims=True)

    route = jnp.where(lane == 0, i1, jnp.where(lane == 1, i2, jnp.where(lane == 2, w1, jnp.where(
        lane == 3, w2, jnp.where(lane == 4, l1, jnp.where(lane == 5, l2, 0.0))))))
    route_t = route.T
    rows = lax.broadcasted_iota(I32, (_local_rows(tm), tm), 0).astype(F32)
    perm = jnp.where(rows == route_t[4:5], 1.0, jnp.where(rows == route_t[5:6], 1.0, 0.0)).astype(BF16)
    return _dot(perm, x_hi), route, cnt


def _router_weights(w_group, b_group, w_router, b_router):
    d = w_group.shape[0]
    pad = LANES - N_EXPERTS - N_GROUPS
    w = jnp.concatenate([w_router, w_group, jnp.zeros((d, pad), F32)], axis=1)
    b = jnp.concatenate([b_router, b_group, jnp.zeros((pad,), F32)]).reshape(1, LANES)
    hi, lo = _split_bf16(w)
    return hi, lo, b


def _strict_lower(n):
    r = lax.broadcasted_iota(I32, (n, n), 0)
    c = lax.broadcasted_iota(I32, (n, n), 1)
    return (c < r).astype(BF16)


def _router_specs(tm, d):
    tok = lambda w: pl.BlockSpec((tm, w), lambda i: (i, 0))
    in_specs = [_full((1, d)), _full((d, LANES)), _full((d, LANES)), _full((1, LANES)), _full((tm, tm)),
                _full((LANES, LANES))]
    out_specs = [tok(d), pl.BlockSpec((_local_rows(tm), d), lambda i: (i, 0)), tok(LANES),
                 pl.BlockSpec((1, 1, LANES), lambda i: (i, 0, 0))]
    return in_specs, out_specs


def _router_out_shapes(t, tm, d):
    n = t // tm
    return [jax.ShapeDtypeStruct((t, d), F32), jax.ShapeDtypeStruct((n * _local_rows(tm), d), F32),
            jax.ShapeDtypeStruct((t, LANES), F32), jax.ShapeDtypeStruct((n, 1, LANES), F32)]


def _conv_kernel(x_ref, ng_ref, w1_ref, b1_ref, dw_ref, dwb_ref, lng_ref, lnb_ref, w2_ref, b2_ref,
                 fng_ref, wrh_ref, wrl_ref, br_ref, tri_ref, upper_ref,
                 h_ref, xl_ref, route_ref, cnt_ref,
                 ubuf, cbuf, *, tiles_per_seq, width):
    i = pl.program_id(0)
    tm, d = x_ref.shape
    n_strips = d // LANES

    @pl.when(i % tiles_per_seq == 0)
    def _():
        ubuf[:, 0:CONV_HALO, :] = jnp.zeros((n_strips, CONV_HALO, LANES), F32)

    @pl.when(i % tiles_per_seq != 0)
    def _():
        ubuf[:, 0:CONV_HALO, :] = ubuf[:, tm:tm + CONV_HALO, :]

    x = x_ref[...]
    hn = _rms(x, ng_ref[...])
    ag = _dot(hn.astype(BF16), w1_ref[...]) + b1_ref[...]
    u = ag[:, :d] * jax.nn.sigmoid(ag[:, d:])
    for c in range(n_strips):
        ubuf[c, CONV_HALO:CONV_HALO + tm, :] = u[:, c * LANES:(c + 1) * LANES]

    first = CONV_HALO - (width - 1)
    for c in range(n_strips):
        cols = slice(c * LANES, (c + 1) * LANES)

        for r in range(tm // CONV_ROWS):
            base = r * CONV_ROWS
            acc = jnp.zeros((CONV_ROWS, LANES), F32)
            for t in range(width):
                acc = acc + dw_ref[t:t + 1, cols] * ubuf[c, base + first + t:base + first + t + CONV_ROWS, :]
            cbuf[base:base + CONV_ROWS, cols] = acc

    v = cbuf[...] + dwb_ref[...]
    mu = jnp.mean(v, axis=-1, keepdims=True)
    vc = v - mu
    var = jnp.mean(vc * vc, axis=-1, keepdims=True)
    y = vc * lax.rsqrt(var + EPS) * lng_ref[...] + lnb_ref[...]
    y = y * jax.nn.sigmoid(y)
    h = x + _dot(y.astype(BF16), w2_ref[...]) + b2_ref[...]
    h_ref[...] = h

    xl, route, cnt = _route(h, fng_ref[...], wrh_ref[...], wrl_ref[...], br_ref[...], tri_ref[...], upper_ref[...])
    xl_ref[...] = xl
    route_ref[...] = route
    cnt_ref[0] = cnt


def _conv_layer(x, seq, ng, w1, b1, dw, dwb, lng, lnb, w2, b2, fng, wrh, wrl, br):
    t, d = x.shape
    tm = min(TOKEN_TILE, seq)
    width = dw.shape[0]
    dw_p = jnp.concatenate([dw, jnp.zeros((CONV_HALO - width, d), F32)], axis=0)
    row = lambda a: a.reshape(1, -1)
    r_in, r_out = _router_specs(tm, d)
    kern = functools.partial(_conv_kernel, tiles_per_seq=seq // tm, width=width)
    return pl.pallas_call(
        kern,
        grid=(t // tm,),
        in_specs=[pl.BlockSpec((tm, d), lambda i: (i, 0)), _full((1, d)), _full((d, 2 * d)), _full((1, 2 * d)),
                  _full((CONV_HALO, d)), _full((1, d)), _full((1, d)), _full((1, d)), _full((d, d)),
                  _full((1, d))] + r_in,
        out_specs=r_out,
        out_shape=_router_out_shapes(t, tm, d),
        scratch_shapes=[pltpu.VMEM((d // LANES, CONV_HALO + tm, LANES), F32), pltpu.VMEM((tm, d), F32)],
        compiler_params=_cparams(("arbitrary",)),
        name="conv_router",
    )(x, row(ng), w1.astype(BF16), row(b1), dw_p, row(dwb), row(lng), row(lnb), w2.astype(BF16), row(b2),
      row(fng), wrh, wrl, br, _strict_lower(tm), _strict_lower(LANES).T)


def _moe_plan(cnt, tm):
    n_tt = cnt.shape[0]
    c = cnt[:, 0, :N_EXPERTS].astype(I32)
    run = (c + SUBLANES - 1) // SUBLANES
    local = jnp.cumsum(run, axis=1) - run
    per_tile = MOE_TILE // SUBLANES
    n_exp = jnp.sum(run, axis=0)
    n_pad = ((n_exp + per_tile - 1) // per_tile) * per_tile
    e_end = jnp.cumsum(n_pad)
    e_base = e_end - n_pad
    dst = e_base[None, :] + jnp.cumsum(run, axis=0) - run
    src = jnp.arange(n_tt, dtype=I32)[:, None] * (_local_rows(tm) // SUBLANES) + local
    max_groups = (2 * n_tt * tm) // SUBLANES + n_tt * N_EXPERTS + N_EXPERTS * (per_tile - 1)
    n_tiles = -(-max_groups // per_tile)
    tile_start = jnp.minimum(jnp.arange(n_tiles, dtype=I32) * per_tile, e_end[-1] - per_tile)
    tile_expert = jnp.sum((tile_start[:, None] >= e_end[None, :]).astype(I32), axis=1)
    g = jnp.arange(n_tiles * per_tile, dtype=I32).reshape(n_tiles, per_tile)
    rs = jnp.take(dst, tile_expert, axis=1)[:, :, None]
    re = rs + jnp.take(run, tile_expert, axis=1)[:, :, None]
    shift = jnp.take(src - dst, tile_expert, axis=1)[:, :, None]
    in_x = jnp.logical_and(rs <= g[None], g[None] < re)
    fetch_x = jnp.sum(jnp.where(in_x, shift + g[None], 0), axis=0)
    lg = _local_rows(tm) // SUBLANES
    p = jnp.arange(lg, dtype=I32)[None, :, None]
    lo = local[:, None, :]
    in_y = jnp.logical_and(lo <= p, p < lo + run[:, None, :])
    fetch_y = jnp.sum(jnp.where(in_y, (dst - local)[:, None, :] + p, 0), axis=2)
    return dict(fetch_x=fetch_x.reshape(n_tiles, 1, per_tile), fetch_y=fetch_y.reshape(n_tt, 1, lg),
                tile_expert=tile_expert, n_used=(e_end[-1] // per_tile).reshape(1).astype(I32), n_tiles=n_tiles)


def _group(ref, g):
    return ref.at[pl.ds(pl.multiple_of(g * SUBLANES, SUBLANES), SUBLANES)]


def _moe_kernel(te_ref, nu_ref, fx_ref, fx_next_ref, xl_hbm, wgu_ref, wd_ref,
                y_ref, xbuf, sems, wgu_bf, wd_bf):
    j = pl.program_id(0)
    n_used = nu_ref[0]
    de = wd_ref.shape[2]
    slot = j % 2

    n_tiles = pl.num_programs(0)
    per_tile = MOE_TILE // SUBLANES

    def copy(g_src, g_dst, s):
        return pltpu.make_async_copy(_group(xl_hbm, g_src), _group(xbuf.at[s], g_dst), sems.at[s])

    def issue(tab_ref, s):
        for k in range(per_tile):
            copy(tab_ref[0, 0, k], k, s).start()

    def wait(s):
        pltpu.make_async_copy(xl_hbm.at[pl.ds(0, MOE_TILE)], xbuf.at[s], sems.at[s]).wait()

    @pl.when(j == 0)
    def _():
        issue(fx_ref, 0)

    @pl.when(j < n_used)
    def _():
        e = te_ref[j]
        prev = te_ref[jnp.maximum(j - 1, 0)]

        @pl.when(jnp.logical_or(j == 0, e != prev))
        def _():
            wgu_bf[...] = wgu_ref[0, 0].astype(BF16)
            wd_bf[...] = wd_ref[0, 0].astype(BF16)

        issue(fx_next_ref, 1 - slot)
        wait(slot)
        gu = _dot(xbuf[slot].astype(BF16), wgu_bf[...])
        a = gu[:, :de]
        b = gu[:, de:]
        mid = (a * jax.nn.sigmoid(a) * b).astype(BF16)
        y_ref[...] = _dot(mid, wd_bf[...])

    @pl.when(j == n_used)
    def _():
        wait(slot)

    @pl.when(jnp.logical_and(j == n_tiles - 1, j < n_used))
    def _():
        wait(1 - slot)

    @pl.when(j >= n_used)
    def _():
        y_ref[...] = jnp.zeros_like(y_ref)


def _moe_experts(xl, plan, layer, w_gate_up, w_down):
    d = xl.shape[1]
    tm = MOE_TILE
    n_tiles = plan["n_tiles"]
    de = w_down.shape[2]
    per_tile = tm // SUBLANES
    grid_spec = pltpu.PrefetchScalarGridSpec(
        num_scalar_prefetch=2,
        grid=(n_tiles,),
        in_specs=[
            pl.BlockSpec((1, 1, per_tile), lambda j, *_: (j, 0, 0), memory_space=pltpu.SMEM),
            pl.BlockSpec((1, 1, per_tile), lambda j, *_: (jnp.minimum(j + 1, n_tiles - 1), 0, 0),
                         memory_space=pltpu.SMEM),
            pl.BlockSpec(memory_space=pl.ANY),
            pl.BlockSpec((1, 1, d, 2 * de), lambda j, te, *_: (layer, te[j], 0, 0)),
            pl.BlockSpec((1, 1, de, d), lambda j, te, *_: (layer, te[j], 0, 0)),
        ],
        out_specs=pl.BlockSpec((tm, d), lambda j, *_: (j, 0)),
        scratch_shapes=[pltpu.VMEM((2, tm, d), F32), pltpu.SemaphoreType.DMA((2,)),
                        pltpu.VMEM((d, 2 * de), BF16), pltpu.VMEM((de, d), BF16)],
    )
    return pl.pallas_call(
        _moe_kernel,
        grid_spec=grid_spec,
        out_shape=jax.ShapeDtypeStruct((n_tiles * tm, d), F32),
        compiler_params=_cparams(("arbitrary",)),
        name="moe_experts",
    )(plan["tile_expert"], plan["n_used"], plan["fetch_x"], plan["fetch_x"], xl, w_gate_up, w_down)


def _combine_kernel(fy_ref, fy_next_ref, h_ref, route_ref, y_hbm, o_ref, ybuf, sems):
    i = pl.program_id(0)
    n = pl.num_programs(0)
    tm = o_ref.shape[0]
    rows = ybuf.shape[1]
    slot = i % 2

    def copy(g_src, g_dst, s):
        return pltpu.make_async_copy(_group(y_hbm, g_src), _group(ybuf.at[s], g_dst), sems.at[s])

    def issue(tab_ref, s):
        for k in range(rows // SUBLANES):
            copy(tab_ref[0, 0, k], k, s).start()

    def wait(s):
        pltpu.make_async_copy(y_hbm.at[pl.ds(0, rows)], ybuf.at[s], sems.at[s]).wait()

    @pl.when(i == 0)
    def _():
        issue(fy_ref, 0)

    issue(fy_next_ref, 1 - slot)
    wait(slot)

    route = route_ref[...]
    route_t = route.T
    row = lax.broadcasted_iota(I32, (rows, tm), 0).astype(F32)
    row_gate = jnp.sum(jnp.where(row == route_t[4:5], route_t[2:3],
                                 jnp.where(row == route_t[5:6], route_t[3:4], 0.0)), axis=1, keepdims=True)
    y = (ybuf[slot] * row_gate).astype(BF16)
    col = lax.broadcasted_iota(I32, (tm, rows), 1).astype(F32)
    pick = jnp.where(col == route[:, 4:5], 1.0, jnp.where(col == route[:, 5:6], 1.0, 0.0)).astype(BF16)
    o_ref[...] = h_ref[...] + _dot(pick, y)

    @pl.when(i == n - 1)
    def _():
        wait(1 - slot)


def _moe_combine(h, route, y, plan, tm):
    t, d = h.shape
    n = t // tm
    lg = _local_rows(tm) // SUBLANES
    return pl.pallas_call(
        _combine_kernel,
        grid=(n,),
        in_specs=[pl.BlockSpec((1, 1, lg), lambda i: (i, 0, 0), memory_space=pltpu.SMEM),
                  pl.BlockSpec((1, 1, lg), lambda i: (jnp.minimum(i + 1, n - 1), 0, 0), memory_space=pltpu.SMEM),
                  pl.BlockSpec((tm, d), lambda i: (i, 0)), pl.BlockSpec((tm, LANES), lambda i: (i, 0)),
                  pl.BlockSpec(memory_space=pl.ANY)],
        out_specs=pl.BlockSpec((tm, d), lambda i: (i, 0)),
        out_shape=jax.ShapeDtypeStruct((t, d), F32),
        scratch_shapes=[pltpu.VMEM((2, _local_rows(tm), d), F32), pltpu.SemaphoreType.DMA((2,))],
        compiler_params=_cparams(("arbitrary",)),
        name="moe_combine",
    )(plan["fetch_y"], plan["fetch_y"], h, route, y)


def _moe_layer(h, xl, route, cnt, tm, layer, w_gate_up, w_down):
    plan = _moe_plan(cnt, tm)
    y = _moe_experts(xl, plan, layer, w_gate_up, w_down)
    return _moe_combine(h, route, y, plan, tm)


def _rope_block(x, cos_tab, sin_tab):
    lane = lax.broadcasted_iota(I32, x.shape, 1)
    half = QK_ROPE_DIM // 2
    up = pltpu.roll(x, LANES - half, 1)
    down = pltpu.roll(x, half, 1)
    partner = jnp.where(lane < half, up, jnp.where(lane < QK_ROPE_DIM, down, 0.0))
    return x * cos_tab + partner * sin_tab


def _proj_kernel(h_ref, cos_ref, sin_ref, kvg_ref, wdkv_ref, kvlg_ref, wkr_ref, wuk_ref, wuv_ref, kng_ref,
                 qng_ref, wdq_ref, qlg_ref, wuq_ref, qg_ref, hsum_ref, hexp_ref,
                 q_ref, k_ref, v_ref):
    h = h_ref[...]
    cos_tab = cos_ref[...]
    sin_tab = sin_ref[...]
    hsum = hsum_ref[...]
    hexp = hexp_ref[...]

    def head_sums(sq):
        return _dot(sq.astype(BF16), hsum)

    def head_spread(val):
        hi, lo = _split_bf16(val)
        return _dot(hi, hexp) + _dot(lo, hexp)

    hn = _rms(h, kvg_ref[...]).astype(BF16)
    c_kv = _rms(_dot(hn, wdkv_ref[...]), kvlg_ref[...]).astype(BF16)
    kn = _dot(c_kv, wuk_ref[...])
    vv = _dot(c_kv, wuv_ref[...])
    kr = _dot(hn, wkr_ref[...])
    kng = kng_ref[...]
    ss = head_sums(kn * kn) + jnp.sum(kr * kr, axis=-1, keepdims=True)
    inv = lax.rsqrt(ss * (1.0 / QK_DIM) + EPS)
    spread = head_spread(inv)
    kr_rot = _rope_block(kr * kng[:, LANES:], cos_tab, sin_tab)
    for hd in range(N_HEADS):
        cols = slice(hd * LANES, (hd + 1) * LANES)
        f = spread[:, cols]
        k_ref[hd, :, 0:LANES] = (kn[:, cols] * f * kng[:, :LANES]).astype(BF16)
        k_ref[hd, :, LANES:HEAD_PAD] = (kr_rot * f).astype(BF16)
        v_ref[hd] = vv[:, cols].astype(BF16)

    hq = _rms(h, qng_ref[...]).astype(BF16)
    c_q = _rms(_dot(hq, wdq_ref[...]), qlg_ref[...]).astype(BF16)
    qn = _dot(c_q, wuq_ref[:, 0:N_HEADS * LANES])
    qr = _dot(c_q, wuq_ref[:, N_HEADS * LANES:2 * N_HEADS * LANES])
    qg = qg_ref[...]
    ssq = head_sums(qn * qn) + head_sums(qr * qr)
    invq = lax.rsqrt(ssq * (1.0 / QK_DIM) + EPS) * SOFTMAX_SCALE
    spreadq = head_spread(invq)
    for hd in range(N_HEADS):
        cols = slice(hd * LANES, (hd + 1) * LANES)
        f = spreadq[:, cols]
        q_ref[hd, :, 0:LANES] = (qn[:, cols] * f * qg[:, :LANES]).astype(BF16)
        rot = _rope_block(qr[:, cols] * qg[:, LANES:], cos_tab, sin_tab)
        q_ref[hd, :, LANES:HEAD_PAD] = (rot * f).astype(BF16)


def _head_major(w, per_head, lo, hi):
    k = w.shape[0]
    w3 = w.reshape(k, N_HEADS, per_head)[:, :, lo:hi]
    w3 = jnp.pad(w3, ((0, 0), (0, 0), (0, LANES - (hi - lo))))
    return w3.reshape(k, N_HEADS * LANES)


def _mla_project(h, seq, cos_tab, sin_tab, kv_norm_g, w_dkv, kv_latent_g, w_kr, w_ukv, k_norm_g,
                 b_norm_g, w_dq, q_latent_g, w_uq, q_norm_g):
    t, d = h.shape
    tm = min(TOKEN_TILE, seq)
    kv_rank = w_dkv.shape[1]
    q_rank = w_dq.shape[1]
    hw = N_HEADS * LANES
    row = lambda a: a.reshape(1, -1)
    wkr = jnp.pad(w_kr, ((0, 0), (0, LANES - QK_ROPE_DIM))).astype(BF16)
    wuk = _head_major(w_ukv, QK_NOPE_DIM + V_HEAD_DIM, 0, QK_NOPE_DIM).astype(BF16)
    wuv = _head_major(w_ukv, QK_NOPE_DIM + V_HEAD_DIM, QK_NOPE_DIM, QK_NOPE_DIM + V_HEAD_DIM).astype(BF16)
    wuq = jnp.concatenate([_head_major(w_uq, QK_DIM, 0, QK_NOPE_DIM),
                           _head_major(w_uq, QK_DIM, QK_NOPE_DIM, QK_DIM)], axis=1).astype(BF16)
    pad_gain = lambda g: jnp.pad(g, (0, HEAD_PAD - QK_DIM)).reshape(1, HEAD_PAD)
    head_of = jnp.arange(hw, dtype=I32) // LANES
    hsum = (head_of[:, None] == jnp.arange(LANES, dtype=I32)[None, :]).astype(BF16)
    hexp = hsum.T
    tok = lambda w: pl.BlockSpec((tm, w), lambda i: (i, 0))
    heads = lambda w: pl.BlockSpec((N_HEADS, tm, w), lambda i: (0, i, 0))
    return pl.pallas_call(
        _proj_kernel,
        grid=(t // tm,),
        in_specs=[tok(d), tok(LANES), tok(LANES), _full((1, d)), _full((d, kv_rank)), _full((1, kv_rank)),
                  _full((d, LANES)), _full((kv_rank, hw)), _full((kv_rank, hw)), _full((1, HEAD_PAD)),
                  _full((1, d)), _full((d, q_rank)), _full((1, q_rank)), _full((q_rank, 2 * hw)),
                  _full((1, HEAD_PAD)), _full((hw, LANES)), _full((LANES, hw))],
        out_specs=[heads(HEAD_PAD), heads(HEAD_PAD), heads(LANES)],
        out_shape=[jax.ShapeDtypeStruct((N_HEADS, t, HEAD_PAD), BF16),
                   jax.ShapeDtypeStruct((N_HEADS, t, HEAD_PAD), BF16),
                   jax.ShapeDtypeStruct((N_HEADS, t, LANES), BF16)],
        compiler_params=_cparams(("arbitrary",)),
        name="mla_project",
    )(h, cos_tab, sin_tab, row(kv_norm_g), w_dkv.astype(BF16), row(kv_latent_g), wkr, wuk, wuv,
      pad_gain(k_norm_g), row(b_norm_g), w_dq.astype(BF16), row(q_latent_g), wuq, pad_gain(q_norm_g),
      hsum, hexp)


def _attn_kernel(qi_ref, ki_ref, q_ref, k_ref, v_ref, o_ref, m_ref, acc_ref, vext_ref):
    s_idx = pl.program_id(1)
    qi = qi_ref[s_idx]
    ki = ki_ref[s_idx]
    tq = q_ref.shape[1]
    tk = k_ref.shape[1]
    dv = V_HEAD_DIM
    k_lo = ki * tk - qi * tq

    @pl.when(ki == 0)
    def _():
        m_ref[...] = jnp.full_like(m_ref, -jnp.inf)
        acc_ref[...] = jnp.zeros_like(acc_ref)
        vext_ref[:, :, dv:] = jnp.ones((N_HEADS, tk, LANES), BF16)

    vext_ref[:, :, :dv] = v_ref[...]

    def step(masked):
        if masked:
            qc = lax.broadcasted_iota(I32, (tq, tk), 0) // CHUNK
            kc = (lax.broadcasted_iota(I32, (tq, tk), 1) + k_lo) // CHUNK
            allowed = kc <= qc
        for hd in range(N_HEADS):
            s = lax.dot_general(q_ref[hd], k_ref[hd], (((1,), (1,)), ((), ())), preferred_element_type=F32)
            if masked:
                s = jnp.where(allowed, s, MASK_VALUE)
            m_prev = m_ref[hd]
            m_new = jnp.maximum(m_prev, jnp.max(s, axis=-1, keepdims=True))
            alpha = jnp.exp(m_prev - m_new)
            p = jnp.exp(s - jnp.concatenate([m_new] * (tk // LANES), axis=1))
            pv = _dot(p.astype(BF16), vext_ref[hd])
            acc_ref[hd] = jnp.concatenate([alpha, alpha], axis=1) * acc_ref[hd] + pv
            m_ref[hd] = m_new

    @pl.when(k_lo < 0)
    def _():
        step(False)

    @pl.when(k_lo >= 0)
    def _():
        step(True)

    @pl.when(k_lo + tk == tq)
    def _():
        for hd in range(N_HEADS):
            acc = acc_ref[hd]
            o_ref[:, hd * dv:(hd + 1) * dv] = (acc[:, :dv] / acc[:, dv:]).astype(o_ref.dtype)


def _attention(q, k, v, batch, seq):
    t = q.shape[1]
    tk = min(ATTN_KEY_TILE, seq)
    tq = min(ATTN_QUERY_TILE, seq)
    nq = seq // tq
    nk = seq // tk
    per_q = tq // tk
    pairs = [(a, b) for a in range(nq) for b in range((a + 1) * per_q)]
    qi_tab = jnp.array([p[0] for p in pairs], I32)
    ki_tab = jnp.array([p[1] for p in pairs], I32)
    grid_spec = pltpu.PrefetchScalarGridSpec(
        num_scalar_prefetch=2,
        grid=(batch, len(pairs)),
        in_specs=[
            pl.BlockSpec((N_HEADS, tq, HEAD_PAD), lambda b, s, qi, ki: (0, b * nq + qi[s], 0)),
            pl.BlockSpec((N_HEADS, tk, HEAD_PAD), lambda b, s, qi, ki: (0, b * nk + ki[s], 0)),
            pl.BlockSpec((N_HEADS, tk, V_HEAD_DIM), lambda b, s, qi, ki: (0, b * nk + ki[s], 0)),
        ],
        out_specs=pl.BlockSpec((tq, N_HEADS * V_HEAD_DIM), lambda b, s, qi, ki: (b * nq + qi[s], 0)),
        scratch_shapes=[pltpu.VMEM((N_HEADS, tq, LANES), F32),
                        pltpu.VMEM((N_HEADS, tq, V_HEAD_DIM + LANES), F32),
                        pltpu.VMEM((N_HEADS, tk, V_HEAD_DIM + LANES), BF16)],
    )
    return pl.pallas_call(
        _attn_kernel,
        grid_spec=grid_spec,
        out_shape=jax.ShapeDtypeStruct((t, N_HEADS * V_HEAD_DIM), BF16),
        compiler_params=_cparams(("arbitrary", "arbitrary")),
        name="attention",
    )(qi_tab, ki_tab, q, k, v)


def _oproj_kernel(o_ref, h_ref, wo_ref, fng_ref, wrh_ref, wrl_ref, br_ref, tri_ref, upper_ref,
                  h_out_ref, xl_ref, route_ref, cnt_ref):
    h = h_ref[...] + _dot(o_ref[...], wo_ref[...])
    h_out_ref[...] = h
    xl, route, cnt = _route(h, fng_ref[...], wrh_ref[...], wrl_ref[...], br_ref[...], tri_ref[...], upper_ref[...])
    xl_ref[...] = xl
    route_ref[...] = route
    cnt_ref[0] = cnt


def _oproj_layer(o, h, seq, w_o, fng, wrh, wrl, br):
    t, d = h.shape
    tm = min(TOKEN_TILE, seq)
    r_in, r_out = _router_specs(tm, d)
    tok = lambda w: pl.BlockSpec((tm, w), lambda i: (i, 0))
    return pl.pallas_call(
        _oproj_kernel,
        grid=(t // tm,),
        in_specs=[tok(o.shape[1]), tok(d), _full(w_o.shape)] + r_in,
        out_specs=r_out,
        out_shape=_router_out_shapes(t, tm, d),
        compiler_params=_cparams(("arbitrary",)),
        name="oproj_router",
    )(o, h, w_o.astype(BF16), fng.reshape(1, d), wrh, wrl, br, _strict_lower(tm), _strict_lower(LANES).T)


def kernel(x, positions, a_norm_g, a_pw1_w, a_pw1_b, a_dw_w, a_dw_b, a_ln_g, a_ln_b, a_pw2_w, a_pw2_b, kv_norm_g, w_dkv, kv_latent_g, w_kr, w_ukv, k_norm_g, b_norm_g, w_dq, q_latent_g, w_uq, q_norm_g, w_o, ffn_norm_g, w_group, b_group, w_router, b_router, w_gate_up, w_down):
    batch, seq, d = x.shape
    t = batch * seq
    tm = min(TOKEN_TILE, seq)
    assert a_norm_g.shape[0] == 1 and b_norm_g.shape[0] == 1 and ffn_norm_g.shape[0] == 2
    cos_tab, sin_tab = _rope_tables(positions)

    wrh, wrl, br = _router_weights(w_group[0], b_group[0], w_router[0], b_router[0])
    h, xl, route, cnt = _conv_layer(x.reshape(t, d), seq, a_norm_g[0], a_pw1_w[0], a_pw1_b[0], a_dw_w[0],
                                    a_dw_b[0], a_ln_g[0], a_ln_b[0], a_pw2_w[0], a_pw2_b[0],
                                    ffn_norm_g[0], wrh, wrl, br)
    h = _moe_layer(h, xl, route, cnt, tm, 0, w_gate_up, w_down)

    q, k, v = _mla_project(h, seq, cos_tab, sin_tab, kv_norm_g, w_dkv, kv_latent_g, w_kr, w_ukv, k_norm_g,
                           b_norm_g[0], w_dq[0], q_latent_g[0], w_uq[0], q_norm_g[0])
    o = _attention(q, k, v, batch, seq)
    wrh, wrl, br = _router_weights(w_group[1], b_group[1], w_router[1], b_router[1])
    h, xl, route, cnt = _oproj_layer(o, h, seq, w_o[0], ffn_norm_g[1], wrh, wrl, br)
    h = _moe_layer(h, xl, route, cnt, tm, 1, w_gate_up, w_down)
    return h.reshape(batch, seq, d)
```

```python
import functools

import jax
import jax.numpy as jnp
from jax import lax
from jax.experimental import pallas as pl
from jax.experimental.pallas import tpu as pltpu

F32 = jnp.float32
BF16 = jnp.bfloat16
I32 = jnp.int32

EPS = 1e-6
CHUNK = 64
N_HEADS = 8
QK_NOPE_DIM = 128
QK_ROPE_DIM = 64
QK_DIM = QK_NOPE_DIM + QK_ROPE_DIM
V_HEAD_DIM = 128
ROPE_THETA = 10000.0
SOFTMAX_SCALE = QK_DIM ** -0.5
MASK_VALUE = -1e30
N_GROUPS = 4
EXPERTS_PER_GROUP = 8
N_EXPERTS = N_GROUPS * EXPERTS_PER_GROUP

LANES = 128
SUBLANES = 8
HEAD_PAD = 256
CONV_HALO = 32
CONV_ROWS = 64
TOKEN_TILE = 512
MOE_TILE = 512
ATTN_QUERY_TILE = 1024
ATTN_KEY_TILE = 512
VMEM_LIMIT = 56 * 1024 * 1024


def _cparams(sem):
    return pltpu.CompilerParams(dimension_semantics=sem, vmem_limit_bytes=VMEM_LIMIT)


def _rms(x, g):
    return x * lax.rsqrt(jnp.mean(x * x, axis=-1, keepdims=True) + EPS) * g


def _dot(a, b):
    return jnp.dot(a, b, preferred_element_type=F32)


def _split_bf16(x):
    hi = x.astype(BF16)
    lo = (x - hi.astype(F32)).astype(BF16)
    return hi, lo


def _full(shape):
    return pl.BlockSpec(shape, lambda *_: (0,) * len(shape))


def _local_rows(tm):
    return 2 * tm + N_EXPERTS * SUBLANES


def _rope_kernel(pos_ref, freq_ref, cos_ref, sin_ref):
    ang = pos_ref[...] * freq_ref[...]
    cos_ref[...] = jnp.cos(ang)
    sin_ref[...] = jnp.sin(ang)


def _rope_tables(positions):
    t = positions.size
    half = QK_ROPE_DIM // 2
    per_row = LANES // half
    inv_freq = ROPE_THETA ** (-jnp.arange(0, QK_ROPE_DIM, 2, dtype=F32) / QK_ROPE_DIM)
    pos = jnp.repeat(positions.reshape(t).astype(F32), half).reshape(t // per_row, LANES)
    freq = jnp.tile(inv_freq, per_row).reshape(1, LANES)
    rows = t // per_row
    blk = min(rows, 1024)
    cos, sin = pl.pallas_call(
        _rope_kernel,
        grid=(rows // blk,),
        in_specs=[pl.BlockSpec((blk, LANES), lambda i: (i, 0)), _full((1, LANES))],
        out_specs=[pl.BlockSpec((blk, LANES), lambda i: (i, 0))] * 2,
        out_shape=[jax.ShapeDtypeStruct((rows, LANES), F32)] * 2,
        compiler_params=_cparams(("arbitrary",)),
        name="rope_tables",
    )(pos, freq)
    cos = cos.reshape(t, half)
    sin = sin.reshape(t, half)
    zeros = jnp.zeros((t, LANES - QK_ROPE_DIM), F32)
    cos_tab = jnp.concatenate([cos, cos, zeros], axis=1)
    sin_tab = jnp.concatenate([-sin, sin, zeros], axis=1)
    return cos_tab, sin_tab


def _route(h, fng, wr_hi, wr_lo, br, tri, upper):
    tm = h.shape[0]
    xt = _rms(h, fng)
    x_hi, x_lo = _split_bf16(xt)
    logits = _dot(x_hi, wr_hi) + _dot(x_lo, wr_hi) + _dot(x_hi, wr_lo) + br
    lane = lax.broadcasted_iota(I32, (tm, LANES), 1).astype(F32)
    neg = -jnp.inf
    big = float(LANES)

    gl = jnp.where(lane >= N_EXPERTS, jnp.where(lane < N_EXPERTS + N_GROUPS, logits, neg), neg)
    gmax = jnp.max(gl, axis=1, keepdims=True)
    gidx = jnp.min(jnp.where(gl == gmax, lane, big), axis=1, keepdims=True) - N_EXPERTS
    g_w = 1.0 / jnp.sum(jnp.exp(gl - gmax), axis=1, keepdims=True)

    lo_lane = gidx * EXPERTS_PER_GROUP
    el = jnp.where(lane >= lo_lane, jnp.where(lane < lo_lane + EXPERTS_PER_GROUP, logits, neg), neg)
    m1 = jnp.max(el, axis=1, keepdims=True)
    i1 = jnp.min(jnp.where(el == m1, lane, big), axis=1, keepdims=True)
    el2 = jnp.where(lane == i1, neg, el)
    m2 = jnp.max(el2, axis=1, keepdims=True)
    i2 = jnp.min(jnp.where(el2 == m2, lane, big), axis=1, keepdims=True)
    p2 = jnp.exp(m2 - m1)
    den = 1.0 + p2
    w1 = g_w / den
    w2 = g_w * p2 / den

    sel1 = lane == i1
    sel2 = lane == i2
    onehot = jnp.where(sel1, 1.0, jnp.where(sel2, 1.0, 0.0))
    before = _dot(tri, onehot.astype(BF16))
    cnt = jnp.sum(onehot, axis=0, keepdims=True)
    groups = jnp.floor((cnt + (SUBLANES - 1)) * (1.0 / SUBLANES))
    start = SUBLANES * _dot(jnp.broadcast_to(groups, (SUBLANES, LANES)).astype(BF16), upper)[0:1]
    at = before + start
    l1 = jnp.sum(jnp.where(sel1, at, 0.0), axis=1, keepdims=True)
    l2 = jnp.sum(jnp.where(sel2, at, 0.0), axis=1, keepdims=True)

    route = jnp.where(lane == 0, i1, jnp.where(lane == 1, i2, jnp.where(lane == 2, w1, jnp.where(
        lane == 3, w2, jnp.where(lane == 4, l1, jnp.where(lane == 5, l2, 0.0))))))
    route_t = route.T
    rows = lax.broadcasted_iota(I32, (_local_rows(tm), tm), 0).astype(F32)
    perm = jnp.where(rows == route_t[4:5], 1.0, jnp.where(rows == route_t[5:6], 1.0, 0.0)).astype(BF16)
    return _dot(perm, x_hi), route, cnt


def _router_weights(w_group, b_group, w_router, b_router):
    d = w_group.shape[0]
    pad = LANES - N_EXPERTS - N_GROUPS
    w = jnp.concatenate([w_router, w_group, jnp.zeros((d, pad), F32)], axis=1)
    b = jnp.concatenate([b_router, b_group, jnp.zeros((pad,), F32)]).reshape(1, LANES)
    hi, lo = _split_bf16(w)
    return hi, lo, b


def _strict_lower(n):
    r = lax.broadcasted_iota(I32, (n, n), 0)
    c = lax.broadcasted_iota(I32, (n, n), 1)
    return (c < r).astype(BF16)


def _router_specs(tm, d):
    tok = lambda w: pl.BlockSpec((tm, w), lambda i: (i, 0))
    in_specs = [_full((1, d)), _full((d, LANES)), _full((d, LANES)), _full((1, LANES)), _full((tm, tm)),
                _full((LANES, LANES))]
    out_specs = [tok(d), pl.BlockSpec((_local_rows(tm), d), lambda i: (i, 0)), tok(LANES),
                 pl.BlockSpec((1, 1, LANES), lambda i: (i, 0, 0))]
    return in_specs, out_specs


def _router_out_shapes(t, tm, d):
    n = t // tm
    return [jax.ShapeDtypeStruct((t, d), F32), jax.ShapeDtypeStruct((n * _local_rows(tm), d), F32),
            jax.ShapeDtypeStruct((t, LANES), F32), jax.ShapeDtypeStruct((n, 1, LANES), F32)]


def _conv_kernel(x_ref, ng_ref, w1_ref, b1_ref, dw_ref, dwb_ref, lng_ref, lnb_ref, w2_ref, b2_ref,
                 fng_ref, wrh_ref, wrl_ref, br_ref, tri_ref, upper_ref,
                 h_ref, xl_ref, route_ref, cnt_ref,
                 ubuf, cbuf, *, tiles_per_seq, width):
    i = pl.program_id(0)
    tm, d = x_ref.shape
    n_strips = d // LANES

    @pl.when(i % tiles_per_seq == 0)
    def _():
        ubuf[:, 0:CONV_HALO, :] = jnp.zeros((n_strips, CONV_HALO, LANES), F32)

    @pl.when(i % tiles_per_seq != 0)
    def _():
        ubuf[:, 0:CONV_HALO, :] = ubuf[:, tm:tm + CONV_HALO, :]

    x = x_ref[...]
    hn = _rms(x, ng_ref[...])
    ag = _dot(hn.astype(BF16), w1_ref[...]) + b1_ref[...]
    u = ag[:, :d] * jax.nn.sigmoid(ag[:, d:])
    for c in range(n_strips):
        ubuf[c, CONV_HALO:CONV_HALO + tm, :] = u[:, c * LANES:(c + 1) * LANES]

    first = CONV_HALO - (width - 1)
    for c in range(n_strips):
        cols = slice(c * LANES, (c + 1) * LANES)

        for r in range(tm // CONV_ROWS):
            base = r * CONV_ROWS
            acc = jnp.zeros((CONV_ROWS, LANES), F32)
            for t in range(width):
                acc = acc + dw_ref[t:t + 1, cols] * ubuf[c, base + first + t:base + first + t + CONV_ROWS, :]
            cbuf[base:base + CONV_ROWS, cols] = acc

    v = cbuf[...] + dwb_ref[...]
    mu = jnp.mean(v, axis=-1, keepdims=True)
    vc = v - mu
    var = jnp.mean(vc * vc, axis=-1, keepdims=True)
    y = vc * lax.rsqrt(var + EPS) * lng_ref[...] + lnb_ref[...]
    y = y * jax.nn.sigmoid(y)
    h = x + _dot(y.astype(BF16), w2_ref[...]) + b2_ref[...]
    h_ref[...] = h

    xl, route, cnt = _route(h, fng_ref[...], wrh_ref[...], wrl_ref[...], br_ref[...], tri_ref[...], upper_ref[...])
    xl_ref[...] = xl
    route_ref[...] = route
    cnt_ref[0] = cnt


def _conv_layer(x, seq, ng, w1, b1, dw, dwb, lng, lnb, w2, b2, fng, wrh, wrl, br):
    t, d = x.shape
    tm = min(TOKEN_TILE, seq)
    width = dw.shape[0]
    dw_p = jnp.concatenate([dw, jnp.zeros((CONV_HALO - width, d), F32)], axis=0)
    row = lambda a: a.reshape(1, -1)
    r_in, r_out = _router_specs(tm, d)
    kern = functools.partial(_conv_kernel, tiles_per_seq=seq // tm, width=width)
    return pl.pallas_call(
        kern,
        grid=(t // tm,),
        in_specs=[pl.BlockSpec((tm, d), lambda i: (i, 0)), _full((1, d)), _full((d, 2 * d)), _full((1, 2 * d)),
                  _full((CONV_HALO, d)), _full((1, d)), _full((1, d)), _full((1, d)), _full((d, d)),
                  _full((1, d))] + r_in,
        out_specs=r_out,
        out_shape=_router_out_shapes(t, tm, d),
        scratch_shapes=[pltpu.VMEM((d // LANES, CONV_HALO + tm, LANES), F32), pltpu.VMEM((tm, d), F32)],
        compiler_params=_cparams(("arbitrary",)),
        name="conv_router",
    )(x, row(ng), w1.astype(BF16), row(b1), dw_p, row(dwb), row(lng), row(lnb), w2.astype(BF16), row(b2),
      row(fng), wrh, wrl, br, _strict_lower(tm), _strict_lower(LANES).T)


def _moe_plan(cnt, tm):
    n_tt = cnt.shape[0]
    c = cnt[:, 0, :N_EXPERTS].astype(I32)
    run = (c + SUBLANES - 1) // SUBLANES
    local = jnp.cumsum(run, axis=1) - run
    per_tile = MOE_TILE // SUBLANES
    n_exp = jnp.sum(run, axis=0)
    n_pad = ((n_exp + per_tile - 1) // per_tile) * per_tile
    e_end = jnp.cumsum(n_pad)
    e_base = e_end - n_pad
    dst = e_base[None, :] + jnp.cumsum(run, axis=0) - run
    src = jnp.arange(n_tt, dtype=I32)[:, None] * (_local_rows(tm) // SUBLANES) + local
    max_groups = (2 * n_tt * tm) // SUBLANES + n_tt * N_EXPERTS + N_EXPERTS * (per_tile - 1)
    n_tiles = -(-max_groups // per_tile)
    tile_start = jnp.minimum(jnp.arange(n_tiles, dtype=I32) * per_tile, e_end[-1] - per_tile)
    tile_expert = jnp.sum((tile_start[:, None] >= e_end[None, :]).astype(I32), axis=1)
    g = jnp.arange(n_tiles * per_tile, dtype=I32).reshape(n_tiles, per_tile)
    rs = jnp.take(dst, tile_expert, axis=1)[:, :, None]
    re = rs + jnp.take(run, tile_expert, axis=1)[:, :, None]
    shift = jnp.take(src - dst, tile_expert, axis=1)[:, :, None]
    fetch_x = jnp.sum(jnp.where(jnp.logical_and(rs <= g[None], g[None] < re), shift, 0), axis=0) + g
    n_x = jnp.clip(jnp.take(e_base + n_exp, tile_expert) - g[:, 0], 0, per_tile)
    lg = _local_rows(tm) // SUBLANES
    p = jnp.arange(lg, dtype=I32)[None, :, None]
    lo = local[:, None, :]
    inside = jnp.logical_and(lo <= p, p < lo + run[:, None, :])
    fetch_y = jnp.sum(jnp.where(inside, (dst - local)[:, None, :], 0), axis=2) + p[:, :, 0]
    return dict(fetch_x=fetch_x.reshape(n_tiles, 1, per_tile), n_x=n_x.astype(I32),
                fetch_y=fetch_y.reshape(n_tt, 1, lg), n_y=jnp.sum(run, axis=1).astype(I32),
                tile_expert=tile_expert, n_used=(e_end[-1] // per_tile).reshape(1).astype(I32), n_tiles=n_tiles)


def _group(ref, g):
    return ref.at[pl.ds(pl.multiple_of(g * SUBLANES, SUBLANES), SUBLANES)]


def _wait_groups(src_hbm, dst, sem, count, max_count):
    bit = 1
    while bit <= max_count:
        @pl.when((count & bit) != 0)
        def _(bit=bit):
            n = bit * SUBLANES
            pltpu.make_async_copy(src_hbm.at[pl.ds(0, n)], dst.at[pl.ds(0, n)], sem).wait()
        bit *= 2


def _moe_kernel(te_ref, nu_ref, nx_ref, fx_ref, fx_next_ref, xl_hbm, wgu_ref, wd_ref,
                y_ref, xbuf, sems, wgu_bf, wd_bf):
    j = pl.program_id(0)
    n_used = nu_ref[0]
    de = wd_ref.shape[2]
    slot = j % 2

    def copy(g_src, g_dst, s):
        return pltpu.make_async_copy(_group(xl_hbm, g_src), _group(xbuf.at[s], g_dst), sems.at[s])

    def issue(tab_ref, tile, s):
        def body(k, c):
            copy(tab_ref[0, 0, k], k, s).start()
            return c
        lax.fori_loop(0, nx_ref[tile], body, 0)

    @pl.when(j == 0)
    def _():
        xbuf[...] = jnp.zeros_like(xbuf)
        issue(fx_ref, 0, 0)

    @pl.when(j + 1 < n_used)
    def _():
        issue(fx_next_ref, j + 1, 1 - slot)

    @pl.when(j < n_used)
    def _():
        e = te_ref[j]
        prev = te_ref[jnp.maximum(j - 1, 0)]

        @pl.when(jnp.logical_or(j == 0, e != prev))
        def _():
            wgu_bf[...] = wgu_ref[0, 0].astype(BF16)
            wd_bf[...] = wd_ref[0, 0].astype(BF16)

        _wait_groups(xl_hbm, xbuf.at[slot], sems.at[slot], nx_ref[j], MOE_TILE // SUBLANES)

        gu = _dot(xbuf[slot].astype(BF16), wgu_bf[...])
        a = gu[:, :de]
        b = gu[:, de:]
        mid = (a * jax.nn.sigmoid(a) * b).astype(BF16)
        y_ref[...] = _dot(mid, wd_bf[...])

    @pl.when(j >= n_used)
    def _():
        y_ref[...] = jnp.zeros_like(y_ref)


def _moe_experts(xl, plan, layer, w_gate_up, w_down):
    d = xl.shape[1]
    tm = MOE_TILE
    n_tiles = plan["n_tiles"]
    de = w_down.shape[2]
    per_tile = tm // SUBLANES
    grid_spec = pltpu.PrefetchScalarGridSpec(
        num_scalar_prefetch=3,
        grid=(n_tiles,),
        in_specs=[
            pl.BlockSpec((1, 1, per_tile), lambda j, *_: (j, 0, 0), memory_space=pltpu.SMEM),
            pl.BlockSpec((1, 1, per_tile), lambda j, *_: (jnp.minimum(j + 1, n_tiles - 1), 0, 0),
                         memory_space=pltpu.SMEM),
            pl.BlockSpec(memory_space=pl.ANY),
            pl.BlockSpec((1, 1, d, 2 * de), lambda j, te, *_: (layer, te[j], 0, 0)),
            pl.BlockSpec((1, 1, de, d), lambda j, te, *_: (layer, te[j], 0, 0)),
        ],
        out_specs=pl.BlockSpec((tm, d), lambda j, *_: (j, 0)),
        scratch_shapes=[pltpu.VMEM((2, tm, d), F32), pltpu.SemaphoreType.DMA((2,)),
                        pltpu.VMEM((d, 2 * de), BF16), pltpu.VMEM((de, d), BF16)],
    )
    return pl.pallas_call(
        _moe_kernel,
        grid_spec=grid_spec,
        out_shape=jax.ShapeDtypeStruct((n_tiles * tm, d), F32),
        compiler_params=_cparams(("arbitrary",)),
        name="moe_experts",
    )(plan["tile_expert"], plan["n_used"], plan["n_x"], plan["fetch_x"], plan["fetch_x"], xl, w_gate_up, w_down)


def _combine_kernel(ny_ref, fy_ref, fy_next_ref, h_ref, route_ref, y_hbm, o_ref, ybuf, sems):
    i = pl.program_id(0)
    n = pl.num_programs(0)
    tm = o_ref.shape[0]
    rows = ybuf.shape[1]
    slot = i % 2

    def copy(g_src, g_dst, s):
        return pltpu.make_async_copy(_group(y_hbm, g_src), _group(ybuf.at[s], g_dst), sems.at[s])

    def issue(tab_ref, tile, s):
        def body(k, c):
            copy(tab_ref[0, 0, k], k, s).start()
            return c
        lax.fori_loop(0, ny_ref[tile], body, 0)

    @pl.when(i == 0)
    def _():
        issue(fy_ref, 0, 0)

    @pl.when(i + 1 < n)
    def _():
        issue(fy_next_ref, i + 1, 1 - slot)

    used = ny_ref[i]

    _wait_groups(y_hbm, ybuf.at[slot], sems.at[slot], used, rows // SUBLANES)

    def clear(g, c):
        ybuf[slot, pl.ds(pl.multiple_of(g * SUBLANES, SUBLANES), SUBLANES), :] = jnp.zeros(
            (SUBLANES, ybuf.shape[2]), F32)
        return c
    lax.fori_loop(used, rows // SUBLANES, clear, 0)

    route = route_ref[...]
    route_t = route.T
    row = lax.broadcasted_iota(I32, (rows, tm), 0).astype(F32)
    row_gate = jnp.sum(jnp.where(row == route_t[4:5], route_t[2:3],
                                 jnp.where(row == route_t[5:6], route_t[3:4], 0.0)), axis=1, keepdims=True)
    y = (ybuf[slot] * row_gate).astype(BF16)
    col = lax.broadcasted_iota(I32, (tm, rows), 1).astype(F32)
    pick = jnp.where(col == route[:, 4:5], 1.0, jnp.where(col == route[:, 5:6], 1.0, 0.0)).astype(BF16)
    o_ref[...] = h_ref[...] + _dot(pick, y)


def _moe_combine(h, route, y, plan, tm):
    t, d = h.shape
    n = t // tm
    lg = _local_rows(tm) // SUBLANES
    grid_spec = pltpu.PrefetchScalarGridSpec(
        num_scalar_prefetch=1,
        grid=(n,),
        in_specs=[pl.BlockSpec((1, 1, lg), lambda i, c: (i, 0, 0), memory_space=pltpu.SMEM),
                  pl.BlockSpec((1, 1, lg), lambda i, c: (jnp.minimum(i + 1, n - 1), 0, 0),
                               memory_space=pltpu.SMEM),
                  pl.BlockSpec((tm, d), lambda i, c: (i, 0)), pl.BlockSpec((tm, LANES), lambda i, c: (i, 0)),
                  pl.BlockSpec(memory_space=pl.ANY)],
        out_specs=pl.BlockSpec((tm, d), lambda i, c: (i, 0)),
        scratch_shapes=[pltpu.VMEM((2, _local_rows(tm), d), F32), pltpu.SemaphoreType.DMA((2,))],
    )
    return pl.pallas_call(
        _combine_kernel,
        grid_spec=grid_spec,
        out_shape=jax.ShapeDtypeStruct((t, d), F32),
        compiler_params=_cparams(("arbitrary",)),
        name="moe_combine",
    )(plan["n_y"], plan["fetch_y"], plan["fetch_y"], h, route, y)


def _moe_layer(h, xl, route, cnt, tm, layer, w_gate_up, w_down):
    plan = _moe_plan(cnt, tm)
    y = _moe_experts(xl, plan, layer, w_gate_up, w_down)
    return _moe_combine(h, route, y, plan, tm)


def _rope_block(x, cos_tab, sin_tab):
    lane = lax.broadcasted_iota(I32, x.shape, 1)
    half = QK_ROPE_DIM // 2
    up = pltpu.roll(x, LANES - half, 1)
    down = pltpu.roll(x, half, 1)
    partner = jnp.where(lane < half, up, jnp.where(lane < QK_ROPE_DIM, down, 0.0))
    return x * cos_tab + partner * sin_tab


def _proj_kernel(h_ref, cos_ref, sin_ref, kvg_ref, wdkv_ref, kvlg_ref, wkr_ref, wuk_ref, wuv_ref, kng_ref,
                 qng_ref, wdq_ref, qlg_ref, wuq_ref, qg_ref, hsum_ref, hexp_ref,
                 q_ref, k_ref, v_ref):
    h = h_ref[...]
    cos_tab = cos_ref[...]
    sin_tab = sin_ref[...]
    hsum = hsum_ref[...]
    hexp = hexp_ref[...]

    def head_sums(sq):
        return _dot(sq.astype(BF16), hsum)

    def head_spread(val):
        hi, lo = _split_bf16(val)
        return _dot(hi, hexp) + _dot(lo, hexp)

    hn = _rms(h, kvg_ref[...]).astype(BF16)
    c_kv = _rms(_dot(hn, wdkv_ref[...]), kvlg_ref[...]).astype(BF16)
    kn = _dot(c_kv, wuk_ref[...])
    vv = _dot(c_kv, wuv_ref[...])
    kr = _dot(hn, wkr_ref[...])
    kng = kng_ref[...]
    ss = head_sums(kn * kn) + jnp.sum(kr * kr, axis=-1, keepdims=True)
    inv = lax.rsqrt(ss * (1.0 / QK_DIM) + EPS)
    spread = head_spread(inv)
    kr_rot = _rope_block(kr * kng[:, LANES:], cos_tab, sin_tab)
    for hd in range(N_HEADS):
        cols = slice(hd * LANES, (hd + 1) * LANES)
        f = spread[:, cols]
        k_ref[hd, :, 0:LANES] = (kn[:, cols] * f * kng[:, :LANES]).astype(BF16)
        k_ref[hd, :, LANES:HEAD_PAD] = (kr_rot * f).astype(BF16)
        v_ref[hd] = vv[:, cols].astype(BF16)

    hq = _rms(h, qng_ref[...]).astype(BF16)
    c_q = _rms(_dot(hq, wdq_ref[...]), qlg_ref[...]).astype(BF16)
    qn = _dot(c_q, wuq_ref[:, 0:N_HEADS * LANES])
    qr = _dot(c_q, wuq_ref[:, N_HEADS * LANES:2 * N_HEADS * LANES])
    qg = qg_ref[...]
    ssq = head_sums(qn * qn) + head_sums(qr * qr)
    invq = lax.rsqrt(ssq * (1.0 / QK_DIM) + EPS) * SOFTMAX_SCALE
    spreadq = head_spread(invq)
    for hd in range(N_HEADS):
        cols = slice(hd * LANES, (hd + 1) * LANES)
        f = spreadq[:, cols]
        q_ref[hd, :, 0:LANES] = (qn[:, cols] * f * qg[:, :LANES]).astype(BF16)
        rot = _rope_block(qr[:, cols] * qg[:, LANES:], cos_tab, sin_tab)
        q_ref[hd, :, LANES:HEAD_PAD] = (rot * f).astype(BF16)


def _head_major(w, per_head, lo, hi):
    k = w.shape[0]
    w3 = w.reshape(k, N_HEADS, per_head)[:, :, lo:hi]
    w3 = jnp.pad(w3, ((0, 0), (0, 0), (0, LANES - (hi - lo))))
    return w3.reshape(k, N_HEADS * LANES)


def _mla_project(h, seq, cos_tab, sin_tab, kv_norm_g, w_dkv, kv_latent_g, w_kr, w_ukv, k_norm_g,
                 b_norm_g, w_dq, q_latent_g, w_uq, q_norm_g):
    t, d = h.shape
    tm = min(TOKEN_TILE, seq)
    kv_rank = w_dkv.shape[1]
    q_rank = w_dq.shape[1]
    hw = N_HEADS * LANES
    row = lambda a: a.reshape(1, -1)
    wkr = jnp.pad(w_kr, ((0, 0), (0, LANES - QK_ROPE_DIM))).astype(BF16)
    wuk = _head_major(w_ukv, QK_NOPE_DIM + V_HEAD_DIM, 0, QK_NOPE_DIM).astype(BF16)
    wuv = _head_major(w_ukv, QK_NOPE_DIM + V_HEAD_DIM, QK_NOPE_DIM, QK_NOPE_DIM + V_HEAD_DIM).astype(BF16)
    wuq = jnp.concatenate([_head_major(w_uq, QK_DIM, 0, QK_NOPE_DIM),
                           _head_major(w_uq, QK_DIM, QK_NOPE_DIM, QK_DIM)], axis=1).astype(BF16)
    pad_gain = lambda g: jnp.pad(g, (0, HEAD_PAD - QK_DIM)).reshape(1, HEAD_PAD)
    head_of = jnp.arange(hw, dtype=I32) // LANES
    hsum = (head_of[:, None] == jnp.arange(LANES, dtype=I32)[None, :]).astype(BF16)
    hexp = hsum.T
    tok = lambda w: pl.BlockSpec((tm, w), lambda i: (i, 0))
    heads = lambda w: pl.BlockSpec((N_HEADS, tm, w), lambda i: (0, i, 0))
    return pl.pallas_call(
        _proj_kernel,
        grid=(t // tm,),
        in_specs=[tok(d), tok(LANES), tok(LANES), _full((1, d)), _full((d, kv_rank)), _full((1, kv_rank)),
                  _full((d, LANES)), _full((kv_rank, hw)), _full((kv_rank, hw)), _full((1, HEAD_PAD)),
                  _full((1, d)), _full((d, q_rank)), _full((1, q_rank)), _full((q_rank, 2 * hw)),
                  _full((1, HEAD_PAD)), _full((hw, LANES)), _full((LANES, hw))],
        out_specs=[heads(HEAD_PAD), heads(HEAD_PAD), heads(LANES)],
        out_shape=[jax.ShapeDtypeStruct((N_HEADS, t, HEAD_PAD), BF16),
                   jax.ShapeDtypeStruct((N_HEADS, t, HEAD_PAD), BF16),
                   jax.ShapeDtypeStruct((N_HEADS, t, LANES), BF16)],
        compiler_params=_cparams(("arbitrary",)),
        name="mla_project",
    )(h, cos_tab, sin_tab, row(kv_norm_g), w_dkv.astype(BF16), row(kv_latent_g), wkr, wuk, wuv,
      pad_gain(k_norm_g), row(b_norm_g), w_dq.astype(BF16), row(q_latent_g), wuq, pad_gain(q_norm_g),
      hsum, hexp)


def _attn_kernel(qi_ref, ki_ref, q_ref, k_ref, v_ref, o_ref, m_ref, acc_ref, vext_ref):
    s_idx = pl.program_id(1)
    qi = qi_ref[s_idx]
    ki = ki_ref[s_idx]
    tq = q_ref.shape[1]
    tk = k_ref.shape[1]
    dv = V_HEAD_DIM
    k_lo = ki * tk - qi * tq

    @pl.when(ki == 0)
    def _():
        m_ref[...] = jnp.full_like(m_ref, -jnp.inf)
        acc_ref[...] = jnp.zeros_like(acc_ref)
        vext_ref[:, :, dv:] = jnp.ones((N_HEADS, tk, LANES), BF16)

    vext_ref[:, :, :dv] = v_ref[...]

    def step(masked, row0=0):
        nr = tq - row0
        if masked:
            qc = lax.broadcasted_iota(I32, (nr, tk), 0) // CHUNK
            kc = lax.broadcasted_iota(I32, (nr, tk), 1) // CHUNK
            allowed = kc <= qc
        for hd in range(N_HEADS):
            s = lax.dot_general(q_ref[hd, row0:, :], k_ref[hd], (((1,), (1,)), ((), ())),
                                preferred_element_type=F32)
            if masked:
                s = jnp.where(allowed, s, MASK_VALUE)
            m_prev = m_ref[hd, row0:, :]
            m_new = jnp.maximum(m_prev, jnp.max(s, axis=-1, keepdims=True))
            alpha = jnp.exp(m_prev - m_new)
            p = jnp.exp(s - jnp.concatenate([m_new] * (tk // LANES), axis=1))
            pv = _dot(p.astype(BF16), vext_ref[hd])
            acc_ref[hd, row0:, :] = jnp.concatenate([alpha, alpha], axis=1) * acc_ref[hd, row0:, :] + pv
            m_ref[hd, row0:, :] = m_new

    @pl.when(k_lo < 0)
    def _():
        step(False)

    for r in range(tq // tk):
        @pl.when(k_lo == r * tk)
        def _(r=r):
            step(True, r * tk)

    @pl.when(k_lo + tk == tq)
    def _():
        for hd in range(N_HEADS):
            acc = acc_ref[hd]
            o_ref[:, hd * dv:(hd + 1) * dv] = (acc[:, :dv] / acc[:, dv:]).astype(o_ref.dtype)


def _attention(q, k, v, batch, seq):
    t = q.shape[1]
    tk = min(ATTN_KEY_TILE, seq)
    tq = min(ATTN_QUERY_TILE, seq)
    nq = seq // tq
    nk = seq // tk
    per_q = tq // tk
    pairs = [(a, b) for a in range(nq) for b in range((a + 1) * per_q)]
    qi_tab = jnp.array([p[0] for p in pairs], I32)
    ki_tab = jnp.array([p[1] for p in pairs], I32)
    grid_spec = pltpu.PrefetchScalarGridSpec(
        num_scalar_prefetch=2,
        grid=(batch, len(pairs)),
        in_specs=[
            pl.BlockSpec((N_HEADS, tq, HEAD_PAD), lambda b, s, qi, ki: (0, b * nq + qi[s], 0)),
            pl.BlockSpec((N_HEADS, tk, HEAD_PAD), lambda b, s, qi, ki: (0, b * nk + ki[s], 0)),
            pl.BlockSpec((N_HEADS, tk, V_HEAD_DIM), lambda b, s, qi, ki: (0, b * nk + ki[s], 0)),
        ],
        out_specs=pl.BlockSpec((tq, N_HEADS * V_HEAD_DIM), lambda b, s, qi, ki: (b * nq + qi[s], 0)),
        scratch_shapes=[pltpu.VMEM((N_HEADS, tq, LANES), F32),
                        pltpu.VMEM((N_HEADS, tq, V_HEAD_DIM + LANES), F32),
                        pltpu.VMEM((N_HEADS, tk, V_HEAD_DIM + LANES), BF16)],
    )
    return pl.pallas_call(
        _attn_kernel,
        grid_spec=grid_spec,
        out_shape=jax.ShapeDtypeStruct((t, N_HEADS * V_HEAD_DIM), BF16),
        compiler_params=_cparams(("arbitrary", "arbitrary")),
        name="attention",
    )(qi_tab, ki_tab, q, k, v)


def _oproj_kernel(o_ref, h_ref, wo_ref, fng_ref, wrh_ref, wrl_ref, br_ref, tri_ref, upper_ref,
                  h_out_ref, xl_ref, route_ref, cnt_ref):
    h = h_ref[...] + _dot(o_ref[...], wo_ref[...])
    h_out_ref[...] = h
    xl, route, cnt = _route(h, fng_ref[...], wrh_ref[...], wrl_ref[...], br_ref[...], tri_ref[...], upper_ref[...])
    xl_ref[...] = xl
    route_ref[...] = route
    cnt_ref[0] = cnt


def _oproj_layer(o, h, seq, w_o, fng, wrh, wrl, br):
    t, d = h.shape
    tm = min(TOKEN_TILE, seq)
    r_in, r_out = _router_specs(tm, d)
    tok = lambda w: pl.BlockSpec((tm, w), lambda i: (i, 0))
    return pl.pallas_call(
        _oproj_kernel,
        grid=(t // tm,),
        in_specs=[tok(o.shape[1]), tok(d), _full(w_o.shape)] + r_in,
        out_specs=r_out,
        out_shape=_router_out_shapes(t, tm, d),
        compiler_params=_cparams(("arbitrary",)),
        name="oproj_router",
    )(o, h, w_o.astype(BF16), fng.reshape(1, d), wrh, wrl, br, _strict_lower(tm), _strict_lower(LANES).T)


def kernel(x, positions, a_norm_g, a_pw1_w, a_pw1_b, a_dw_w, a_dw_b, a_ln_g, a_ln_b, a_pw2_w, a_pw2_b, kv_norm_g, w_dkv, kv_latent_g, w_kr, w_ukv, k_norm_g, b_norm_g, w_dq, q_latent_g, w_uq, q_norm_g, w_o, ffn_norm_g, w_group, b_group, w_router, b_router, w_gate_up, w_down):
    batch, seq, d = x.shape
    t = batch * seq
    tm = min(TOKEN_TILE, seq)
    assert a_norm_g.shape[0] == 1 and b_norm_g.shape[0] == 1 and ffn_norm_g.shape[0] == 2
    cos_tab, sin_tab = _rope_tables(positions)

    wrh, wrl, br = _router_weights(w_group[0], b_group[0], w_router[0], b_router[0])
    h, xl, route, cnt = _conv_layer(x.reshape(t, d), seq, a_norm_g[0], a_pw1_w[0], a_pw1_b[0], a_dw_w[0],
                                    a_dw_b[0], a_ln_g[0], a_ln_b[0], a_pw2_w[0], a_pw2_b[0],
                                    ffn_norm_g[0], wrh, wrl, br)
    h = _moe_layer(h, xl, route, cnt, tm, 0, w_gate_up, w_down)

    q, k, v = _mla_project(h, seq, cos_tab, sin_tab, kv_norm_g, w_dkv, kv_latent_g, w_kr, w_ukv, k_norm_g,
                           b_norm_g[0], w_dq[0], q_latent_g[0], w_uq[0], q_norm_g[0])
    o = _attention(q, k, v, batch, seq)
    wrh, wrl, br = _router_weights(w_group[1], b_group[1], w_router[1], b_router[1])
    h, xl, route, cnt = _oproj_layer(o, h, seq, w_o[0], ffn_norm_g[1], wrh, wrl, br)
    h = _moe_layer(h, xl, route, cnt, tm, 1, w_gate_up, w_down)
    return h.reshape(batch, seq, d)
```

```python
import functools

import jax
import jax.numpy as jnp
from jax import lax
from jax.experimental import pallas as pl
from jax.experimental.pallas import tpu as pltpu

F32 = jnp.float32
BF16 = jnp.bfloat16
I32 = jnp.int32

EPS = 1e-6
CHUNK = 64
N_HEADS = 8
QK_NOPE_DIM = 128
QK_ROPE_DIM = 64
QK_DIM = QK_NOPE_DIM + QK_ROPE_DIM
V_HEAD_DIM = 128
ROPE_THETA = 10000.0
SOFTMAX_SCALE = QK_DIM ** -0.5
MASK_VALUE = -1e30
N_GROUPS = 4
EXPERTS_PER_GROUP = 8
N_EXPERTS = N_GROUPS * EXPERTS_PER_GROUP

LANES = 128
SUBLANES = 8
HEAD_PAD = 256
CONV_HALO = 32
CONV_ROWS = 64
TOKEN_TILE = 512
MOE_TILE = 512
ATTN_QUERY_TILE = 1024
ATTN_KEY_TILE = 512
VMEM_LIMIT = 56 * 1024 * 1024


def _cparams(sem):
    return pltpu.CompilerParams(dimension_semantics=sem, vmem_limit_bytes=VMEM_LIMIT)


def _rms(x, g):
    return x * lax.rsqrt(jnp.mean(x * x, axis=-1, keepdims=True) + EPS) * g


def _dot(a, b):
    return jnp.dot(a, b, preferred_element_type=F32)


def _split_bf16(x):
    hi = x.astype(BF16)
    lo = (x - hi.astype(F32)).astype(BF16)
    return hi, lo


def _full(shape):
    return pl.BlockSpec(shape, lambda *_: (0,) * len(shape))


def _local_rows(tm):
    return 2 * tm + N_EXPERTS * SUBLANES


def _rope_kernel(pos_ref, freq_ref, cos_ref, sin_ref):
    ang = pos_ref[...] * freq_ref[...]
    cos_ref[...] = jnp.cos(ang)
    sin_ref[...] = jnp.sin(ang)


def _rope_tables(positions):
    t = positions.size
    half = QK_ROPE_DIM // 2
    per_row = LANES // half
    inv_freq = ROPE_THETA ** (-jnp.arange(0, QK_ROPE_DIM, 2, dtype=F32) / QK_ROPE_DIM)
    pos = jnp.repeat(positions.reshape(t).astype(F32), half).reshape(t // per_row, LANES)
    freq = jnp.tile(inv_freq, per_row).reshape(1, LANES)
    rows = t // per_row
    blk = min(rows, 1024)
    cos, sin = pl.pallas_call(
        _rope_kernel,
        grid=(rows // blk,),
        in_specs=[pl.BlockSpec((blk, LANES), lambda i: (i, 0)), _full((1, LANES))],
        out_specs=[pl.BlockSpec((blk, LANES), lambda i: (i, 0))] * 2,
        out_shape=[jax.ShapeDtypeStruct((rows, LANES), F32)] * 2,
        compiler_params=_cparams(("arbitrary",)),
        name="rope_tables",
    )(pos, freq)
    cos = cos.reshape(t, half)
    sin = sin.reshape(t, half)
    zeros = jnp.zeros((t, LANES - QK_ROPE_DIM), F32)
    cos_tab = jnp.concatenate([cos, cos, zeros], axis=1)
    sin_tab = jnp.concatenate([-sin, sin, zeros], axis=1)
    return cos_tab, sin_tab


def _route(h, fng, wr_hi, wr_lo, br, tri, upper):
    tm = h.shape[0]
    xt = _rms(h, fng)
    x_hi, x_lo = _split_bf16(xt)
    logits = _dot(x_hi, wr_hi) + _dot(x_lo, wr_hi) + _dot(x_hi, wr_lo) + br
    lane = lax.broadcasted_iota(I32, (tm, LANES), 1).astype(F32)
    neg = -jnp.inf
    big = float(LANES)

    gl = jnp.where(lane >= N_EXPERTS, jnp.where(lane < N_EXPERTS + N_GROUPS, logits, neg), neg)
    gmax = jnp.max(gl, axis=1, keepdims=True)
    gidx = jnp.min(jnp.where(gl == gmax, lane, big), axis=1, keepdims=True) - N_EXPERTS
    g_w = 1.0 / jnp.sum(jnp.exp(gl - gmax), axis=1, keepdims=True)

    lo_lane = gidx * EXPERTS_PER_GROUP
    el = jnp.where(lane >= lo_lane, jnp.where(lane < lo_lane + EXPERTS_PER_GROUP, logits, neg), neg)
    m1 = jnp.max(el, axis=1, keepdims=True)
    i1 = jnp.min(jnp.where(el == m1, lane, big), axis=1, keepdims=True)
    el2 = jnp.where(lane == i1, neg, el)
    m2 = jnp.max(el2, axis=1, keepdims=True)
    i2 = jnp.min(jnp.where(el2 == m2, lane, big), axis=1, keepdims=True)
    p2 = jnp.exp(m2 - m1)
    den = 1.0 + p2
    w1 = g_w / den
    w2 = g_w * p2 / den

    sel1 = lane == i1
    sel2 = lane == i2
    onehot = jnp.where(sel1, 1.0, jnp.where(sel2, 1.0, 0.0))
    before = _dot(tri, onehot.astype(BF16))
    cnt = jnp.sum(onehot, axis=0, keepdims=True)
    groups = jnp.floor((cnt + (SUBLANES - 1)) * (1.0 / SUBLANES))
    start = SUBLANES * _dot(jnp.broadcast_to(groups, (SUBLANES, LANES)).astype(BF16), upper)[0:1]
    at = before + start
    l1 = jnp.sum(jnp.where(sel1, at, 0.0), axis=1, keepdims=True)
    l2 = jnp.sum(jnp.where(sel2, at, 0.0), axis=1, keepdims=True)

    route = jnp.where(lane == 0, i1, jnp.where(lane == 1, i2, jnp.where(lane == 2, w1, jnp.where(
        lane == 3, w2, jnp.where(lane == 4, l1, jnp.where(lane == 5, l2, 0.0))))))
    route_t = route.T
    rows = lax.broadcasted_iota(I32, (_local_rows(tm), tm), 0).astype(F32)
    perm = jnp.where(rows == route_t[4:5], 1.0, jnp.where(rows == route_t[5:6], 1.0, 0.0)).astype(BF16)
    return _dot(perm, x_hi), route, cnt


def _router_weights(w_group, b_group, w_router, b_router):
    d = w_group.shape[0]
    pad = LANES - N_EXPERTS - N_GROUPS
    w = jnp.concatenate([w_router, w_group, jnp.zeros((d, pad), F32)], axis=1)
    b = jnp.concatenate([b_router, b_group, jnp.zeros((pad,), F32)]).reshape(1, LANES)
    hi, lo = _split_bf16(w)
    return hi, lo, b


def _strict_lower(n):
    r = lax.broadcasted_iota(I32, (n, n), 0)
    c = lax.broadcasted_iota(I32, (n, n), 1)
    return (c < r).astype(BF16)


def _router_specs(tm, d):
    tok = lambda w: pl.BlockSpec((tm, w), lambda i: (i, 0))
    in_specs = [_full((1, d)), _full((d, LANES)), _full((d, LANES)), _full((1, LANES)), _full((tm, tm)),
                _full((LANES, LANES))]
    out_specs = [tok(d), pl.BlockSpec((_local_rows(tm), d), lambda i: (i, 0)), tok(LANES),
                 pl.BlockSpec((1, 1, LANES), lambda i: (i, 0, 0))]
    return in_specs, out_specs


def _router_out_shapes(t, tm, d):
    n = t // tm
    return [jax.ShapeDtypeStruct((t, d), F32), jax.ShapeDtypeStruct((n * _local_rows(tm), d), F32),
            jax.ShapeDtypeStruct((t, LANES), F32), jax.ShapeDtypeStruct((n, 1, LANES), F32)]


def _conv_kernel(x_ref, ng_ref, w1_ref, b1_ref, dw_ref, dwb_ref, lng_ref, lnb_ref, w2_ref, b2_ref,
                 fng_ref, wrh_ref, wrl_ref, br_ref, tri_ref, upper_ref,
                 h_ref, xl_ref, route_ref, cnt_ref,
                 ubuf, cbuf, *, tiles_per_seq, width):
    i = pl.program_id(0)
    tm, d = x_ref.shape
    n_strips = d // LANES

    @pl.when(i % tiles_per_seq == 0)
    def _():
        ubuf[:, 0:CONV_HALO, :] = jnp.zeros((n_strips, CONV_HALO, LANES), F32)

    @pl.when(i % tiles_per_seq != 0)
    def _():
        ubuf[:, 0:CONV_HALO, :] = ubuf[:, tm:tm + CONV_HALO, :]

    x = x_ref[...]
    hn = _rms(x, ng_ref[...])
    ag = _dot(hn.astype(BF16), w1_ref[...]) + b1_ref[...]
    u = ag[:, :d] * jax.nn.sigmoid(ag[:, d:])
    for c in range(n_strips):
        ubuf[c, CONV_HALO:CONV_HALO + tm, :] = u[:, c * LANES:(c + 1) * LANES]

    first = CONV_HALO - (width - 1)
    for c in range(n_strips):
        cols = slice(c * LANES, (c + 1) * LANES)

        for r in range(tm // CONV_ROWS):
            base = r * CONV_ROWS
            acc = jnp.zeros((CONV_ROWS, LANES), F32)
            for t in range(width):
                acc = acc + dw_ref[t:t + 1, cols] * ubuf[c, base + first + t:base + first + t + CONV_ROWS, :]
            cbuf[base:base + CONV_ROWS, cols] = acc

    v = cbuf[...] + dwb_ref[...]
    mu = jnp.mean(v, axis=-1, keepdims=True)
    vc = v - mu
    var = jnp.mean(vc * vc, axis=-1, keepdims=True)
    y = vc * lax.rsqrt(var + EPS) * lng_ref[...] + lnb_ref[...]
    y = y * jax.nn.sigmoid(y)
    h = x + _dot(y.astype(BF16), w2_ref[...]) + b2_ref[...]
    h_ref[...] = h

    xl, route, cnt = _route(h, fng_ref[...], wrh_ref[...], wrl_ref[...], br_ref[...], tri_ref[...], upper_ref[...])
    xl_ref[...] = xl
    route_ref[...] = route
    cnt_ref[0] = cnt


def _conv_layer(x, seq, ng, w1, b1, dw, dwb, lng, lnb, w2, b2, fng, wrh, wrl, br):
    t, d = x.shape
    tm = min(TOKEN_TILE, seq)
    width = dw.shape[0]
    dw_p = jnp.concatenate([dw, jnp.zeros((CONV_HALO - width, d), F32)], axis=0)
    row = lambda a: a.reshape(1, -1)
    r_in, r_out = _router_specs(tm, d)
    kern = functools.partial(_conv_kernel, tiles_per_seq=seq // tm, width=width)
    return pl.pallas_call(
        kern,
        grid=(t // tm,),
        in_specs=[pl.BlockSpec((tm, d), lambda i: (i, 0)), _full((1, d)), _full((d, 2 * d)), _full((1, 2 * d)),
                  _full((CONV_HALO, d)), _full((1, d)), _full((1, d)), _full((1, d)), _full((d, d)),
                  _full((1, d))] + r_in,
        out_specs=r_out,
        out_shape=_router_out_shapes(t, tm, d),
        scratch_shapes=[pltpu.VMEM((d // LANES, CONV_HALO + tm, LANES), F32), pltpu.VMEM((tm, d), F32)],
        compiler_params=_cparams(("arbitrary",)),
        name="conv_router",
    )(x, row(ng), w1.astype(BF16), row(b1), dw_p, row(dwb), row(lng), row(lnb), w2.astype(BF16), row(b2),
      row(fng), wrh, wrl, br, _strict_lower(tm), _strict_lower(LANES).T)


def _moe_plan(cnt, tm):
    n_tt = cnt.shape[0]
    c = cnt[:, 0, :N_EXPERTS].astype(I32)
    run = (c + SUBLANES - 1) // SUBLANES
    local = jnp.cumsum(run, axis=1) - run
    per_tile = MOE_TILE // SUBLANES
    n_exp = jnp.sum(run, axis=0)
    n_pad = ((n_exp + per_tile - 1) // per_tile) * per_tile
    e_end = jnp.cumsum(n_pad)
    e_base = e_end - n_pad
    dst = e_base[None, :] + jnp.cumsum(run, axis=0) - run
    src = jnp.arange(n_tt, dtype=I32)[:, None] * (_local_rows(tm) // SUBLANES) + local
    max_groups = (2 * n_tt * tm) // SUBLANES + n_tt * N_EXPERTS + N_EXPERTS * (per_tile - 1)
    n_tiles = -(-max_groups // per_tile)
    tile_start = jnp.minimum(jnp.arange(n_tiles, dtype=I32) * per_tile, e_end[-1] - per_tile)
    tile_expert = jnp.sum((tile_start[:, None] >= e_end[None, :]).astype(I32), axis=1)
    g = jnp.arange(n_tiles * per_tile, dtype=I32).reshape(n_tiles, per_tile)
    rs = jnp.take(dst, tile_expert, axis=1)[:, :, None]
    re = rs + jnp.take(run, tile_expert, axis=1)[:, :, None]
    shift = jnp.take(src - dst, tile_expert, axis=1)[:, :, None]
    fetch_x = jnp.sum(jnp.where(jnp.logical_and(rs <= g[None], g[None] < re), shift, 0), axis=0) + g
    n_x = jnp.clip(jnp.take(e_base + n_exp, tile_expert) - g[:, 0], 0, per_tile)
    lg = _local_rows(tm) // SUBLANES
    p = jnp.arange(lg, dtype=I32)[None, :, None]
    lo = local[:, None, :]
    inside = jnp.logical_and(lo <= p, p < lo + run[:, None, :])
    fetch_y = jnp.sum(jnp.where(inside, (dst - local)[:, None, :], 0), axis=2) + p[:, :, 0]
    return dict(fetch_x=fetch_x.reshape(n_tiles, 1, per_tile), n_x=n_x.astype(I32),
                fetch_y=fetch_y.reshape(n_tt, 1, lg), n_y=jnp.sum(run, axis=1).astype(I32),
                tile_expert=tile_expert, n_used=(e_end[-1] // per_tile).reshape(1).astype(I32), n_tiles=n_tiles)


def _group(ref, g):
    return ref.at[pl.ds(pl.multiple_of(g * SUBLANES, SUBLANES), SUBLANES)]


ISSUE_UNROLL = 4


def _issue_groups(make_copy, count):
    full = count // ISSUE_UNROLL

    def trip(q, c):
        for u in range(ISSUE_UNROLL):
            make_copy(q * ISSUE_UNROLL + u).start()
        return c

    def single(k, c):
        make_copy(k).start()
        return c

    lax.fori_loop(0, full, trip, 0)
    lax.fori_loop(full * ISSUE_UNROLL, count, single, 0)


def _wait_groups(src_hbm, dst, sem, count, max_count):
    bit = 1
    while bit <= max_count:
        @pl.when((count & bit) != 0)
        def _(bit=bit):
            n = bit * SUBLANES
            pltpu.make_async_copy(src_hbm.at[pl.ds(0, n)], dst.at[pl.ds(0, n)], sem).wait()
        bit *= 2


def _moe_kernel(te_ref, nu_ref, nx_ref, fx_ref, fx_next_ref, xl_hbm, wgu_ref, wd_ref,
                y_ref, xbuf, sems, wgu_bf, wd_bf):
    j = pl.program_id(0)
    n_used = nu_ref[0]
    de = wd_ref.shape[2]
    slot = j % 2

    def copy(g_src, g_dst, s):
        return pltpu.make_async_copy(_group(xl_hbm, g_src), _group(xbuf.at[s], g_dst), sems.at[s])

    def issue(tab_ref, tile, s):
        _issue_groups(lambda k: copy(tab_ref[0, 0, k], k, s), nx_ref[tile])

    @pl.when(j == 0)
    def _():
        xbuf[...] = jnp.zeros_like(xbuf)
        issue(fx_ref, 0, 0)

    @pl.when(j + 1 < n_used)
    def _():
        issue(fx_next_ref, j + 1, 1 - slot)

    @pl.when(j < n_used)
    def _():
        e = te_ref[j]
        prev = te_ref[jnp.maximum(j - 1, 0)]

        @pl.when(jnp.logical_or(j == 0, e != prev))
        def _():
            wgu_bf[...] = wgu_ref[0, 0].astype(BF16)
            wd_bf[...] = wd_ref[0, 0].astype(BF16)

        _wait_groups(xl_hbm, xbuf.at[slot], sems.at[slot], nx_ref[j], MOE_TILE // SUBLANES)

        gu = _dot(xbuf[slot].astype(BF16), wgu_bf[...])
        a = gu[:, :de]
        b = gu[:, de:]
        mid = (a * jax.nn.sigmoid(a) * b).astype(BF16)
        y_ref[...] = _dot(mid, wd_bf[...])

    @pl.when(j >= n_used)
    def _():
        y_ref[...] = jnp.zeros_like(y_ref)


def _moe_experts(xl, plan, layer, w_gate_up, w_down):
    d = xl.shape[1]
    tm = MOE_TILE
    n_tiles = plan["n_tiles"]
    de = w_down.shape[2]
    per_tile = tm // SUBLANES
    grid_spec = pltpu.PrefetchScalarGridSpec(
        num_scalar_prefetch=3,
        grid=(n_tiles,),
        in_specs=[
            pl.BlockSpec((1, 1, per_tile), lambda j, *_: (j, 0, 0), memory_space=pltpu.SMEM),
            pl.BlockSpec((1, 1, per_tile), lambda j, *_: (jnp.minimum(j + 1, n_tiles - 1), 0, 0),
                         memory_space=pltpu.SMEM),
            pl.BlockSpec(memory_space=pl.ANY),
            pl.BlockSpec((1, 1, d, 2 * de), lambda j, te, *_: (layer, te[j], 0, 0)),
            pl.BlockSpec((1, 1, de, d), lambda j, te, *_: (layer, te[j], 0, 0)),
        ],
        out_specs=pl.BlockSpec((tm, d), lambda j, *_: (j, 0)),
        scratch_shapes=[pltpu.VMEM((2, tm, d), F32), pltpu.SemaphoreType.DMA((2,)),
                        pltpu.VMEM((d, 2 * de), BF16), pltpu.VMEM((de, d), BF16)],
    )
    return pl.pallas_call(
        _moe_kernel,
        grid_spec=grid_spec,
        out_shape=jax.ShapeDtypeStruct((n_tiles * tm, d), F32),
        compiler_params=_cparams(("arbitrary",)),
        name="moe_experts",
    )(plan["tile_expert"], plan["n_used"], plan["n_x"], plan["fetch_x"], plan["fetch_x"], xl, w_gate_up, w_down)


def _combine_kernel(ny_ref, fy_ref, fy_next_ref, h_ref, route_ref, y_hbm, o_ref, ybuf, sems):
    i = pl.program_id(0)
    n = pl.num_programs(0)
    tm = o_ref.shape[0]
    rows = ybuf.shape[1]
    slot = i % 2

    def copy(g_src, g_dst, s):
        return pltpu.make_async_copy(_group(y_hbm, g_src), _group(ybuf.at[s], g_dst), sems.at[s])

    def issue(tab_ref, tile, s):
        _issue_groups(lambda k: copy(tab_ref[0, 0, k], k, s), ny_ref[tile])

    @pl.when(i == 0)
    def _():
        issue(fy_ref, 0, 0)

    @pl.when(i + 1 < n)
    def _():
        issue(fy_next_ref, i + 1, 1 - slot)

    used = ny_ref[i]

    _wait_groups(y_hbm, ybuf.at[slot], sems.at[slot], used, rows // SUBLANES)

    def clear(g, c):
        ybuf[slot, pl.ds(pl.multiple_of(g * SUBLANES, SUBLANES), SUBLANES), :] = jnp.zeros(
            (SUBLANES, ybuf.shape[2]), F32)
        return c
    lax.fori_loop(used, rows // SUBLANES, clear, 0)

    route = route_ref[...]
    route_t = route.T
    row = lax.broadcasted_iota(I32, (rows, tm), 0).astype(F32)
    row_gate = jnp.sum(jnp.where(row == route_t[4:5], route_t[2:3],
                                 jnp.where(row == route_t[5:6], route_t[3:4], 0.0)), axis=1, keepdims=True)
    y = (ybuf[slot] * row_gate).astype(BF16)
    col = lax.broadcasted_iota(I32, (tm, rows), 1).astype(F32)
    pick = jnp.where(col == route[:, 4:5], 1.0, jnp.where(col == route[:, 5:6], 1.0, 0.0)).astype(BF16)
    o_ref[...] = h_ref[...] + _dot(pick, y)


def _moe_combine(h, route, y, plan, tm):
    t, d = h.shape
    n = t // tm
    lg = _local_rows(tm) // SUBLANES
    grid_spec = pltpu.PrefetchScalarGridSpec(
        num_scalar_prefetch=1,
        grid=(n,),
        in_specs=[pl.BlockSpec((1, 1, lg), lambda i, c: (i, 0, 0), memory_space=pltpu.SMEM),
                  pl.BlockSpec((1, 1, lg), lambda i, c: (jnp.minimum(i + 1, n - 1), 0, 0),
                               memory_space=pltpu.SMEM),
                  pl.BlockSpec((tm, d), lambda i, c: (i, 0)), pl.BlockSpec((tm, LANES), lambda i, c: (i, 0)),
                  pl.BlockSpec(memory_space=pl.ANY)],
        out_specs=pl.BlockSpec((tm, d), lambda i, c: (i, 0)),
        scratch_shapes=[pltpu.VMEM((2, _local_rows(tm), d), F32), pltpu.SemaphoreType.DMA((2,))],
    )
    return pl.pallas_call(
        _combine_kernel,
        grid_spec=grid_spec,
        out_shape=jax.ShapeDtypeStruct((t, d), F32),
        compiler_params=_cparams(("arbitrary",)),
        name="moe_combine",
    )(plan["n_y"], plan["fetch_y"], plan["fetch_y"], h, route, y)


def _moe_layer(h, xl, route, cnt, tm, layer, w_gate_up, w_down):
    plan = _moe_plan(cnt, tm)
    y = _moe_experts(xl, plan, layer, w_gate_up, w_down)
    return _moe_combine(h, route, y, plan, tm)


def _rope_block(x, cos_tab, sin_tab):
    lane = lax.broadcasted_iota(I32, x.shape, 1)
    half = QK_ROPE_DIM // 2
    up = pltpu.roll(x, LANES - half, 1)
    down = pltpu.roll(x, half, 1)
    partner = jnp.where(lane < half, up, jnp.where(lane < QK_ROPE_DIM, down, 0.0))
    return x * cos_tab + partner * sin_tab


def _proj_kernel(h_ref, cos_ref, sin_ref, kvg_ref, wdkv_ref, kvlg_ref, wkr_ref, wuk_ref, wuv_ref, kng_ref,
                 qng_ref, wdq_ref, qlg_ref, wuq_ref, qg_ref, hsum_ref, hexp_ref,
                 q_ref, k_ref, v_ref):
    h = h_ref[...]
    cos_tab = cos_ref[...]
    sin_tab = sin_ref[...]
    hsum = hsum_ref[...]
    hexp = hexp_ref[...]

    def head_sums(sq):
        return _dot(sq.astype(BF16), hsum)

    def head_spread(val):
        hi, lo = _split_bf16(val)
        return _dot(hi, hexp) + _dot(lo, hexp)

    hn = _rms(h, kvg_ref[...]).astype(BF16)
    c_kv = _rms(_dot(hn, wdkv_ref[...]), kvlg_ref[...]).astype(BF16)
    kn = _dot(c_kv, wuk_ref[...])
    vv = _dot(c_kv, wuv_ref[...])
    kr = _dot(hn, wkr_ref[...])
    kng = kng_ref[...]
    ss = head_sums(kn * kn) + jnp.sum(kr * kr, axis=-1, keepdims=True)
    inv = lax.rsqrt(ss * (1.0 / QK_DIM) + EPS)
    spread = head_spread(inv)
    kr_rot = _rope_block(kr * kng[:, LANES:], cos_tab, sin_tab)
    for hd in range(N_HEADS):
        cols = slice(hd * LANES, (hd + 1) * LANES)
        f = spread[:, cols]
        k_ref[hd, :, 0:LANES] = (kn[:, cols] * f * kng[:, :LANES]).astype(BF16)
        k_ref[hd, :, LANES:HEAD_PAD] = (kr_rot * f).astype(BF16)
        v_ref[hd] = vv[:, cols].astype(BF16)

    hq = _rms(h, qng_ref[...]).astype(BF16)
    c_q = _rms(_dot(hq, wdq_ref[...]), qlg_ref[...]).astype(BF16)
    qn = _dot(c_q, wuq_ref[:, 0:N_HEADS * LANES])
    qr = _dot(c_q, wuq_ref[:, N_HEADS * LANES:2 * N_HEADS * LANES])
    qg = qg_ref[...]
    ssq = head_sums(qn * qn) + head_sums(qr * qr)
    invq = lax.rsqrt(ssq * (1.0 / QK_DIM) + EPS) * SOFTMAX_SCALE
    spreadq = head_spread(invq)
    for hd in range(N_HEADS):
        cols = slice(hd * LANES, (hd + 1) * LANES)
        f = spreadq[:, cols]
        q_ref[hd, :, 0:LANES] = (qn[:, cols] * f * qg[:, :LANES]).astype(BF16)
        rot = _rope_block(qr[:, cols] * qg[:, LANES:], cos_tab, sin_tab)
        q_ref[hd, :, LANES:HEAD_PAD] = (rot * f).astype(BF16)


def _head_major(w, per_head, lo, hi):
    k = w.shape[0]
    w3 = w.reshape(k, N_HEADS, per_head)[:, :, lo:hi]
    w3 = jnp.pad(w3, ((0, 0), (0, 0), (0, LANES - (hi - lo))))
    return w3.reshape(k, N_HEADS * LANES)


def _mla_project(h, seq, cos_tab, sin_tab, kv_norm_g, w_dkv, kv_latent_g, w_kr, w_ukv, k_norm_g,
                 b_norm_g, w_dq, q_latent_g, w_uq, q_norm_g):
    t, d = h.shape
    tm = min(TOKEN_TILE, seq)
    kv_rank = w_dkv.shape[1]
    q_rank = w_dq.shape[1]
    hw = N_HEADS * LANES
    row = lambda a: a.reshape(1, -1)
    wkr = jnp.pad(w_kr, ((0, 0), (0, LANES - QK_ROPE_DIM))).astype(BF16)
    wuk = _head_major(w_ukv, QK_NOPE_DIM + V_HEAD_DIM, 0, QK_NOPE_DIM).astype(BF16)
    wuv = _head_major(w_ukv, QK_NOPE_DIM + V_HEAD_DIM, QK_NOPE_DIM, QK_NOPE_DIM + V_HEAD_DIM).astype(BF16)
    wuq = jnp.concatenate([_head_major(w_uq, QK_DIM, 0, QK_NOPE_DIM),
                           _head_major(w_uq, QK_DIM, QK_NOPE_DIM, QK_DIM)], axis=1).astype(BF16)
    pad_gain = lambda g: jnp.pad(g, (0, HEAD_PAD - QK_DIM)).reshape(1, HEAD_PAD)
    head_of = jnp.arange(hw, dtype=I32) // LANES
    hsum = (head_of[:, None] == jnp.arange(LANES, dtype=I32)[None, :]).astype(BF16)
    hexp = hsum.T
    tok = lambda w: pl.BlockSpec((tm, w), lambda i: (i, 0))
    heads = lambda w: pl.BlockSpec((N_HEADS, tm, w), lambda i: (0, i, 0))
    return pl.pallas_call(
        _proj_kernel,
        grid=(t // tm,),
        in_specs=[tok(d), tok(LANES), tok(LANES), _full((1, d)), _full((d, kv_rank)), _full((1, kv_rank)),
                  _full((d, LANES)), _full((kv_rank, hw)), _full((kv_rank, hw)), _full((1, HEAD_PAD)),
                  _full((1, d)), _full((d, q_rank)), _full((1, q_rank)), _full((q_rank, 2 * hw)),
                  _full((1, HEAD_PAD)), _full((hw, LANES)), _full((LANES, hw))],
        out_specs=[heads(HEAD_PAD), heads(HEAD_PAD), heads(LANES)],
        out_shape=[jax.ShapeDtypeStruct((N_HEADS, t, HEAD_PAD), BF16),
                   jax.ShapeDtypeStruct((N_HEADS, t, HEAD_PAD), BF16),
                   jax.ShapeDtypeStruct((N_HEADS, t, LANES), BF16)],
        compiler_params=_cparams(("arbitrary",)),
        name="mla_project",
    )(h, cos_tab, sin_tab, row(kv_norm_g), w_dkv.astype(BF16), row(kv_latent_g), wkr, wuk, wuv,
      pad_gain(k_norm_g), row(b_norm_g), w_dq.astype(BF16), row(q_latent_g), wuq, pad_gain(q_norm_g),
      hsum, hexp)


def _attn_kernel(qi_ref, ki_ref, q_ref, k_ref, v_ref, o_ref, m_ref, acc_ref, vext_ref):
    s_idx = pl.program_id(1)
    qi = qi_ref[s_idx]
    ki = ki_ref[s_idx]
    tq = q_ref.shape[1]
    tk = k_ref.shape[1]
    dv = V_HEAD_DIM
    k_lo = ki * tk - qi * tq

    @pl.when(ki == 0)
    def _():
        m_ref[...] = jnp.full_like(m_ref, -jnp.inf)
        acc_ref[...] = jnp.zeros_like(acc_ref)
        vext_ref[:, :, dv:] = jnp.ones((N_HEADS, tk, LANES), BF16)

    vext_ref[:, :, :dv] = v_ref[...]

    def step(masked, row0=0):
        nr = tq - row0
        if masked:
            qc = lax.broadcasted_iota(I32, (nr, tk), 0) // CHUNK
            kc = lax.broadcasted_iota(I32, (nr, tk), 1) // CHUNK
            allowed = kc <= qc
        for hd in range(N_HEADS):
            s = lax.dot_general(q_ref[hd, row0:, :], k_ref[hd], (((1,), (1,)), ((), ())),
                                preferred_element_type=F32)
            if masked:
                s = jnp.where(allowed, s, MASK_VALUE)
            m_prev = m_ref[hd, row0:, :]
            m_new = jnp.maximum(m_prev, jnp.max(s, axis=-1, keepdims=True))
            alpha = jnp.exp(m_prev - m_new)
            p = jnp.exp(s - jnp.concatenate([m_new] * (tk // LANES), axis=1))
            pv = _dot(p.astype(BF16), vext_ref[hd])
            acc_ref[hd, row0:, :] = jnp.concatenate([alpha, alpha], axis=1) * acc_ref[hd, row0:, :] + pv
            m_ref[hd, row0:, :] = m_new

    @pl.when(k_lo < 0)
    def _():
        step(False)

    for r in range(tq // tk):
        @pl.when(k_lo == r * tk)
        def _(r=r):
            step(True, r * tk)

    @pl.when(k_lo + tk == tq)
    def _():
        for hd in range(N_HEADS):
            acc = acc_ref[hd]
            o_ref[:, hd * dv:(hd + 1) * dv] = (acc[:, :dv] / acc[:, dv:]).astype(o_ref.dtype)


def _attention(q, k, v, batch, seq):
    t = q.shape[1]
    tk = min(ATTN_KEY_TILE, seq)
    tq = min(ATTN_QUERY_TILE, seq)
    nq = seq // tq
    nk = seq // tk
    per_q = tq // tk
    pairs = [(a, b) for a in range(nq) for b in range((a + 1) * per_q)]
    qi_tab = jnp.array([p[0] for p in pairs], I32)
    ki_tab = jnp.array([p[1] for p in pairs], I32)
    grid_spec = pltpu.PrefetchScalarGridSpec(
        num_scalar_prefetch=2,
        grid=(batch, len(pairs)),
        in_specs=[
            pl.BlockSpec((N_HEADS, tq, HEAD_PAD), lambda b, s, qi, ki: (0, b * nq + qi[s], 0)),
            pl.BlockSpec((N_HEADS, tk, HEAD_PAD), lambda b, s, qi, ki: (0, b * nk + ki[s], 0)),
            pl.BlockSpec((N_HEADS, tk, V_HEAD_DIM), lambda b, s, qi, ki: (0, b * nk + ki[s], 0)),
        ],
        out_specs=pl.BlockSpec((tq, N_HEADS * V_HEAD_DIM), lambda b, s, qi, ki: (b * nq + qi[s], 0)),
        scratch_shapes=[pltpu.VMEM((N_HEADS, tq, LANES), F32),
                        pltpu.VMEM((N_HEADS, tq, V_HEAD_DIM + LANES), F32),
                        pltpu.VMEM((N_HEADS, tk, V_HEAD_DIM + LANES), BF16)],
    )
    return pl.pallas_call(
        _attn_kernel,
        grid_spec=grid_spec,
        out_shape=jax.ShapeDtypeStruct((t, N_HEADS * V_HEAD_DIM), BF16),
        compiler_params=_cparams(("arbitrary", "arbitrary")),
        name="attention",
    )(qi_tab, ki_tab, q, k, v)


def _oproj_kernel(o_ref, h_ref, wo_ref, fng_ref, wrh_ref, wrl_ref, br_ref, tri_ref, upper_ref,
                  h_out_ref, xl_ref, route_ref, cnt_ref):
    h = h_ref[...] + _dot(o_ref[...], wo_ref[...])
    h_out_ref[...] = h
    xl, route, cnt = _route(h, fng_ref[...], wrh_ref[...], wrl_ref[...], br_ref[...], tri_ref[...], upper_ref[...])
    xl_ref[...] = xl
    route_ref[...] = route
    cnt_ref[0] = cnt


def _oproj_layer(o, h, seq, w_o, fng, wrh, wrl, br):
    t, d = h.shape
    tm = min(TOKEN_TILE, seq)
    r_in, r_out = _router_specs(tm, d)
    tok = lambda w: pl.BlockSpec((tm, w), lambda i: (i, 0))
    return pl.pallas_call(
        _oproj_kernel,
        grid=(t // tm,),
        in_specs=[tok(o.shape[1]), tok(d), _full(w_o.shape)] + r_in,
        out_specs=r_out,
        out_shape=_router_out_shapes(t, tm, d),
        compiler_params=_cparams(("arbitrary",)),
        name="oproj_router",
    )(o, h, w_o.astype(BF16), fng.reshape(1, d), wrh, wrl, br, _strict_lower(tm), _strict_lower(LANES).T)


def kernel(x, positions, a_norm_g, a_pw1_w, a_pw1_b, a_dw_w, a_dw_b, a_ln_g, a_ln_b, a_pw2_w, a_pw2_b, kv_norm_g, w_dkv, kv_latent_g, w_kr, w_ukv, k_norm_g, b_norm_g, w_dq, q_latent_g, w_uq, q_norm_g, w_o, ffn_norm_g, w_group, b_group, w_router, b_router, w_gate_up, w_down):
    batch, seq, d = x.shape
    t = batch * seq
    tm = min(TOKEN_TILE, seq)
    assert a_norm_g.shape[0] == 1 and b_norm_g.shape[0] == 1 and ffn_norm_g.shape[0] == 2
    cos_tab, sin_tab = _rope_tables(positions)

    wrh, wrl, br = _router_weights(w_group[0], b_group[0], w_router[0], b_router[0])
    h, xl, route, cnt = _conv_layer(x.reshape(t, d), seq, a_norm_g[0], a_pw1_w[0], a_pw1_b[0], a_dw_w[0],
                                    a_dw_b[0], a_ln_g[0], a_ln_b[0], a_pw2_w[0], a_pw2_b[0],
                                    ffn_norm_g[0], wrh, wrl, br)
    h = _moe_layer(h, xl, route, cnt, tm, 0, w_gate_up, w_down)

    q, k, v = _mla_project(h, seq, cos_tab, sin_tab, kv_norm_g, w_dkv, kv_latent_g, w_kr, w_ukv, k_norm_g,
                           b_norm_g[0], w_dq[0], q_latent_g[0], w_uq[0], q_norm_g[0])
    o = _attention(q, k, v, batch, seq)
    wrh, wrl, br = _router_weights(w_group[1], b_group[1], w_router[1], b_router[1])
    h, xl, route, cnt = _oproj_layer(o, h, seq, w_o[0], ffn_norm_g[1], wrh, wrl, br)
    h = _moe_layer(h, xl, route, cnt, tm, 1, w_gate_up, w_down)
    return h.reshape(batch, seq, d)
```

```python
import functools

import jax
import jax.numpy as jnp
from jax import lax
from jax.experimental import pallas as pl
from jax.experimental.pallas import tpu as pltpu

F32 = jnp.float32
BF16 = jnp.bfloat16
I32 = jnp.int32

EPS = 1e-6
CHUNK = 64
N_HEADS = 8
QK_NOPE_DIM = 128
QK_ROPE_DIM = 64
QK_DIM = QK_NOPE_DIM + QK_ROPE_DIM
V_HEAD_DIM = 128
ROPE_THETA = 10000.0
SOFTMAX_SCALE = QK_DIM ** -0.5
MASK_VALUE = -1e30
N_GROUPS = 4
EXPERTS_PER_GROUP = 8
N_EXPERTS = N_GROUPS * EXPERTS_PER_GROUP

LANES = 128
SUBLANES = 8
HEAD_PAD = 256
CONV_HALO = 32
CONV_ROWS = 64
TOKEN_TILE = 512
MOE_TILE = 512
ATTN_QUERY_TILE = 1024
ATTN_KEY_TILE = 512
VMEM_LIMIT = 56 * 1024 * 1024


def _cparams(sem):
    return pltpu.CompilerParams(dimension_semantics=sem, vmem_limit_bytes=VMEM_LIMIT)


def _rms(x, g):
    return x * lax.rsqrt(jnp.mean(x * x, axis=-1, keepdims=True) + EPS) * g


def _dot(a, b):
    return jnp.dot(a, b, preferred_element_type=F32)


def _split_bf16(x):
    hi = x.astype(BF16)
    lo = (x - hi.astype(F32)).astype(BF16)
    return hi, lo


def _full(shape):
    return pl.BlockSpec(shape, lambda *_: (0,) * len(shape))


def _local_rows(tm):
    return 2 * tm + N_EXPERTS * SUBLANES


def _rope_kernel(pos_ref, freq_ref, cos_ref, sin_ref):
    ang = pos_ref[...] * freq_ref[...]
    cos_ref[...] = jnp.cos(ang)
    sin_ref[...] = jnp.sin(ang)


def _rope_tables(positions):
    t = positions.size
    half = QK_ROPE_DIM // 2
    per_row = LANES // half
    inv_freq = ROPE_THETA ** (-jnp.arange(0, QK_ROPE_DIM, 2, dtype=F32) / QK_ROPE_DIM)
    pos = jnp.repeat(positions.reshape(t).astype(F32), half).reshape(t // per_row, LANES)
    freq = jnp.tile(inv_freq, per_row).reshape(1, LANES)
    rows = t // per_row
    blk = min(rows, 1024)
    cos, sin = pl.pallas_call(
        _rope_kernel,
        grid=(rows // blk,),
        in_specs=[pl.BlockSpec((blk, LANES), lambda i: (i, 0)), _full((1, LANES))],
        out_specs=[pl.BlockSpec((blk, LANES), lambda i: (i, 0))] * 2,
        out_shape=[jax.ShapeDtypeStruct((rows, LANES), F32)] * 2,
        compiler_params=_cparams(("arbitrary",)),
        name="rope_tables",
    )(pos, freq)
    cos = cos.reshape(t, half)
    sin = sin.reshape(t, half)
    zeros = jnp.zeros((t, LANES - QK_ROPE_DIM), F32)
    cos_tab = jnp.concatenate([cos, cos, zeros], axis=1)
    sin_tab = jnp.concatenate([-sin, sin, zeros], axis=1)
    return cos_tab, sin_tab


def _route(h, fng, wr_hi, wr_lo, br, tri, upper):
    tm = h.shape[0]
    xt = _rms(h, fng)
    x_hi, x_lo = _split_bf16(xt)
    logits = _dot(x_hi, wr_hi) + _dot(x_lo, wr_hi) + _dot(x_hi, wr_lo) + br
    lane = lax.broadcasted_iota(I32, (tm, LANES), 1).astype(F32)
    neg = -jnp.inf
    big = float(LANES)

    gl = jnp.where(lane >= N_EXPERTS, jnp.where(lane < N_EXPERTS + N_GROUPS, logits, neg), neg)
    gmax = jnp.max(gl, axis=1, keepdims=True)
    gidx = jnp.min(jnp.where(gl == gmax, lane, big), axis=1, keepdims=True) - N_EXPERTS
    g_w = 1.0 / jnp.sum(jnp.exp(gl - gmax), axis=1, keepdims=True)

    lo_lane = gidx * EXPERTS_PER_GROUP
    el = jnp.where(lane >= lo_lane, jnp.where(lane < lo_lane + EXPERTS_PER_GROUP, logits, neg), neg)
    m1 = jnp.max(el, axis=1, keepdims=True)
    i1 = jnp.min(jnp.where(el == m1, lane, big), axis=1, keepdims=True)
    el2 = jnp.where(lane == i1, neg, el)
    m2 = jnp.max(el2, axis=1, keepdims=True)
    i2 = jnp.min(jnp.where(el2 == m2, lane, big), axis=1, keepdims=True)
    p2 = jnp.exp(m2 - m1)
    den = 1.0 + p2
    w1 = g_w / den
    w2 = g_w * p2 / den

    sel1 = lane == i1
    sel2 = lane == i2
    onehot = jnp.where(sel1, 1.0, jnp.where(sel2, 1.0, 0.0))
    before = _dot(tri, onehot.astype(BF16))
    cnt = jnp.sum(onehot, axis=0, keepdims=True)
    groups = jnp.floor((cnt + (SUBLANES - 1)) * (1.0 / SUBLANES))
    start = SUBLANES * _dot(jnp.broadcast_to(groups, (SUBLANES, LANES)).astype(BF16), upper)[0:1]
    at = before + start
    l1 = jnp.sum(jnp.where(sel1, at, 0.0), axis=1, keepdims=True)
    l2 = jnp.sum(jnp.where(sel2, at, 0.0), axis=1, keepdims=True)

    route = jnp.where(lane == 0, i1, jnp.where(lane == 1, i2, jnp.where(lane == 2, w1, jnp.where(
        lane == 3, w2, jnp.where(lane == 4, l1, jnp.where(lane == 5, l2, 0.0))))))
    route_t = route.T
    rows = lax.broadcasted_iota(I32, (_local_rows(tm), tm), 0).astype(F32)
    perm = jnp.where(rows == route_t[4:5], 1.0, jnp.where(rows == route_t[5:6], 1.0, 0.0)).astype(BF16)
    return _dot(perm, x_hi), route, cnt


def _router_weights(w_group, b_group, w_router, b_router):
    d = w_group.shape[0]
    pad = LANES - N_EXPERTS - N_GROUPS
    w = jnp.concatenate([w_router, w_group, jnp.zeros((d, pad), F32)], axis=1)
    b = jnp.concatenate([b_router, b_group, jnp.zeros((pad,), F32)]).reshape(1, LANES)
    hi, lo = _split_bf16(w)
    return hi, lo, b


def _strict_lower(n):
    r = lax.broadcasted_iota(I32, (n, n), 0)
    c = lax.broadcasted_iota(I32, (n, n), 1)
    return (c < r).astype(BF16)


def _router_specs(tm, d):
    tok = lambda w: pl.BlockSpec((tm, w), lambda i: (i, 0))
    in_specs = [_full((1, d)), _full((d, LANES)), _full((d, LANES)), _full((1, LANES)), _full((tm, tm)),
                _full((LANES, LANES))]
    out_specs = [tok(d), pl.BlockSpec((_local_rows(tm), d), lambda i: (i, 0)), tok(LANES),
                 pl.BlockSpec((1, 1, LANES), lambda i: (i, 0, 0))]
    return in_specs, out_specs


def _router_out_shapes(t, tm, d):
    n = t // tm
    return [jax.ShapeDtypeStruct((t, d), F32), jax.ShapeDtypeStruct((n * _local_rows(tm), d), F32),
            jax.ShapeDtypeStruct((t, LANES), F32), jax.ShapeDtypeStruct((n, 1, LANES), F32)]


def _conv_kernel(x_ref, ng_ref, w1_ref, b1_ref, dw_ref, dwb_ref, lng_ref, lnb_ref, w2_ref, b2_ref,
                 fng_ref, wrh_ref, wrl_ref, br_ref, tri_ref, upper_ref,
                 h_ref, xl_ref, route_ref, cnt_ref,
                 ubuf, cbuf, *, tiles_per_seq, width):
    i = pl.program_id(0)
    tm, d = x_ref.shape
    n_strips = d // LANES

    @pl.when(i % tiles_per_seq == 0)
    def _():
        ubuf[:, 0:CONV_HALO, :] = jnp.zeros((n_strips, CONV_HALO, LANES), F32)

    @pl.when(i % tiles_per_seq != 0)
    def _():
        ubuf[:, 0:CONV_HALO, :] = ubuf[:, tm:tm + CONV_HALO, :]

    x = x_ref[...]
    hn = _rms(x, ng_ref[...])
    ag = _dot(hn.astype(BF16), w1_ref[...]) + b1_ref[...]
    u = ag[:, :d] * jax.nn.sigmoid(ag[:, d:])
    for c in range(n_strips):
        ubuf[c, CONV_HALO:CONV_HALO + tm, :] = u[:, c * LANES:(c + 1) * LANES]

    first = CONV_HALO - (width - 1)
    for c in range(n_strips):
        cols = slice(c * LANES, (c + 1) * LANES)

        for r in range(tm // CONV_ROWS):
            base = r * CONV_ROWS
            acc = jnp.zeros((CONV_ROWS, LANES), F32)
            for t in range(width):
                acc = acc + dw_ref[t:t + 1, cols] * ubuf[c, base + first + t:base + first + t + CONV_ROWS, :]
            cbuf[base:base + CONV_ROWS, cols] = acc

    v = cbuf[...] + dwb_ref[...]
    mu = jnp.mean(v, axis=-1, keepdims=True)
    vc = v - mu
    var = jnp.mean(vc * vc, axis=-1, keepdims=True)
    y = vc * lax.rsqrt(var + EPS) * lng_ref[...] + lnb_ref[...]
    y = y * jax.nn.sigmoid(y)
    h = x + _dot(y.astype(BF16), w2_ref[...]) + b2_ref[...]
    h_ref[...] = h

    xl, route, cnt = _route(h, fng_ref[...], wrh_ref[...], wrl_ref[...], br_ref[...], tri_ref[...], upper_ref[...])
    xl_ref[...] = xl
    route_ref[...] = route
    cnt_ref[0] = cnt


def _conv_layer(x, seq, ng, w1, b1, dw, dwb, lng, lnb, w2, b2, fng, wrh, wrl, br):
    t, d = x.shape
    tm = min(TOKEN_TILE, seq)
    width = dw.shape[0]
    dw_p = jnp.concatenate([dw, jnp.zeros((CONV_HALO - width, d), F32)], axis=0)
    row = lambda a: a.reshape(1, -1)
    r_in, r_out = _router_specs(tm, d)
    kern = functools.partial(_conv_kernel, tiles_per_seq=seq // tm, width=width)
    return pl.pallas_call(
        kern,
        grid=(t // tm,),
        in_specs=[pl.BlockSpec((tm, d), lambda i: (i, 0)), _full((1, d)), _full((d, 2 * d)), _full((1, 2 * d)),
                  _full((CONV_HALO, d)), _full((1, d)), _full((1, d)), _full((1, d)), _full((d, d)),
                  _full((1, d))] + r_in,
        out_specs=r_out,
        out_shape=_router_out_shapes(t, tm, d),
        scratch_shapes=[pltpu.VMEM((d // LANES, CONV_HALO + tm, LANES), F32), pltpu.VMEM((tm, d), F32)],
        compiler_params=_cparams(("arbitrary",)),
        name="conv_router",
    )(x, row(ng), w1.astype(BF16), row(b1), dw_p, row(dwb), row(lng), row(lnb), w2.astype(BF16), row(b2),
      row(fng), wrh, wrl, br, _strict_lower(tm), _strict_lower(LANES).T)


def _moe_plan(cnt, tm):
    n_tt = cnt.shape[0]
    c = cnt[:, 0, :N_EXPERTS].astype(I32)
    run = (c + SUBLANES - 1) // SUBLANES
    local = jnp.cumsum(run, axis=1) - run
    per_tile = MOE_TILE // SUBLANES
    n_exp = jnp.sum(run, axis=0)
    n_pad = ((n_exp + per_tile - 1) // per_tile) * per_tile
    e_end = jnp.cumsum(n_pad)
    e_base = e_end - n_pad
    dst = e_base[None, :] + jnp.cumsum(run, axis=0) - run
    src = jnp.arange(n_tt, dtype=I32)[:, None] * (_local_rows(tm) // SUBLANES) + local
    max_groups = (2 * n_tt * tm) // SUBLANES + n_tt * N_EXPERTS + N_EXPERTS * (per_tile - 1)
    n_tiles = -(-max_groups // per_tile)
    tile_start = jnp.minimum(jnp.arange(n_tiles, dtype=I32) * per_tile, e_end[-1] - per_tile)
    tile_expert = jnp.sum((tile_start[:, None] >= e_end[None, :]).astype(I32), axis=1)
    g = jnp.arange(n_tiles * per_tile, dtype=I32).reshape(n_tiles, per_tile)
    rs = jnp.take(dst, tile_expert, axis=1)[:, :, None]
    re = rs + jnp.take(run, tile_expert, axis=1)[:, :, None]
    shift = jnp.take(src - dst, tile_expert, axis=1)[:, :, None]
    fetch_x = jnp.sum(jnp.where(jnp.logical_and(rs <= g[None], g[None] < re), shift, 0), axis=0) + g
    n_x = jnp.clip(jnp.take(e_base + n_exp, tile_expert) - g[:, 0], 0, per_tile)
    lg = _local_rows(tm) // SUBLANES
    p = jnp.arange(lg, dtype=I32)[None, :, None]
    lo = local[:, None, :]
    inside = jnp.logical_and(lo <= p, p < lo + run[:, None, :])
    fetch_y = jnp.sum(jnp.where(inside, (dst - local)[:, None, :], 0), axis=2) + p[:, :, 0]
    return dict(fetch_x=fetch_x.reshape(n_tiles, 1, per_tile), n_x=n_x.astype(I32),
                fetch_y=fetch_y.reshape(n_tt, 1, lg), n_y=jnp.sum(run, axis=1).astype(I32),
                tile_expert=tile_expert, n_used=(e_end[-1] // per_tile).reshape(1).astype(I32), n_tiles=n_tiles)


def _group(ref, g):
    return ref.at[pl.ds(pl.multiple_of(g * SUBLANES, SUBLANES), SUBLANES)]


ISSUE_UNROLL = 4


def _issue_groups(make_copy, count):
    full = count // ISSUE_UNROLL

    def trip(q, c):
        for u in range(ISSUE_UNROLL):
            make_copy(q * ISSUE_UNROLL + u).start(priority=u % 2)
        return c

    def single(k, c):
        make_copy(k).start()
        return c

    lax.fori_loop(0, full, trip, 0)
    lax.fori_loop(full * ISSUE_UNROLL, count, single, 0)


def _wait_groups(src_hbm, dst, sem, count, max_count):
    bit = 1
    while bit <= max_count:
        @pl.when((count & bit) != 0)
        def _(bit=bit):
            n = bit * SUBLANES
            pltpu.make_async_copy(src_hbm.at[pl.ds(0, n)], dst.at[pl.ds(0, n)], sem).wait()
        bit *= 2


def _moe_kernel(te_ref, nu_ref, nx_ref, fx_ref, fx_next_ref, xl_hbm, wgu_ref, wd_ref,
                y_ref, xbuf, sems, wgu_bf, wd_bf):
    j = pl.program_id(0)
    n_used = nu_ref[0]
    de = wd_ref.shape[2]
    slot = j % 2

    def copy(g_src, g_dst, s):
        return pltpu.make_async_copy(_group(xl_hbm, g_src), _group(xbuf.at[s], g_dst), sems.at[s])

    def issue(tab_ref, tile, s):
        _issue_groups(lambda k: copy(tab_ref[0, 0, k], k, s), nx_ref[tile])

    @pl.when(j == 0)
    def _():
        xbuf[...] = jnp.zeros_like(xbuf)
        issue(fx_ref, 0, 0)

    @pl.when(j + 1 < n_used)
    def _():
        issue(fx_next_ref, j + 1, 1 - slot)

    @pl.when(j < n_used)
    def _():
        e = te_ref[j]
        prev = te_ref[jnp.maximum(j - 1, 0)]

        @pl.when(jnp.logical_or(j == 0, e != prev))
        def _():
            wgu_bf[...] = wgu_ref[0, 0].astype(BF16)
            wd_bf[...] = wd_ref[0, 0].astype(BF16)

        _wait_groups(xl_hbm, xbuf.at[slot], sems.at[slot], nx_ref[j], MOE_TILE // SUBLANES)

        gu = _dot(xbuf[slot].astype(BF16), wgu_bf[...])
        a = gu[:, :de]
        b = gu[:, de:]
        mid = (a * jax.nn.sigmoid(a) * b).astype(BF16)
        y_ref[...] = _dot(mid, wd_bf[...])

    @pl.when(j >= n_used)
    def _():
        y_ref[...] = jnp.zeros_like(y_ref)


def _moe_experts(xl, plan, layer, w_gate_up, w_down):
    d = xl.shape[1]
    tm = MOE_TILE
    n_tiles = plan["n_tiles"]
    de = w_down.shape[2]
    per_tile = tm // SUBLANES
    grid_spec = pltpu.PrefetchScalarGridSpec(
        num_scalar_prefetch=3,
        grid=(n_tiles,),
        in_specs=[
            pl.BlockSpec((1, 1, per_tile), lambda j, *_: (j, 0, 0), memory_space=pltpu.SMEM),
            pl.BlockSpec((1, 1, per_tile), lambda j, *_: (jnp.minimum(j + 1, n_tiles - 1), 0, 0),
                         memory_space=pltpu.SMEM),
            pl.BlockSpec(memory_space=pl.ANY),
            pl.BlockSpec((1, 1, d, 2 * de), lambda j, te, *_: (layer, te[j], 0, 0)),
            pl.BlockSpec((1, 1, de, d), lambda j, te, *_: (layer, te[j], 0, 0)),
        ],
        out_specs=pl.BlockSpec((tm, d), lambda j, *_: (j, 0)),
        scratch_shapes=[pltpu.VMEM((2, tm, d), F32), pltpu.SemaphoreType.DMA((2,)),
                        pltpu.VMEM((d, 2 * de), BF16), pltpu.VMEM((de, d), BF16)],
    )
    return pl.pallas_call(
        _moe_kernel,
        grid_spec=grid_spec,
        out_shape=jax.ShapeDtypeStruct((n_tiles * tm, d), F32),
        compiler_params=_cparams(("arbitrary",)),
        name="moe_experts",
    )(plan["tile_expert"], plan["n_used"], plan["n_x"], plan["fetch_x"], plan["fetch_x"], xl, w_gate_up, w_down)


def _combine_kernel(ny_ref, fy_ref, fy_next_ref, h_ref, route_ref, y_hbm, o_ref, ybuf, sems):
    i = pl.program_id(0)
    n = pl.num_programs(0)
    tm = o_ref.shape[0]
    rows = ybuf.shape[1]
    slot = i % 2

    def copy(g_src, g_dst, s):
        return pltpu.make_async_copy(_group(y_hbm, g_src), _group(ybuf.at[s], g_dst), sems.at[s])

    def issue(tab_ref, tile, s):
        _issue_groups(lambda k: copy(tab_ref[0, 0, k], k, s), ny_ref[tile])

    @pl.when(i == 0)
    def _():
        issue(fy_ref, 0, 0)

    @pl.when(i + 1 < n)
    def _():
        issue(fy_next_ref, i + 1, 1 - slot)

    used = ny_ref[i]

    _wait_groups(y_hbm, ybuf.at[slot], sems.at[slot], used, rows // SUBLANES)

    def clear(g, c):
        ybuf[slot, pl.ds(pl.multiple_of(g * SUBLANES, SUBLANES), SUBLANES), :] = jnp.zeros(
            (SUBLANES, ybuf.shape[2]), F32)
        return c
    lax.fori_loop(used, rows // SUBLANES, clear, 0)

    route = route_ref[...]
    route_t = route.T
    row = lax.broadcasted_iota(I32, (rows, tm), 0).astype(F32)
    row_gate = jnp.sum(jnp.where(row == route_t[4:5], route_t[2:3],
                                 jnp.where(row == route_t[5:6], route_t[3:4], 0.0)), axis=1, keepdims=True)
    y = (ybuf[slot] * row_gate).astype(BF16)
    col = lax.broadcasted_iota(I32, (tm, rows), 1).astype(F32)
    pick = jnp.where(col == route[:, 4:5], 1.0, jnp.where(col == route[:, 5:6], 1.0, 0.0)).astype(BF16)
    o_ref[...] = h_ref[...] + _dot(pick, y)


def _moe_combine(h, route, y, plan, tm):
    t, d = h.shape
    n = t // tm
    lg = _local_rows(tm) // SUBLANES
    grid_spec = pltpu.PrefetchScalarGridSpec(
        num_scalar_prefetch=1,
        grid=(n,),
        in_specs=[pl.BlockSpec((1, 1, lg), lambda i, c: (i, 0, 0), memory_space=pltpu.SMEM),
                  pl.BlockSpec((1, 1, lg), lambda i, c: (jnp.minimum(i + 1, n - 1), 0, 0),
                               memory_space=pltpu.SMEM),
                  pl.BlockSpec((tm, d), lambda i, c: (i, 0)), pl.BlockSpec((tm, LANES), lambda i, c: (i, 0)),
                  pl.BlockSpec(memory_space=pl.ANY)],
        out_specs=pl.BlockSpec((tm, d), lambda i, c: (i, 0)),
        scratch_shapes=[pltpu.VMEM((2, _local_rows(tm), d), F32), pltpu.SemaphoreType.DMA((2,))],
    )
    return pl.pallas_call(
        _combine_kernel,
        grid_spec=grid_spec,
        out_shape=jax.ShapeDtypeStruct((t, d), F32),
        compiler_params=_cparams(("arbitrary",)),
        name="moe_combine",
    )(plan["n_y"], plan["fetch_y"], plan["fetch_y"], h, route, y)


def _moe_layer(h, xl, route, cnt, tm, layer, w_gate_up, w_down):
    plan = _moe_plan(cnt, tm)
    y = _moe_experts(xl, plan, layer, w_gate_up, w_down)
    return _moe_combine(h, route, y, plan, tm)


def _rope_block(x, cos_tab, sin_tab):
    lane = lax.broadcasted_iota(I32, x.shape, 1)
    half = QK_ROPE_DIM // 2
    up = pltpu.roll(x, LANES - half, 1)
    down = pltpu.roll(x, half, 1)
    partner = jnp.where(lane < half, up, jnp.where(lane < QK_ROPE_DIM, down, 0.0))
    return x * cos_tab + partner * sin_tab


def _proj_kernel(h_ref, cos_ref, sin_ref, kvg_ref, wdkv_ref, kvlg_ref, wkr_ref, wuk_ref, wuv_ref, kng_ref,
                 qng_ref, wdq_ref, qlg_ref, wuq_ref, qg_ref, hsum_ref, hexp_ref,
                 q_ref, k_ref, v_ref):
    h = h_ref[...]
    cos_tab = cos_ref[...]
    sin_tab = sin_ref[...]
    hsum = hsum_ref[...]
    hexp = hexp_ref[...]

    def head_sums(sq):
        return _dot(sq.astype(BF16), hsum)

    def head_spread(val):
        hi, lo = _split_bf16(val)
        return _dot(hi, hexp) + _dot(lo, hexp)

    hn = _rms(h, kvg_ref[...]).astype(BF16)
    c_kv = _rms(_dot(hn, wdkv_ref[...]), kvlg_ref[...]).astype(BF16)
    kn = _dot(c_kv, wuk_ref[...])
    vv = _dot(c_kv, wuv_ref[...])
    kr = _dot(hn, wkr_ref[...])
    kng = kng_ref[...]
    ss = head_sums(kn * kn) + jnp.sum(kr * kr, axis=-1, keepdims=True)
    inv = lax.rsqrt(ss * (1.0 / QK_DIM) + EPS)
    spread = head_spread(inv)
    kr_rot = _rope_block(kr * kng[:, LANES:], cos_tab, sin_tab)
    for hd in range(N_HEADS):
        cols = slice(hd * LANES, (hd + 1) * LANES)
        f = spread[:, cols]
        k_ref[hd, :, 0:LANES] = (kn[:, cols] * f * kng[:, :LANES]).astype(BF16)
        k_ref[hd, :, LANES:HEAD_PAD] = (kr_rot * f).astype(BF16)
        v_ref[hd] = vv[:, cols].astype(BF16)

    hq = _rms(h, qng_ref[...]).astype(BF16)
    c_q = _rms(_dot(hq, wdq_ref[...]), qlg_ref[...]).astype(BF16)
    qn = _dot(c_q, wuq_ref[:, 0:N_HEADS * LANES])
    qr = _dot(c_q, wuq_ref[:, N_HEADS * LANES:2 * N_HEADS * LANES])
    qg = qg_ref[...]
    ssq = head_sums(qn * qn) + head_sums(qr * qr)
    invq = lax.rsqrt(ssq * (1.0 / QK_DIM) + EPS) * SOFTMAX_SCALE
    spreadq = head_spread(invq)
    for hd in range(N_HEADS):
        cols = slice(hd * LANES, (hd + 1) * LANES)
        f = spreadq[:, cols]
        q_ref[hd, :, 0:LANES] = (qn[:, cols] * f * qg[:, :LANES]).astype(BF16)
        rot = _rope_block(qr[:, cols] * qg[:, LANES:], cos_tab, sin_tab)
        q_ref[hd, :, LANES:HEAD_PAD] = (rot * f).astype(BF16)


def _head_major(w, per_head, lo, hi):
    k = w.shape[0]
    w3 = w.reshape(k, N_HEADS, per_head)[:, :, lo:hi]
    w3 = jnp.pad(w3, ((0, 0), (0, 0), (0, LANES - (hi - lo))))
    return w3.reshape(k, N_HEADS * LANES)


def _mla_project(h, seq, cos_tab, sin_tab, kv_norm_g, w_dkv, kv_latent_g, w_kr, w_ukv, k_norm_g,
                 b_norm_g, w_dq, q_latent_g, w_uq, q_norm_g):
    t, d = h.shape
    tm = min(TOKEN_TILE, seq)
    kv_rank = w_dkv.shape[1]
    q_rank = w_dq.shape[1]
    hw = N_HEADS * LANES
    row = lambda a: a.reshape(1, -1)
    wkr = jnp.pad(w_kr, ((0, 0), (0, LANES - QK_ROPE_DIM))).astype(BF16)
    wuk = _head_major(w_ukv, QK_NOPE_DIM + V_HEAD_DIM, 0, QK_NOPE_DIM).astype(BF16)
    wuv = _head_major(w_ukv, QK_NOPE_DIM + V_HEAD_DIM, QK_NOPE_DIM, QK_NOPE_DIM + V_HEAD_DIM).astype(BF16)
    wuq = jnp.concatenate([_head_major(w_uq, QK_DIM, 0, QK_NOPE_DIM),
                           _head_major(w_uq, QK_DIM, QK_NOPE_DIM, QK_DIM)], axis=1).astype(BF16)
    pad_gain = lambda g: jnp.pad(g, (0, HEAD_PAD - QK_DIM)).reshape(1, HEAD_PAD)
    head_of = jnp.arange(hw, dtype=I32) // LANES
    hsum = (head_of[:, None] == jnp.arange(LANES, dtype=I32)[None, :]).astype(BF16)
    hexp = hsum.T
    tok = lambda w: pl.BlockSpec((tm, w), lambda i: (i, 0))
    heads = lambda w: pl.BlockSpec((N_HEADS, tm, w), lambda i: (0, i, 0))
    return pl.pallas_call(
        _proj_kernel,
        grid=(t // tm,),
        in_specs=[tok(d), tok(LANES), tok(LANES), _full((1, d)), _full((d, kv_rank)), _full((1, kv_rank)),
                  _full((d, LANES)), _full((kv_rank, hw)), _full((kv_rank, hw)), _full((1, HEAD_PAD)),
                  _full((1, d)), _full((d, q_rank)), _full((1, q_rank)), _full((q_rank, 2 * hw)),
                  _full((1, HEAD_PAD)), _full((hw, LANES)), _full((LANES, hw))],
        out_specs=[heads(HEAD_PAD), heads(HEAD_PAD), heads(LANES)],
        out_shape=[jax.ShapeDtypeStruct((N_HEADS, t, HEAD_PAD), BF16),
                   jax.ShapeDtypeStruct((N_HEADS, t, HEAD_PAD), BF16),
                   jax.ShapeDtypeStruct((N_HEADS, t, LANES), BF16)],
        compiler_params=_cparams(("arbitrary",)),
        name="mla_project",
    )(h, cos_tab, sin_tab, row(kv_norm_g), w_dkv.astype(BF16), row(kv_latent_g), wkr, wuk, wuv,
      pad_gain(k_norm_g), row(b_norm_g), w_dq.astype(BF16), row(q_latent_g), wuq, pad_gain(q_norm_g),
      hsum, hexp)


def _attn_kernel(qi_ref, ki_ref, q_ref, k_ref, v_ref, o_ref, m_ref, acc_ref, vext_ref):
    s_idx = pl.program_id(1)
    qi = qi_ref[s_idx]
    ki = ki_ref[s_idx]
    tq = q_ref.shape[1]
    tk = k_ref.shape[1]
    dv = V_HEAD_DIM
    k_lo = ki * tk - qi * tq

    @pl.when(ki == 0)
    def _():
        m_ref[...] = jnp.full_like(m_ref, -jnp.inf)
        acc_ref[...] = jnp.zeros_like(acc_ref)
        vext_ref[:, :, dv:] = jnp.ones((N_HEADS, tk, LANES), BF16)

    vext_ref[:, :, :dv] = v_ref[...]

    def step(masked, row0=0):
        nr = tq - row0
        if masked:
            qc = lax.broadcasted_iota(I32, (nr, tk), 0) // CHUNK
            kc = lax.broadcasted_iota(I32, (nr, tk), 1) // CHUNK
            allowed = kc <= qc
        for hd in range(N_HEADS):
            s = lax.dot_general(q_ref[hd, row0:, :], k_ref[hd], (((1,), (1,)), ((), ())),
                                preferred_element_type=F32)
            if masked:
                s = jnp.where(allowed, s, MASK_VALUE)
            m_prev = m_ref[hd, row0:, :]
            m_new = jnp.maximum(m_prev, jnp.max(s, axis=-1, keepdims=True))
            alpha = jnp.exp(m_prev - m_new)
            p = jnp.exp(s - jnp.concatenate([m_new] * (tk // LANES), axis=1))
            pv = _dot(p.astype(BF16), vext_ref[hd])
            acc_ref[hd, row0:, :] = jnp.concatenate([alpha, alpha], axis=1) * acc_ref[hd, row0:, :] + pv
            m_ref[hd, row0:, :] = m_new

    @pl.when(k_lo < 0)
    def _():
        step(False)

    for r in range(tq // tk):
        @pl.when(k_lo == r * tk)
        def _(r=r):
            step(True, r * tk)

    @pl.when(k_lo + tk == tq)
    def _():
        for hd in range(N_HEADS):
            acc = acc_ref[hd]
            o_ref[:, hd * dv:(hd + 1) * dv] = (acc[:, :dv] / acc[:, dv:]).astype(o_ref.dtype)


def _attention(q, k, v, batch, seq):
    t = q.shape[1]
    tk = min(ATTN_KEY_TILE, seq)
    tq = min(ATTN_QUERY_TILE, seq)
    nq = seq // tq
    nk = seq // tk
    per_q = tq // tk
    pairs = [(a, b) for a in range(nq) for b in range((a + 1) * per_q)]
    qi_tab = jnp.array([p[0] for p in pairs], I32)
    ki_tab = jnp.array([p[1] for p in pairs], I32)
    grid_spec = pltpu.PrefetchScalarGridSpec(
        num_scalar_prefetch=2,
        grid=(batch, len(pairs)),
        in_specs=[
            pl.BlockSpec((N_HEADS, tq, HEAD_PAD), lambda b, s, qi, ki: (0, b * nq + qi[s], 0)),
            pl.BlockSpec((N_HEADS, tk, HEAD_PAD), lambda b, s, qi, ki: (0, b * nk + ki[s], 0)),
            pl.BlockSpec((N_HEADS, tk, V_HEAD_DIM), lambda b, s, qi, ki: (0, b * nk + ki[s], 0)),
        ],
        out_specs=pl.BlockSpec((tq, N_HEADS * V_HEAD_DIM), lambda b, s, qi, ki: (b * nq + qi[s], 0)),
        scratch_shapes=[pltpu.VMEM((N_HEADS, tq, LANES), F32),
                        pltpu.VMEM((N_HEADS, tq, V_HEAD_DIM + LANES), F32),
                        pltpu.VMEM((N_HEADS, tk, V_HEAD_DIM + LANES), BF16)],
    )
    return pl.pallas_call(
        _attn_kernel,
        grid_spec=grid_spec,
        out_shape=jax.ShapeDtypeStruct((t, N_HEADS * V_HEAD_DIM), BF16),
        compiler_params=_cparams(("arbitrary", "arbitrary")),
        name="attention",
    )(qi_tab, ki_tab, q, k, v)


def _oproj_kernel(o_ref, h_ref, wo_ref, fng_ref, wrh_ref, wrl_ref, br_ref, tri_ref, upper_ref,
                  h_out_ref, xl_ref, route_ref, cnt_ref):
    h = h_ref[...] + _dot(o_ref[...], wo_ref[...])
    h_out_ref[...] = h
    xl, route, cnt = _route(h, fng_ref[...], wrh_ref[...], wrl_ref[...], br_ref[...], tri_ref[...], upper_ref[...])
    xl_ref[...] = xl
    route_ref[...] = route
    cnt_ref[0] = cnt


def _oproj_layer(o, h, seq, w_o, fng, wrh, wrl, br):
    t, d = h.shape
    tm = min(TOKEN_TILE, seq)
    r_in, r_out = _router_specs(tm, d)
    tok = lambda w: pl.BlockSpec((tm, w), lambda i: (i, 0))
    return pl.pallas_call(
        _oproj_kernel,
        grid=(t // tm,),
        in_specs=[tok(o.shape[1]), tok(d), _full(w_o.shape)] + r_in,
        out_specs=r_out,
        out_shape=_router_out_shapes(t, tm, d),
        compiler_params=_cparams(("arbitrary",)),
        name="oproj_router",
    )(o, h, w_o.astype(BF16), fng.reshape(1, d), wrh, wrl, br, _strict_lower(tm), _strict_lower(LANES).T)


def kernel(x, positions, a_norm_g, a_pw1_w, a_pw1_b, a_dw_w, a_dw_b, a_ln_g, a_ln_b, a_pw2_w, a_pw2_b, kv_norm_g, w_dkv, kv_latent_g, w_kr, w_ukv, k_norm_g, b_norm_g, w_dq, q_latent_g, w_uq, q_norm_g, w_o, ffn_norm_g, w_group, b_group, w_router, b_router, w_gate_up, w_down):
    batch, seq, d = x.shape
    t = batch * seq
    tm = min(TOKEN_TILE, seq)
    assert a_norm_g.shape[0] == 1 and b_norm_g.shape[0] == 1 and ffn_norm_g.shape[0] == 2
    cos_tab, sin_tab = _rope_tables(positions)

    wrh, wrl, br = _router_weights(w_group[0], b_group[0], w_router[0], b_router[0])
    h, xl, route, cnt = _conv_layer(x.reshape(t, d), seq, a_norm_g[0], a_pw1_w[0], a_pw1_b[0], a_dw_w[0],
                                    a_dw_b[0], a_ln_g[0], a_ln_b[0], a_pw2_w[0], a_pw2_b[0],
                                    ffn_norm_g[0], wrh, wrl, br)
    h = _moe_layer(h, xl, route, cnt, tm, 0, w_gate_up, w_down)

    q, k, v = _mla_project(h, seq, cos_tab, sin_tab, kv_norm_g, w_dkv, kv_latent_g, w_kr, w_ukv, k_norm_g,
                           b_norm_g[0], w_dq[0], q_latent_g[0], w_uq[0], q_norm_g[0])
    o = _attention(q, k, v, batch, seq)
    wrh, wrl, br = _router_weights(w_group[1], b_group[1], w_router[1], b_router[1])
    h, xl, route, cnt = _oproj_layer(o, h, seq, w_o[0], ffn_norm_g[1], wrh, wrl, br)
    h = _moe_layer(h, xl, route, cnt, tm, 1, w_gate_up, w_down)
    return h.reshape(batch, seq, d)
```
